```python
import math
import jax
import jax.numpy as jnp
from jax import lax
import numpy as np

D_MODEL = 1024
BATCH = 32
SEQ = 2048
DEPTH = 2

CHUNK = 64
Q_BLOCK = 128
ROPE_THETA = 10000.0
NORM_EPS = 1e-6

DA_HEADS = 4
DA_QK_DIM = 64
DA_V_DIM = 2 * DA_QK_DIM
DA_WIDTH = DA_HEADS * DA_V_DIM

HG_HEADS = 4
HG_DIM = 128
HG_WIDTH = HG_HEADS * HG_DIM

RW_HEADS = 8
RW_DIM = 64
RW_WIDTH = RW_HEADS * RW_DIM
RW_DECAY_LORA = 64
RW_A_LORA = 64
RW_V_LORA = 32
RW_GATE_LORA = 128
RW_GN_EPS = 64e-5
RW_COLS = (RW_WIDTH, RW_WIDTH, RW_WIDTH, RW_DECAY_LORA, RW_A_LORA, RW_GATE_LORA)
RW_SHIFT_WIDTH = sum(RW_COLS)
RW_SPLIT_IDX = tuple(np.cumsum(RW_COLS)[:-1].tolist())

N_BRANCHES = 3
FFN_HIDDEN = -(-8 * D_MODEL // (3 * 256)) * 256
ADA_WIDTH = 6 * D_MODEL

COL_SIZES = (
    2 * DA_HEADS * DA_QK_DIM, 2 * DA_HEADS * DA_QK_DIM, DA_WIDTH,
    HG_WIDTH, HG_WIDTH, HG_WIDTH, HG_WIDTH,
    RW_SHIFT_WIDTH,
    N_BRANCHES * D_MODEL,
)
IN_COLS = sum(COL_SIZES)
SPLIT_IDX = tuple(np.cumsum(COL_SIZES)[:-1].tolist())

kernel_name = "chunk_causal_hybrid_diffattn_hgrn2_rwkv7"


def rms_norm(x, w, eps=NORM_EPS):
    xf = x.astype(jnp.float32)
    y = xf * lax.rsqrt(jnp.mean(xf * xf, axis=-1, keepdims=True) + eps)
    return (y * w.astype(jnp.float32)).astype(x.dtype)


def rope_tables(positions):
    inv_freq = ROPE_THETA ** (-jnp.arange(0, DA_QK_DIM, 2, dtype=jnp.float32) / DA_QK_DIM)
    ang = positions.astype(jnp.float32)[..., None] * inv_freq
    return jnp.cos(ang), jnp.sin(ang)


def apply_rope(x, cos, sin):
    cos = cos[:, :, None, None, :]
    sin = sin[:, :, None, None, :]
    x1, x2 = jnp.split(x, 2, axis=-1)
    return jnp.concatenate([x1 * cos - x2 * sin, x2 * cos + x1 * sin], axis=-1)


def diff_attention(q, k, v, cos, sin, lam_vecs, subln_w, layer_idx):
    Bn, Sn = q.shape[0], q.shape[1]
    q = apply_rope(q.reshape(Bn, Sn, DA_HEADS, 2, DA_QK_DIM).astype(jnp.float32), cos, sin) * (DA_QK_DIM ** -0.5)
    k = apply_rope(k.reshape(Bn, Sn, DA_HEADS, 2, DA_QK_DIM).astype(jnp.float32), cos, sin)
    v = v.reshape(Bn, Sn, DA_HEADS, DA_V_DIM).astype(jnp.float32)
    lam_init = 0.8 - 0.6 * math.exp(-0.3 * layer_idx)
    lv = lam_vecs.astype(jnp.float32)
    lam = jnp.exp(jnp.sum(lv[0] * lv[1])) - jnp.exp(jnp.sum(lv[2] * lv[3])) + lam_init
    outs = []
    for blk in range(Sn // Q_BLOCK):
        start = blk * Q_BLOCK
        end = start + Q_BLOCK
        s = jnp.einsum('bqhme,bkhme->bhmqk', q[:, start:end], k[:, :end])
        q_chunk = (start + jnp.arange(Q_BLOCK)) // CHUNK
        k_chunk = jnp.arange(end) // CHUNK
        allowed = k_chunk[None, :] <= q_chunk[:, None]
        p = jax.nn.softmax(jnp.where(allowed, s, -jnp.inf), axis=-1)
        pd = p[:, :, 0] - lam * p[:, :, 1]
        outs.append(jnp.einsum('bhqk,bkhd->bqhd', pd, v[:, :end]))
    o = jnp.concatenate(outs, axis=1)
    o = rms_norm(o, subln_w) * (1.0 - lam_init)
    return o.reshape(Bn, Sn, DA_WIDTH)


def hgrn2(f_pre, i_in, q_in, g_in, lb, norm_w):
    Bn, Sn = f_pre.shape[0], f_pre.shape[1]
    nc = Sn // CHUNK
    z = f_pre.astype(jnp.float32)
    lb = lb.astype(jnp.float32)
    log_f = jnp.logaddexp(jnp.log(lb), jnp.log1p(-lb) + jax.nn.log_sigmoid(z))
    key = (1.0 - lb) * jax.nn.sigmoid(-z)

    def to_chunks(t):
        return t.astype(jnp.float32).reshape(Bn, nc, CHUNK, HG_HEADS, HG_DIM).transpose(1, 0, 3, 2, 4)

    causal = jnp.tril(jnp.ones((CHUNK, CHUNK), dtype=bool))

    def chunk_step(state, inp):
        qc, kc, vc, lfc = inp
        b = jnp.cumsum(lfc, axis=2)
        rel = jnp.where(causal[:, :, None], b[:, :, :, None, :] - b[:, :, None, :, :], -jnp.inf)
        scores = jnp.einsum('bhqd,bhkd,bhqkd->bhqk', qc, kc, jnp.exp(rel))
        o_intra = jnp.einsum('bhqk,bhkv->bhqv', scores, vc)
        o_inter = jnp.einsum('bhqd,bhdv->bhqv', qc * jnp.exp(b), state)
        b_last = b[:, :, -1:, :]
        new_state = jnp.exp(b_last)[:, :, 0, :, None] * state + jnp.einsum(
            'bhkd,bhkv->bhdv', kc * jnp.exp(b_last - b), vc)
        return new_state, o_intra + o_inter

    s0 = jnp.zeros((Bn, HG_HEADS, HG_DIM, HG_DIM), jnp.float32)
    _, o = lax.scan(chunk_step, s0, (to_chunks(q_in), to_chunks(key), to_chunks(i_in), to_chunks(log_f)))
    o = o.transpose(1, 0, 3, 2, 4).reshape(Bn, Sn, HG_HEADS, HG_DIM)
    g = jax.nn.silu(g_in.astype(jnp.float32)).reshape(Bn, Sn, HG_HEADS, HG_DIM)
    return (rms_norm(o, norm_w) * g).reshape(Bn, Sn, HG_WIDTH)


def rwkv7(u, mu, w0, w2, a0, a2, g2, k_k, k_a, r_k, gn_w, gn_b, v_first, v_res):
    Bn, Sn = u.shape[0], u.shape[1]
    u = u.astype(jnp.float32)
    u_prev = jnp.pad(u, ((0, 0), (1, 0), (0, 0)))[:, :-1]
    u = u + (u_prev - u) * mu
    r, k, v, w_lo, a_lo, g_lo = jnp.split(u, RW_SPLIT_IDX, axis=-1)
    w = -jax.nn.softplus(-(w0 + jnp.tanh(w_lo) @ w2)) - 0.5
    decay = jnp.exp(-jnp.exp(w))
    a = jax.nn.sigmoid(a0 + a_lo @ a2)
    g = jax.nn.sigmoid(g_lo) @ g2
    if v_res is None:
        v_first = v
    else:
        v0, v1, v2 = v_res
        v = v + (v_first - v) * jax.nn.sigmoid(v0 + (v @ v1) @ v2)

    def heads(t):
        return t.reshape(Bn, Sn, RW_HEADS, RW_DIM)

    kk = heads(k * k_k)
    kk = kk * lax.rsqrt(jnp.maximum(jnp.sum(kk * kk, axis=-1, keepdims=True), 1e-24))
    k = k * (1.0 + (a - 1.0) * k_a)
    r_h, k_h, v_h, w_h, a_h = heads(r), heads(k), heads(v), heads(decay), heads(a)

    def step(state, inp):
        r_t, w_t, k_t, v_t, kk_t, a_t = inp
        sa = jnp.einsum('bhvk,bhk->bhv', state, -kk_t)
        state = (state * w_t[:, :, None, :] + sa[..., None] * (kk_t * a_t)[:, :, None, :]
                 + v_t[..., None] * k_t[:, :, None, :])
        return state, jnp.einsum('bhvk,bhk->bhv', state, r_t)

    s0 = jnp.zeros((Bn, RW_HEADS, RW_DIM, RW_DIM), jnp.float32)
    xs = tuple(t.transpose(1, 0, 2, 3) for t in (r_h, w_h, k_h, v_h, kk, a_h))
    _, o = lax.scan(step, s0, xs)
    o = o.transpose(1, 0, 2, 3)
    mean = jnp.mean(o, axis=-1, keepdims=True)
    var = jnp.mean(jnp.square(o - mean), axis=-1, keepdims=True)
    o = (o - mean) * lax.rsqrt(var + RW_GN_EPS) * gn_w.reshape(RW_HEADS, RW_DIM) + gn_b.reshape(RW_HEADS, RW_DIM)
    o = o + jnp.sum(r_h * k_h * r_k, axis=-1, keepdims=True) * v_h
    return o.reshape(Bn, Sn, RW_WIDTH) * g, v_first


def setup_inputs(seed: int = 0) -> dict:
    key = jax.random.key(seed)
    keys = jax.random.split(key, 40)
    counter = [0]

    def nk():
        counter[0] += 1
        return keys[counter[0] - 1]

    def normal(shape, scale):
        return jax.random.normal(nk(), shape, jnp.float32) * scale

    L, D = DEPTH, D_MODEL
    x = normal((BATCH, SEQ, D), 1.0)
    c = normal((BATCH, D), 1.0)
    offset = jax.random.randint(nk(), (BATCH, 1), 0, 1024, dtype=jnp.int32)
    positions = offset + jnp.arange(SEQ, dtype=jnp.int32)[None, :]
    return {
        "x": x,
        "c": c,
        "positions": positions,
        "ada_w": normal((L, D, ADA_WIDTH), 0.5 * D ** -0.5),
        "ada_b": normal((L, ADA_WIDTH), 0.01),
        "norm_mix_w": 1.0 + normal((L, D), 0.05),
        "norm_ffn_w": 1.0 + normal((L, D), 0.05),
        "w_in": normal((L, D, IN_COLS), D ** -0.5),
        "da_lambda": normal((L, 4, DA_QK_DIM), 0.1),
        "da_subln_w": 1.0 + normal((L, DA_V_DIM), 0.05),
        "hg_lb": normal((L, HG_WIDTH), 0.5),
        "hg_norm_w": 1.0 + normal((L, HG_DIM), 0.05),
        "rw_mu": jax.random.uniform(nk(), (L, RW_SHIFT_WIDTH), jnp.float32, 0.0, 1.0),
        "rw_w0": jax.random.uniform(nk(), (L, RW_WIDTH), jnp.float32, -6.0, -1.0),
        "rw_w2": normal((L, RW_DECAY_LORA, RW_WIDTH), RW_DECAY_LORA ** -0.5),
        "rw_a0": normal((L, RW_WIDTH), 0.1),
        "rw_a2": normal((L, RW_A_LORA, RW_WIDTH), RW_A_LORA ** -0.5),
        "rw_g2": normal((L, RW_GATE_LORA, RW_WIDTH), RW_GATE_LORA ** -0.5),
        "rw_k_k": 0.85 + normal((L, RW_WIDTH), 0.05),
        "rw_k_a": 1.0 + normal((L, RW_WIDTH), 0.05),
        "rw_r_k": normal((L, RW_HEADS, RW_DIM), 0.1),
        "rw_gn_w": 1.0 + normal((L, RW_WIDTH), 0.05),
        "rw_gn_b": normal((L, RW_WIDTH), 0.01),
        "rw_v0": normal((L - 1, RW_WIDTH), 0.1),
        "rw_v1": normal((L - 1, RW_WIDTH, RW_V_LORA), RW_WIDTH ** -0.5),
        "rw_v2": normal((L - 1, RW_V_LORA, RW_WIDTH), RW_V_LORA ** -0.5),
        "w_branch_a": normal((L, DA_WIDTH, D), DA_WIDTH ** -0.5),
        "w_branch_b": normal((L, HG_WIDTH, D), HG_WIDTH ** -0.5),
        "w_branch_c": normal((L, RW_WIDTH, D), RW_WIDTH ** -0.5),
        "w_out": normal((L, D, D), D ** -0.5),
        "ffn_w_in": normal((L, D, 2 * FFN_HIDDEN), D ** -0.5),
        "ffn_w_out": normal((L, FFN_HIDDEN, D), FFN_HIDDEN ** -0.5),
        "final_norm_w": 1.0 + normal((D,), 0.05),
    }


def reference(x, c, positions, ada_w, ada_b, norm_mix_w, norm_ffn_w, w_in, da_lambda, da_subln_w,
              hg_lb, hg_norm_w, rw_mu, rw_w0, rw_w2, rw_a0, rw_a2, rw_g2, rw_k_k, rw_k_a, rw_r_k,
              rw_gn_w, rw_gn_b, rw_v0, rw_v1, rw_v2, w_branch_a, w_branch_b, w_branch_c, w_out,
              ffn_w_in, ffn_w_out, final_norm_w):
    dt = x.dtype
    cos, sin = rope_tables(positions)
    lb_all = jnp.cumsum(jax.nn.softmax(hg_lb.astype(jnp.float32), axis=0), axis=0)
    lb_all = lb_all - lb_all[0]
    c_act = jax.nn.silu(c.astype(jnp.float32))
    v_first = None
    for l in range(DEPTH):
        ada = c_act @ ada_w[l].astype(jnp.float32) + ada_b[l].astype(jnp.float32)
        shift1, scale1, gate1, shift2, scale2, gate2 = [t[:, None, :] for t in jnp.split(ada, 6, axis=-1)]

        h = rms_norm(x, norm_mix_w[l]).astype(jnp.float32) * (1.0 + scale1) + shift1
        u = h.astype(dt) @ w_in[l]
        da_q, da_k, da_v, hg_f, hg_i, hg_q, hg_g, rw_u, gates = jnp.split(u, SPLIT_IDX, axis=-1)
        o_a = diff_attention(da_q, da_k, da_v, cos, sin, da_lambda[l], da_subln_w[l], l)
        o_b = hgrn2(hg_f, hg_i, hg_q, hg_g, lb_all[l], hg_norm_w[l])
        v_res = None if l == 0 else (rw_v0[l - 1], rw_v1[l - 1], rw_v2[l - 1])
        o_c, v_first = rwkv7(rw_u, rw_mu[l], rw_w0[l], rw_w2[l], rw_a0[l], rw_a2[l], rw_g2[l],
                             rw_k_k[l], rw_k_a[l], rw_r_k[l], rw_gn_w[l], rw_gn_b[l], v_first, v_res)
        g_a, g_b, g_c = jnp.split(jax.nn.sigmoid(gates.astype(jnp.float32)), N_BRANCHES, axis=-1)
        merged = g_a * (o_a @ w_branch_a[l]) + g_b * (o_b @ w_branch_b[l]) + g_c * (o_c @ w_branch_c[l])
        mix = merged.astype(dt) @ w_out[l]
        x = x + (gate1 * mix).astype(dt)

        h = rms_norm(x, norm_ffn_w[l]).astype(jnp.float32) * (1.0 + scale2) + shift2
        gate_h, up_h = jnp.split(h.astype(dt) @ ffn_w_in[l], 2, axis=-1)
        ffn = (jax.nn.silu(gate_h) * up_h) @ ffn_w_out[l]
        x = x + (gate2 * ffn).astype(dt)
    return rms_norm(x, final_norm_w)
```

```python
import functools
import math

import jax
import jax.numpy as jnp
from jax import lax
from jax.experimental import pallas as pl
from jax.experimental.pallas import tpu as pltpu

F32 = jnp.float32
BF16 = jnp.bfloat16

D_MODEL = 1024
CHUNK = 64
ROPE_THETA = 10000.0
NORM_EPS = 1e-6

DA_HEADS = 4
DA_QK_DIM = 64
DA_V_DIM = 128
HG_HEADS = 4
HG_DIM = 128
RW_HEADS = 8
RW_DIM = 64
RW_WIDTH = 512
RW_GN_EPS = 64e-5
RW_SHIFT_WIDTH = 1792
FFN_HIDDEN = 2816
IN_COLS = 8448

LANES = 128
SUB = 16
RW_PACK = 4
RW_GW = RW_PACK * RW_DIM

COL_GATES = 0
COL_Q = 3072
COL_RW = 3584
COL_K = 5376
COL_V = 5888
COL_HF = 6400
COL_HI = 6912
COL_HQ = 7424
COL_HG = 7936
PROJ_TN = 768
TILE_Q = COL_Q // PROJ_TN
TILE_K = COL_K // PROJ_TN

VMEM_LIMIT = 56 * 1024 * 1024


def _cp(sem):
    return pltpu.CompilerParams(dimension_semantics=sem, vmem_limit_bytes=VMEM_LIMIT)


def _sigmoid(x):
    return 1.0 / (1.0 + jnp.exp(-x))


def _softplus(x):
    return jnp.maximum(x, 0.0) + jnp.log1p(jnp.exp(-jnp.abs(x)))


def _split3(x):
    hi = x.astype(BF16)
    r1 = x - hi.astype(F32)
    mid = r1.astype(BF16)
    lo = (r1 - mid.astype(F32)).astype(BF16)
    return hi, mid, lo


def _dot(a, b):
    return jnp.dot(a, b, preferred_element_type=F32)


def _dot_nt(a, b):
    return lax.dot_general(a, b, (((1,), (1,)), ((), ())), preferred_element_type=F32)


def _dot_exact_lhs(a01, x):
    hi, mid, lo = _split3(x)
    return _dot(a01, hi) + _dot(a01, mid) + _dot(a01, lo)


def _dot_exact_rhs(x, b01):
    hi, mid, lo = _split3(x)
    return _dot(hi, b01) + _dot(mid, b01) + _dot(lo, b01)


def _dot_hp(a, b):
    a_hi = a.astype(BF16)
    a_lo = (a - a_hi.astype(F32)).astype(BF16)
    b_hi = b.astype(BF16)
    b_lo = (b - b_hi.astype(F32)).astype(BF16)
    return _dot(a_hi, b_hi) + _dot(a_hi, b_lo) + _dot(a_lo, b_hi)


def _iota(shape, dim):
    return lax.broadcasted_iota(jnp.int32, shape, dim)


def _ada_body(c_ref, w_ref, b_ref, o_ref):
    c = c_ref[...]
    ca = c * _sigmoid(c)
    o_ref[...] = _dot_hp(ca, w_ref[...]) + b_ref[...]


def _ada_call(c, ada_w, ada_b):
    nl, d, n6 = ada_w.shape
    bsz = c.shape[0]
    tn = 1536
    return pl.pallas_call(
        _ada_body,
        grid=(nl, n6 // tn),
        in_specs=[
            pl.BlockSpec((bsz, d), lambda l, j: (0, 0)),
            pl.BlockSpec((None, d, tn), lambda l, j: (l, 0, j)),
            pl.BlockSpec((None, 1, tn), lambda l, j: (l, 0, j)),
        ],
        out_specs=pl.BlockSpec((None, bsz, tn), lambda l, j: (l, 0, j)),
        out_shape=jax.ShapeDtypeStruct((nl, bsz, n6), F32),
        compiler_params=_cp(("parallel", "parallel")),
        name="ada",
    )(c, ada_w, ada_b.reshape(nl, 1, n6))


def _rope_body(pos_ref, invf_ref, sgn_ref, cos_ref, sin_ref):
    ang = pos_ref[...].astype(F32) * invf_ref[...]
    cos_ref[...] = jnp.cos(ang)
    sin_ref[...] = jnp.sin(ang) * sgn_ref[...]


def _rope_call(positions):
    m = positions.size
    tm = min(2048, m)
    inv_freq = ROPE_THETA ** (-jnp.arange(0, DA_QK_DIM, 2, dtype=F32) / DA_QK_DIM)
    invf = jnp.tile(inv_freq, LANES // (DA_QK_DIM // 2)).reshape(1, LANES)
    half = (jnp.arange(LANES) % DA_QK_DIM) < (DA_QK_DIM // 2)
    sgn = jnp.where(half, -1.0, 1.0).astype(F32).reshape(1, LANES)
    return pl.pallas_call(
        _rope_body,
        grid=(m // tm,),
        in_specs=[
            pl.BlockSpec((tm, 1), lambda i: (i, 0)),
            pl.BlockSpec((1, LANES), lambda i: (0, 0)),
            pl.BlockSpec((1, LANES), lambda i: (0, 0)),
        ],
        out_specs=[pl.BlockSpec((tm, LANES), lambda i: (i, 0))] * 2,
        out_shape=[jax.ShapeDtypeStruct((m, LANES), F32)] * 2,
        compiler_params=_cp(("parallel",)),
        name="rope_tables",
    )(positions.reshape(m, 1), invf, sgn)


def _norm_mod(x, nw, scale, shift):
    ms = jnp.mean(x * x, axis=-1, keepdims=True)
    y = x * lax.rsqrt(ms + NORM_EPS) * nw
    return y * (1.0 + scale) + shift


def _proj_in_body(x_ref, nw_ref, shift_ref, scale_ref, cos_ref, sin_ref, w_ref, o_ref, h_ref):
    j = pl.program_id(1)

    @pl.when(j == 0)
    def _():
        h_ref[...] = _norm_mod(x_ref[...], nw_ref[...], scale_ref[...], shift_ref[...]).astype(BF16)

    acc = _dot(h_ref[...], w_ref[...])
    is_q = j == TILE_Q
    is_k = j == TILE_K

    @pl.when(jnp.logical_or(is_q, is_k))
    def _():
        cos = cos_ref[...]
        sin_s = sin_ref[...]
        lo_half = (_iota((1, LANES), 1) % DA_QK_DIM) < (DA_QK_DIM // 2)
        qscale = jnp.where(is_q, DA_QK_DIM ** -0.5, 1.0).astype(F32)
        for g in range(PROJ_TN // LANES):
            slab = acc[:, g * LANES:(g + 1) * LANES]
            if g < 4:
                partner = jnp.where(lo_half, pltpu.roll(slab, LANES - 32, 1), pltpu.roll(slab, 32, 1))
                slab = (slab * cos + partner * sin_s) * qscale
            o_ref[:, g * LANES:(g + 1) * LANES] = slab

    @pl.when(jnp.logical_not(jnp.logical_or(is_q, is_k)))
    def _():
        o_ref[...] = acc


def _proj_in_call(x2, nw, ada3, cos, sin_s, w_bf16, layer, bsz, seq):
    m, d = x2.shape
    n = w_bf16.shape[1]
    tm = min(1024, seq)
    per_b = seq // tm
    base = layer * bsz * 6

    def ada_spec(k):
        return pl.BlockSpec((None, 1, d), lambda i, j: (base + (i // per_b) * 6 + k, 0, 0))

    return pl.pallas_call(
        _proj_in_body,
        grid=(m // tm, n // PROJ_TN),
        in_specs=[
            pl.BlockSpec((tm, d), lambda i, j: (i, 0)),
            pl.BlockSpec((1, d), lambda i, j: (0, 0)),
            ada_spec(0),
            ada_spec(1),
            pl.BlockSpec((tm, LANES), lambda i, j: (i, 0)),
            pl.BlockSpec((tm, LANES), lambda i, j: (i, 0)),
            pl.BlockSpec((d, PROJ_TN), lambda i, j: (0, j)),
        ],
        out_specs=pl.BlockSpec((tm, PROJ_TN), lambda i, j: (i, j)),
        out_shape=jax.ShapeDtypeStruct((m, n), F32),
        scratch_shapes=[pltpu.VMEM((tm, d), BF16)],
        compiler_params=_cp(("parallel", "arbitrary")),
        name="proj_in",
    )(x2, nw, ada3, ada3, cos, sin_s, w_bf16)


def _attn_body(lam_ref, q_ref, k_ref, v_ref, sw_ref, o_ref, *, tq, lam_init):
    i = pl.program_id(2)
    lv = lam_ref[...]
    lam = (jnp.exp(jnp.sum(lv[0:1] * lv[1:2], axis=1, keepdims=True))
           - jnp.exp(jnp.sum(lv[2:3] * lv[3:4], axis=1, keepdims=True)) + lam_init)

    q = q_ref[...]
    lane = _iota((1, LANES), 1)
    q0 = jnp.where(lane < DA_QK_DIM, q, 0.0).astype(BF16)
    q1 = jnp.where(lane >= DA_QK_DIM, q, 0.0).astype(BF16)
    qs = jnp.concatenate([q0, q1], axis=0)

    rr = _iota((2 * tq, tq), 0)
    cc = _iota((2 * tq, tq), 1)
    q_chunk = jnp.right_shift(jnp.bitwise_and(rr, tq - 1), 6)
    allowed = jnp.right_shift(cc, 6) <= q_chunk

    def step(j, carry, masked):
        m_i, l_i, acc = carry
        start = pl.multiple_of(j * tq, tq)
        kb = k_ref[pl.ds(start, tq), :].astype(BF16)
        vb = v_ref[pl.ds(start, tq), :].astype(BF16)
        s = _dot_nt(qs, kb)
        if masked:
            s = jnp.where(allowed, s, -jnp.inf)
        m_new = jnp.maximum(m_i, jnp.max(s, axis=1, keepdims=True))
        p = jnp.exp(s - m_new)
        alpha = jnp.exp(m_i - m_new)
        l_new = alpha * l_i + jnp.sum(p, axis=1, keepdims=True)
        acc_new = alpha * acc + _dot(p.astype(BF16), vb)
        return m_new, l_new, acc_new

    init = (jnp.full((2 * tq, 1), -jnp.inf, F32), jnp.zeros((2 * tq, 1), F32),
            jnp.zeros((2 * tq, DA_V_DIM), F32))
    carry = lax.fori_loop(0, i, lambda j, c: step(j, c, False), init)
    m_i, l_i, acc = step(i, carry, True)

    o = acc / l_i
    o = o[:tq] - lam * o[tq:]
    ms = jnp.mean(o * o, axis=-1, keepdims=True)
    o = o * lax.rsqrt(ms + NORM_EPS) * sw_ref[...] * (1.0 - lam_init)
    o_ref[...] = o.astype(o_ref.dtype)


def _attn_call(u3, lam_vecs, subln_w, layer):
    bsz, seq, _ = u3.shape
    tq = min(256, seq)
    lam_init = 0.8 - 0.6 * math.exp(-0.3 * layer)
    qb, kb, vb = COL_Q // LANES, COL_K // LANES, COL_V // LANES
    return pl.pallas_call(
        functools.partial(_attn_body, tq=tq, lam_init=lam_init),
        grid=(bsz, DA_HEADS, seq // tq),
        in_specs=[
            pl.BlockSpec((4, DA_QK_DIM), lambda b, h, i: (0, 0)),
            pl.BlockSpec((None, tq, LANES), lambda b, h, i: (b, i, qb + h)),
            pl.BlockSpec((None, seq, LANES), lambda b, h, i: (b, 0, kb + h)),
            pl.BlockSpec((None, seq, LANES), lambda b, h, i: (b, 0, vb + h)),
            pl.BlockSpec((1, LANES), lambda b, h, i: (0, 0)),
        ],
        out_specs=pl.BlockSpec((None, tq, LANES), lambda b, h, i: (b, i, h)),
        out_shape=jax.ShapeDtypeStruct((bsz, seq, DA_HEADS * DA_V_DIM), BF16),
        compiler_params=_cp(("parallel", "parallel", "arbitrary")),
        name="diff_attn",
    )(lam_vecs, u3, u3, u3, subln_w.reshape(1, LANES))


def _hgrn_body(lbp_ref, z_ref, i_ref, q_ref, g_ref, nw_ref, o_ref, *, layer, nchunks):
    lbp = lbp_ref[...]
    e = jnp.exp(lbp - jnp.max(lbp, axis=0, keepdims=True))
    sm = e / jnp.sum(e, axis=0, keepdims=True)
    cs = sm[0:1]
    for t in range(1, layer + 1):
        cs = cs + sm[t:t + 1]
    lb = cs - sm[0:1]
    log_lb = jnp.log(lb)
    log1m = jnp.log1p(-lb)

    rr = _iota((CHUNK, CHUNK), 0)
    cc = _iota((CHUNK, CHUNK), 1)
    sub_start = jnp.bitwise_and(rr, ~(SUB - 1))
    tril = (cc <= rr).astype(BF16)
    before_sub = (cc < sub_start).astype(BF16)
    causal = cc <= rr
    same_sub = jnp.right_shift(cc, 4) == jnp.right_shift(rr, 4)
    row = _iota((CHUNK, 1), 0)

    def chunk(c, state_t):
        sl = pl.ds(pl.multiple_of(c * CHUNK, CHUNK), CHUNK)
        z = z_ref[sl, :]
        y = log1m - _softplus(-z)
        mx = jnp.maximum(log_lb, y)
        lf = mx + jnp.log1p(jnp.exp(-jnp.abs(log_lb - y)))
        key = (1.0 - lb) * _sigmoid(-z)
        qc = q_ref[sl, :]
        vc = i_ref[sl, :].astype(BF16)

        b = _dot_exact_lhs(tril, lf)
        bsub = _dot_exact_lhs(before_sub, lf)
        d = b - bsub
        b_last = b[CHUNK - 1:CHUNK, :]

        q_d = (qc * jnp.exp(d)).astype(BF16)
        k_d = (key * jnp.exp(-d)).astype(BF16)
        scores = jnp.where(jnp.logical_and(causal, same_sub), _dot_nt(q_d, k_d), 0.0)
        for jb in range(CHUNK // SUB - 1):
            e_j = b[(jb + 1) * SUB - 1:(jb + 1) * SUB, :]
            q_j = jnp.where(row >= (jb + 1) * SUB, qc * jnp.exp(jnp.minimum(b - e_j, 0.0)), 0.0)
            in_j = jnp.logical_and(row >= jb * SUB, row < (jb + 1) * SUB)
            k_j = jnp.where(in_j, key * jnp.exp(jnp.minimum(e_j - b, 0.0)), 0.0)
            scores = scores + _dot_nt(q_j.astype(BF16), k_j.astype(BF16))

        o = _dot(scores.astype(BF16), vc)
        o = o + _dot_nt((qc * jnp.exp(b)).astype(BF16), state_t.astype(BF16))
        k_out = (key * jnp.exp(b_last - b)).astype(BF16)
        state_new = state_t * jnp.exp(b_last) + _dot(vc.T, k_out)

        ms = jnp.mean(o * o, axis=-1, keepdims=True)
        g = g_ref[sl, :]
        o = o * lax.rsqrt(ms + NORM_EPS) * nw_ref[...] * (g * _sigmoid(g))
        o_ref[sl, :] = o.astype(o_ref.dtype)
        return state_new

    lax.fori_loop(0, nchunks, chunk, jnp.zeros((HG_DIM, HG_DIM), F32))


def _hgrn_call(u3, hg_lb, norm_w, layer):
    bsz, seq, _ = u3.shape
    nl = hg_lb.shape[0]

    def col(base):
        return lambda b, h: (b, 0, base // LANES + h)

    blk = (None, seq, LANES)
    return pl.pallas_call(
        functools.partial(_hgrn_body, layer=layer, nchunks=seq // CHUNK),
        grid=(bsz, HG_HEADS),
        in_specs=[
            pl.BlockSpec((nl, LANES), lambda b, h: (0, h)),
            pl.BlockSpec(blk, col(COL_HF)),
            pl.BlockSpec(blk, col(COL_HI)),
            pl.BlockSpec(blk, col(COL_HQ)),
            pl.BlockSpec(blk, col(COL_HG)),
            pl.BlockSpec((1, LANES), lambda b, h: (0, 0)),
        ],
        out_specs=pl.BlockSpec(blk, lambda b, h: (b, 0, h)),
        out_shape=jax.ShapeDtypeStruct((bsz, seq, HG_HEADS * HG_DIM), BF16),
        compiler_params=_cp(("parallel", "parallel")),
        name="hgrn2",
    )(hg_lb, u3, u3, u3, u3, norm_w.reshape(1, LANES))


def _head_ones(n, width):
    r = _iota((n, n), 0) // width
    c = _iota((n, n), 1) // width
    return (r == c).astype(BF16)


def _rw_prep_body(*refs, has_vres):
    if has_vres:
        (u_ref, prev_ref, mu_ref, w0_ref, w2_ref, a0_ref, a2_ref, g2_ref, kk_ref, ka_ref,
         vf_ref, v0_ref, v1_ref, v2_ref,
         r_o, k_o, v_o, lw_o, kkn_o, bk_o, g_o) = refs
    else:
        (u_ref, prev_ref, mu_ref, w0_ref, w2_ref, a0_ref, a2_ref, g2_ref, kk_ref, ka_ref,
         r_o, k_o, v_o, lw_o, kkn_o, bk_o, g_o) = refs
    j = pl.program_id(1)
    u = u_ref[...]
    ts = u.shape[0]
    prev_row = prev_ref[7:8, :] * (j > 0).astype(F32)
    row = _iota((ts, 1), 0)
    u_prev = jnp.where(row == 0, prev_row, pltpu.roll(u, 1, 0))
    x = u + (u_prev - u) * mu_ref[...]

    r = x[:, 0:512]
    k = x[:, 512:1024]
    v = x[:, 1024:1536]
    w_lo = x[:, 1536:1600]
    a_lo = x[:, 1600:1664]
    g_lo = x[:, 1664:1792]

    wpre = w0_ref[...] + _dot_hp(jnp.tanh(w_lo), w2_ref[...])
    w = -_softplus(-wpre) - 0.5
    lw = -jnp.exp(w)
    a = _sigmoid(a0_ref[...] + _dot_hp(a_lo, a2_ref[...]))
    g = _dot_hp(_sigmoid(g_lo), g2_ref[...])
    if has_vres:
        mix = _sigmoid(v0_ref[...] + _dot_hp(_dot_hp(v, v1_ref[...]), v2_ref[...]))
        v = v + (vf_ref[...] - v) * mix

    kk = k * kk_ref[...]
    ssq = _dot_exact_rhs(kk * kk, _head_ones(RW_WIDTH, RW_DIM))
    kkn = kk * lax.rsqrt(jnp.maximum(ssq, 1e-24))
    k = k * (1.0 + (a - 1.0) * ka_ref[...])

    r_o[...] = r
    k_o[...] = k
    v_o[...] = v
    lw_o[...] = lw
    kkn_o[...] = kkn
    bk_o[...] = kkn * a
    g_o[...] = g


def _rw_prep_call(u3, mu, w0, w2, a0, a2, g2, k_k, k_a, vres):
    bsz, seq, _ = u3.shape
    ts = min(512, seq)
    w = RW_WIDTH
    has_vres = vres is not None

    def full(arr):
        return pl.BlockSpec(arr.shape, lambda b, j: (0,) * arr.ndim)

    params = [mu.reshape(1, -1), w0.reshape(1, w), w2, a0.reshape(1, w), a2, g2,
              k_k.reshape(1, w), k_a.reshape(1, w)]
    in_specs = [
        pl.BlockSpec((None, ts, RW_SHIFT_WIDTH), lambda b, j: (b, j, COL_RW // RW_SHIFT_WIDTH)),
        pl.BlockSpec((None, 8, RW_SHIFT_WIDTH),
                     lambda b, j: (b, jnp.maximum(j * (ts // 8) - 1, 0), COL_RW // RW_SHIFT_WIDTH)),
    ] + [full(p) for p in params]
    args = [u3, u3] + params
    if has_vres:
        v_first, v0, v1, v2 = vres
        extra = [v0.reshape(1, w), v1, v2]
        in_specs += [pl.BlockSpec((None, ts, w), lambda b, j: (b, j, 0))] + [full(p) for p in extra]
        args += [v_first] + extra
    out_spec = pl.BlockSpec((None, ts, w), lambda b, j: (b, j, 0))
    return pl.pallas_call(
        functools.partial(_rw_prep_body, has_vres=has_vres),
        grid=(bsz, seq // ts),
        in_specs=in_specs,
        out_specs=[out_spec] * 7,
        out_shape=[jax.ShapeDtypeStruct((bsz, seq, w), F32)] * 7,
        compiler_params=_cp(("parallel", "parallel")),
        name="rwkv_prep",
    )(*args)


def _rw_scan_body(r_ref, k_ref, v_ref, lw_ref, kk_ref, bk_ref, g_ref, rk_ref, gw_ref, gb_ref,
                  o_ref, state_ref, *, nsub):
    @pl.when(pl.program_id(1) == 0)
    def _():
        state_ref[...] = jnp.zeros_like(state_ref)

    t = CHUNK
    gw = RW_GW
    rr = _iota((t, t), 0)
    cc = _iota((t, t), 1)
    tril = (cc <= rr).astype(BF16)
    rr4 = _iota((t, gw), 0)
    cj = jnp.bitwise_and(_iota((t, gw), 1), t - 1)
    strict = cj < rr4
    incl = cj <= rr4
    eye_c = (cj == rr4).astype(F32)
    bd_mask = (_iota((gw, gw), 0) // t) == (_iota((gw, gw), 1) // RW_DIM)
    ones_h = _head_ones(gw, RW_DIM)

    def bd(xc):
        xb = xc.astype(BF16)
        return jnp.where(bd_mask, jnp.concatenate([xb] * RW_PACK, axis=0), jnp.zeros((), BF16))

    for s in range(nsub):
        rows = pl.ds(s * t, t)
        for grp in range(RW_HEADS // RW_PACK):
            cols = pl.ds(grp * gw, gw)
            r = r_ref[rows, cols]
            k = k_ref[rows, cols]
            v = v_ref[rows, cols]
            lw = lw_ref[rows, cols]
            kk = kk_ref[rows, cols]
            bk = bk_ref[rows, cols]

            c = _dot_exact_lhs(tril, lw)
            c_last = c[t - 1:t, :]
            e_neg = jnp.exp(-c)
            a_t = (-kk * jnp.exp(c - lw)).astype(BF16)
            b_t = bk * e_neg
            k_t = k * e_neg
            r_t = (r * jnp.exp(c)).astype(BF16)
            ar = jnp.concatenate([a_t, r_t], axis=0)

            xb = _dot_nt(ar, bd(b_t))
            xk = _dot_nt(ar, bd(k_t))
            l_c = jnp.where(strict, xb[:t], 0.0)
            m_c = jnp.where(strict, xk[:t], 0.0)
            rb = jnp.where(incl, xb[t:], 0.0).astype(BF16)
            rkm = jnp.where(incl, xk[t:], 0.0).astype(BF16)

            x_inv = eye_c + l_c
            p = l_c
            for _ in range(5):
                p_bd = bd(p)
                p = _dot(p.astype(BF16), p_bd)
                x_inv = x_inv + _dot(x_inv.astype(BF16), bd(p))
            x_inv = x_inv.astype(BF16)

            v_bd = bd(v)
            mv = _dot(m_c.astype(BF16), v_bd)
            w_t = _dot(x_inv, bd(a_t)).astype(BF16)
            u_t = _dot(x_inv, bd(mv))

            st = state_ref[grp]
            st_b = st.astype(BF16)
            u = _dot_nt(w_t, st_b) + u_t
            o = _dot_nt(r_t, st_b) + _dot(rb, bd(u)) + _dot(rkm, v_bd)

            decay_out = jnp.exp(c_last - c)
            bkg = jnp.concatenate([bk * decay_out, k * decay_out], axis=0).astype(BF16)
            uv = jnp.concatenate([u, v], axis=0)
            upd = _dot(uv.T.astype(BF16), bkg)
            state_ref[grp] = st * jnp.exp(c_last) + jnp.where(bd_mask, upd, 0.0)

            mean = _dot_exact_rhs(o, ones_h) * (1.0 / RW_DIM)
            dlt = o - mean
            var = _dot_exact_rhs(dlt * dlt, ones_h) * (1.0 / RW_DIM)
            on = dlt * lax.rsqrt(var + RW_GN_EPS) * gw_ref[:, cols] + gb_ref[:, cols]
            bonus = _dot_exact_rhs(r * k * rk_ref[:, cols], ones_h)
            on = (on + bonus * v) * g_ref[rows, cols]
            o_ref[rows, cols] = on.astype(o_ref.dtype)


def _rw_scan_call(r, k, v, lw, kkn, bk, g, r_k, gn_w, gn_b):
    bsz, seq, w = r.shape
    tc = min(256, seq)
    spec = pl.BlockSpec((None, tc, w), lambda b, j: (b, j, 0))
    pspec = pl.BlockSpec((1, w), lambda b, j: (0, 0))
    return pl.pallas_call(
        functools.partial(_rw_scan_body, nsub=tc // CHUNK),
        grid=(bsz, seq // tc),
        in_specs=[spec] * 7 + [pspec] * 3,
        out_specs=spec,
        out_shape=jax.ShapeDtypeStruct((bsz, seq, w), BF16),
        scratch_shapes=[pltpu.VMEM((RW_HEADS // RW_PACK, RW_GW, RW_GW), F32)],
        compiler_params=_cp(("parallel", "arbitrary")),
        name="rwkv_scan",
    )(r, k, v, lw, kkn, bk, g, r_k.reshape(1, w), gn_w.reshape(1, w), gn_b.reshape(1, w))


def _merge_body(ga_ref, gb_ref, gc_ref, oa_ref, ob_ref, oc_ref, x_ref, gate_ref,
                wa_ref, wb_ref, wc_ref, wo_ref, o_ref):
    merged = (_sigmoid(ga_ref[...]) * _dot(oa_ref[...], wa_ref[...])
              + _sigmoid(gb_ref[...]) * _dot(ob_ref[...], wb_ref[...])
              + _sigmoid(gc_ref[...]) * _dot(oc_ref[...], wc_ref[...]))
    mix = _dot(merged.astype(BF16), wo_ref[...])
    o_ref[...] = x_ref[...] + gate_ref[...] * mix


def _merge_call(u2, o_a, o_b, o_c, x2, ada3, wa, wb, wc, wo, layer, bsz, seq):
    m, d = x2.shape
    tm = min(512, seq)
    per_b = seq // tm
    base = layer * bsz * 6
    wdt = o_a.shape[1]

    def const(arr):
        return pl.BlockSpec(arr.shape, lambda i: (0, 0))

    return pl.pallas_call(
        _merge_body,
        grid=(m // tm,),
        in_specs=[
            pl.BlockSpec((tm, d), lambda i: (i, 0)),
            pl.BlockSpec((tm, d), lambda i: (i, 1)),
            pl.BlockSpec((tm, d), lambda i: (i, 2)),
            pl.BlockSpec((tm, wdt), lambda i: (i, 0)),
            pl.BlockSpec((tm, wdt), lambda i: (i, 0)),
            pl.BlockSpec((tm, wdt), lambda i: (i, 0)),
            pl.BlockSpec((tm, d), lambda i: (i, 0)),
            pl.BlockSpec((None, 1, d), lambda i: (base + (i // per_b) * 6 + 2, 0, 0)),
            const(wa), const(wb), const(wc), const(wo),
        ],
        out_specs=pl.BlockSpec((tm, d), lambda i: (i, 0)),
        out_shape=jax.ShapeDtypeStruct((m, d), F32),
        compiler_params=_cp(("parallel",)),
        name="merge_out",
    )(u2, u2, u2, o_a, o_b, o_c, x2, ada3, wa, wb, wc, wo)


def _ffn_body(x_ref, nw_ref, shift_ref, scale_ref, gate_ref, wi_ref, wo_ref, fn_ref, o_ref, *, final):
    x = x_ref[...]
    h = _norm_mod(x, nw_ref[...], scale_ref[...], shift_ref[...]).astype(BF16)
    half = FFN_HIDDEN // 2
    acc = jnp.zeros(x.shape, F32)
    for c in range(2):
        gh = _dot(h, wi_ref[:, c * half:(c + 1) * half])
        uh = _dot(h, wi_ref[:, FFN_HIDDEN + c * half:FFN_HIDDEN + (c + 1) * half])
        act = (gh * _sigmoid(gh) * uh).astype(BF16)
        acc = acc + _dot(act, wo_ref[c * half:(c + 1) * half, :])
    y = x + gate_ref[...] * acc
    if final:
        ms = jnp.mean(y * y, axis=-1, keepdims=True)
        y = y * lax.rsqrt(ms + NORM_EPS) * fn_ref[...]
    o_ref[...] = y


def _ffn_call(x2, nw, ada3, wi, wo, final_w, layer, bsz, seq, final):
    m, d = x2.shape
    tm = min(512, seq)
    per_b = seq // tm
    base = layer * bsz * 6

    def ada_spec(k):
        return pl.BlockSpec((None, 1, d), lambda i: (base + (i // per_b) * 6 + k, 0, 0))

    def const(arr):
        return pl.BlockSpec(arr.shape, lambda i: (0, 0), pipeline_mode=pl.Buffered(1))

    return pl.pallas_call(
        functools.partial(_ffn_body, final=final),
        grid=(m // tm,),
        in_specs=[
            pl.BlockSpec((tm, d), lambda i: (i, 0)),
            pl.BlockSpec((1, d), lambda i: (0, 0)),
            ada_spec(3), ada_spec(4), ada_spec(5),
            const(wi), const(wo),
            pl.BlockSpec((1, d), lambda i: (0, 0)),
        ],
        out_specs=pl.BlockSpec((tm, d), lambda i: (i, 0)),
        out_shape=jax.ShapeDtypeStruct((m, d), F32),
        compiler_params=_cp(("parallel",)),
        name="ffn_final" if final else "ffn",
    )(x2, nw, ada3, ada3, ada3, wi, wo, final_w)


def _permute_w_in(w):
    q, k, v = w[:, 0:512], w[:, 512:1024], w[:, 1024:1536]
    hf, hi, hq, hg = w[:, 1536:2048], w[:, 2048:2560], w[:, 2560:3072], w[:, 3072:3584]
    rw = w[:, 3584:5376]
    gates = w[:, 5376:8448]
    return jnp.concatenate([gates, q, rw, k, v, hf, hi, hq, hg], axis=1)


def kernel(x, c, positions, ada_w, ada_b, norm_mix_w, norm_ffn_w, w_in, da_lambda, da_subln_w, hg_lb, hg_norm_w, rw_mu, rw_w0, rw_w2, rw_a0, rw_a2, rw_g2, rw_k_k, rw_k_a, rw_r_k, rw_gn_w, rw_gn_b, rw_v0, rw_v1, rw_v2, w_branch_a, w_branch_b, w_branch_c, w_out, ffn_w_in, ffn_w_out, final_norm_w):
    bsz, seq, d = x.shape
    depth = ada_w.shape[0]
    m = bsz * seq

    ada = _ada_call(c, ada_w, ada_b)
    ada3 = ada.reshape(depth * bsz * 6, 1, d)
    cos, sin_s = _rope_call(positions)

    x2 = x.reshape(m, d)
    v_first = None
    for l in range(depth):
        w_l = _permute_w_in(w_in[l]).astype(BF16)
        u2 = _proj_in_call(x2, norm_mix_w[l].reshape(1, d), ada3, cos, sin_s, w_l, l, bsz, seq)
        u3 = u2.reshape(bsz, seq, IN_COLS)

        o_a = _attn_call(u3, da_lambda[l], da_subln_w[l], l)
        o_b = _hgrn_call(u3, hg_lb, hg_norm_w[l], l)
        vres = None if l == 0 else (v_first, rw_v0[l - 1], rw_v1[l - 1], rw_v2[l - 1])
        r, k, v, lw, kkn, bk, g = _rw_prep_call(u3, rw_mu[l], rw_w0[l], rw_w2[l], rw_a0[l], rw_a2[l],
                                                rw_g2[l], rw_k_k[l], rw_k_a[l], vres)
        if l == 0:
            v_first = v
        o_c = _rw_scan_call(r, k, v, lw, kkn, bk, g, rw_r_k[l], rw_gn_w[l], rw_gn_b[l])

        x2 = _merge_call(u2, o_a.reshape(m, -1), o_b.reshape(m, -1), o_c.reshape(m, -1), x2, ada3,
                         w_branch_a[l].astype(BF16), w_branch_b[l].astype(BF16),
                         w_branch_c[l].astype(BF16), w_out[l].astype(BF16), l, bsz, seq)
        x2 = _ffn_call(x2, norm_ffn_w[l].reshape(1, d), ada3, ffn_w_in[l].astype(BF16),
                       ffn_w_out[l].astype(BF16), final_norm_w.reshape(1, d), l, bsz, seq,
                       final=(l == depth - 1))
    return x2.reshape(bsz, seq, d)
```

```python
import functools
import math

import jax
import jax.numpy as jnp
from jax import lax
from jax.experimental import pallas as pl
from jax.experimental.pallas import tpu as pltpu

F32 = jnp.float32
BF16 = jnp.bfloat16

D_MODEL = 1024
CHUNK = 64
ROPE_THETA = 10000.0
NORM_EPS = 1e-6

DA_HEADS = 4
DA_QK_DIM = 64
DA_V_DIM = 128
HG_HEADS = 4
HG_DIM = 128
RW_HEADS = 8
RW_DIM = 64
RW_WIDTH = 512
RW_GN_EPS = 64e-5
FFN_HIDDEN = 2816
IN_COLS = 8448

LANES = 128
ATTN_LOCKSTEP = 4
SUB = 16
RW_PACK = 4
RW_GW = RW_PACK * RW_DIM
RW_GROUPS = RW_HEADS // RW_PACK

COL_GATES = 0
COL_HF = 3072
COL_HI = 3584
COL_HQ = 4096
COL_HG = 4608
COL_Q = 5120
COL_K = 5632
COL_V = 6144
COL_RR = 6656
COL_RK = 7168
COL_RV = 7680
COL_RL = 8192
RW_LORA_W = 256
HEADW = 512
PROJ_TN = 768
GROUPS_PER_TILE = PROJ_TN // LANES
ROPE_Q0 = COL_Q // LANES
ROPE_K0 = COL_K // LANES

VMEM_LIMIT = 56 * 1024 * 1024


def _cp(sem):
    return pltpu.CompilerParams(dimension_semantics=sem, vmem_limit_bytes=VMEM_LIMIT)


def _sigmoid(x):
    return 1.0 / (1.0 + jnp.exp(-x))


def _softplus(x):
    return jnp.maximum(x, 0.0) + jnp.log1p(jnp.exp(-jnp.abs(x)))


def _split3(x):
    hi = x.astype(BF16)
    r1 = x - hi.astype(F32)
    mid = r1.astype(BF16)
    lo = (r1 - mid.astype(F32)).astype(BF16)
    return hi, mid, lo


def _dot(a, b):
    return jnp.dot(a, b, preferred_element_type=F32)


def _dot_nt(a, b):
    return lax.dot_general(a, b, (((1,), (1,)), ((), ())), preferred_element_type=F32)


def _dot_exact_lhs(a01, x):
    n = x.shape[1]
    res = _dot(a01, jnp.concatenate(_split3(x), axis=1))
    return res[:, 0:n] + res[:, n:2 * n] + res[:, 2 * n:3 * n]


def _dot_exact_rhs(x, b01):
    m = x.shape[0]
    res = _dot(jnp.concatenate(_split3(x), axis=0), b01)
    return res[0:m] + res[m:2 * m] + res[2 * m:3 * m]


def _dot_hp(a, b):
    a_hi = a.astype(BF16)
    a_lo = (a - a_hi.astype(F32)).astype(BF16)
    b_hi = b.astype(BF16)
    b_lo = (b - b_hi.astype(F32)).astype(BF16)
    return _dot(a_hi, b_hi) + _dot(a_hi, b_lo) + _dot(a_lo, b_hi)


def _iota(shape, dim):
    return lax.broadcasted_iota(jnp.int32, shape, dim)


def _ada_body(c_ref, w_ref, b_ref, o_ref):
    c = c_ref[...]
    ca = c * _sigmoid(c)
    o_ref[...] = _dot_hp(ca, w_ref[...]) + b_ref[...]


def _ada_call(c, ada_w, ada_b):
    nl, d, n6 = ada_w.shape
    bsz = c.shape[0]
    tn = 1536
    return pl.pallas_call(
        _ada_body,
        grid=(nl, n6 // tn),
        in_specs=[
            pl.BlockSpec((bsz, d), lambda l, j: (0, 0)),
            pl.BlockSpec((None, d, tn), lambda l, j: (l, 0, j)),
            pl.BlockSpec((None, 1, tn), lambda l, j: (l, 0, j)),
        ],
        out_specs=pl.BlockSpec((None, bsz, tn), lambda l, j: (l, 0, j)),
        out_shape=jax.ShapeDtypeStruct((nl, bsz, n6), F32),
        compiler_params=_cp(("parallel", "parallel")),
        name="ada",
    )(c, ada_w, ada_b.reshape(nl, 1, n6))


def _rope_body(pos_ref, invf_ref, sgn_ref, cos_ref, sin_ref):
    ang = pos_ref[...].astype(F32) * invf_ref[...]
    cos_ref[...] = jnp.cos(ang)
    sin_ref[...] = jnp.sin(ang) * sgn_ref[...]


def _rope_call(positions):
    m = positions.size
    tm = min(2048, m)
    inv_freq = ROPE_THETA ** (-jnp.arange(0, DA_QK_DIM, 2, dtype=F32) / DA_QK_DIM)
    invf = jnp.tile(inv_freq, LANES // (DA_QK_DIM // 2)).reshape(1, LANES)
    half = (jnp.arange(LANES) % DA_QK_DIM) < (DA_QK_DIM // 2)
    sgn = jnp.where(half, -1.0, 1.0).astype(F32).reshape(1, LANES)
    return pl.pallas_call(
        _rope_body,
        grid=(m // tm,),
        in_specs=[
            pl.BlockSpec((tm, 1), lambda i: (i, 0)),
            pl.BlockSpec((1, LANES), lambda i: (0, 0)),
            pl.BlockSpec((1, LANES), lambda i: (0, 0)),
        ],
        out_specs=[pl.BlockSpec((tm, LANES), lambda i: (i, 0))] * 2,
        out_shape=[jax.ShapeDtypeStruct((m, LANES), F32)] * 2,
        compiler_params=_cp(("parallel",)),
        name="rope_tables",
    )(positions.reshape(m, 1), invf, sgn)


def _norm_mod(x, nw, scale, shift):
    ms = jnp.mean(x * x, axis=-1, keepdims=True)
    y = x * lax.rsqrt(ms + NORM_EPS) * nw
    return y * (1.0 + scale) + shift


def _rope_kind(group):
    if ROPE_Q0 <= group < ROPE_Q0 + DA_HEADS:
        return "q"
    if ROPE_K0 <= group < ROPE_K0 + DA_HEADS:
        return "k"
    return None


_ROPE_TILES = sorted({g // GROUPS_PER_TILE for g in range(IN_COLS // LANES) if _rope_kind(g)})


def _proj_in_body(x_ref, nw_ref, shift_ref, scale_ref, cos_ref, sin_ref, w_ref, o_ref, h_ref):
    j = pl.program_id(1)

    @pl.when(j == 0)
    def _():
        h_ref[...] = _norm_mod(x_ref[...], nw_ref[...], scale_ref[...], shift_ref[...]).astype(BF16)

    acc = _dot(h_ref[...], w_ref[...])

    for jt in _ROPE_TILES:
        @pl.when(j == jt)
        def _(jt=jt):
            cos = cos_ref[...]
            sin_s = sin_ref[...]
            lo_half = (_iota((1, LANES), 1) % DA_QK_DIM) < (DA_QK_DIM // 2)
            for g in range(GROUPS_PER_TILE):
                slab = acc[:, g * LANES:(g + 1) * LANES]
                kind = _rope_kind(jt * GROUPS_PER_TILE + g)
                if kind is not None:
                    partner = jnp.where(lo_half, pltpu.roll(slab, LANES - 32, 1), pltpu.roll(slab, 32, 1))
                    slab = slab * cos + partner * sin_s
                    if kind == "q":
                        slab = slab * (DA_QK_DIM ** -0.5)
                o_ref[:, g * LANES:(g + 1) * LANES] = slab.astype(o_ref.dtype)

    plain = j != _ROPE_TILES[0]
    for jt in _ROPE_TILES[1:]:
        plain = jnp.logical_and(plain, j != jt)

    @pl.when(plain)
    def _():
        o_ref[...] = acc.astype(o_ref.dtype)


def _proj_in_call(x2, nw, ada3, cos, sin_s, w_bf16, layer, bsz, seq):
    m, d = x2.shape
    n = w_bf16.shape[1]
    tm = min(1024, seq)
    per_b = seq // tm
    base = layer * bsz * 6

    def ada_spec(k):
        return pl.BlockSpec((None, 1, d), lambda i, j: (base + (i // per_b) * 6 + k, 0, 0))

    return pl.pallas_call(
        _proj_in_body,
        grid=(m // tm, n // PROJ_TN),
        in_specs=[
            pl.BlockSpec((tm, d), lambda i, j: (i, 0)),
            pl.BlockSpec((1, d), lambda i, j: (0, 0)),
            ada_spec(0),
            ada_spec(1),
            pl.BlockSpec((tm, LANES), lambda i, j: (i, 0)),
            pl.BlockSpec((tm, LANES), lambda i, j: (i, 0)),
            pl.BlockSpec((d, PROJ_TN), lambda i, j: (0, j)),
        ],
        out_specs=pl.BlockSpec((tm, PROJ_TN), lambda i, j: (i, j)),
        out_shape=jax.ShapeDtypeStruct((m, n), BF16),
        scratch_shapes=[pltpu.VMEM((tm, d), BF16)],
        compiler_params=_cp(("parallel", "arbitrary")),
        name="proj_in",
    )(x2, nw, ada3, ada3, cos, sin_s, w_bf16)


def _attn_body(lam_ref, q_ref, k_ref, v_ref, sw_ref, o_ref, vt_ref, *, tq, lam_init):
    i = pl.program_id(1)

    @pl.when(i == 0)
    def _():
        for h in range(DA_HEADS):
            vt_ref[h] = v_ref[:, h * LANES:(h + 1) * LANES].astype(F32).T.astype(BF16)

    lv = lam_ref[...]
    lam = (jnp.exp(jnp.sum(lv[0:1] * lv[1:2], axis=1, keepdims=True))
           - jnp.exp(jnp.sum(lv[2:3] * lv[3:4], axis=1, keepdims=True)) + lam_init)

    lane = _iota((1, LANES), 1)
    zero = jnp.zeros((), BF16)
    qs = []
    for h in range(DA_HEADS):
        q = q_ref[:, h * LANES:(h + 1) * LANES]
        q0 = jnp.where(lane < DA_QK_DIM, q, zero)
        q1 = jnp.where(lane >= DA_QK_DIM, q, zero)
        qs.append(jnp.concatenate([q0, q1], axis=0))

    kk = _iota((tq, 2 * tq), 0)
    qq = _iota((tq, 2 * tq), 1)
    allowed = jnp.right_shift(kk, 6) <= jnp.right_shift(jnp.bitwise_and(qq, tq - 1), 6)

    def step(j, carry, masked):
        start = pl.multiple_of(j * tq, tq)
        out = [None] * DA_HEADS
        for g0 in range(0, DA_HEADS, ATTN_LOCKSTEP):
            grp = range(g0, g0 + ATTN_LOCKSTEP)
            ss = {h: _dot_nt(k_ref[pl.ds(start, tq), h * LANES:(h + 1) * LANES], qs[h]) for h in grp}
            ps, stats = {}, {}
            for h in grp:
                m_i, l_i, _ = carry[h]
                s = ss[h]
                if masked:
                    s = jnp.where(allowed, s, -jnp.inf)
                m_new = jnp.maximum(m_i, jnp.max(s, axis=0, keepdims=True))
                p = jnp.exp(s - m_new)
                alpha = jnp.exp(m_i - m_new)
                l_new = alpha * l_i + jnp.sum(p, axis=0, keepdims=True)
                ps[h] = p.astype(BF16)
                stats[h] = (m_new, l_new, alpha)
            for h in grp:
                m_new, l_new, alpha = stats[h]
                pv = _dot(vt_ref[h, :, pl.ds(start, tq)], ps[h])
                out[h] = (m_new, l_new, alpha * carry[h][2] + pv)
        return tuple(out)

    init = tuple((jnp.full((1, 2 * tq), -jnp.inf, F32), jnp.zeros((1, 2 * tq), F32),
                  jnp.zeros((DA_V_DIM, 2 * tq), F32)) for _ in range(DA_HEADS))
    carry = lax.fori_loop(0, i, lambda j, c: step(j, c, False), init)
    carry = step(i, carry, True)

    for h in range(DA_HEADS):
        _, l_i, acc = carry[h]
        o_t = acc / l_i
        o_t = o_t[:, :tq] - lam * o_t[:, tq:]
        ms = jnp.mean(o_t * o_t, axis=0, keepdims=True)
        o = (o_t * lax.rsqrt(ms + NORM_EPS)).T * sw_ref[...] * (1.0 - lam_init)
        o_ref[:, h * LANES:(h + 1) * LANES] = o.astype(o_ref.dtype)


def _attn_call(u3, lam_vecs, subln_w, layer):
    bsz, seq, _ = u3.shape
    tq = min(256, seq)
    lam_init = 0.8 - 0.6 * math.exp(-0.3 * layer)
    return pl.pallas_call(
        functools.partial(_attn_body, tq=tq, lam_init=lam_init),
        grid=(bsz, seq // tq),
        in_specs=[
            pl.BlockSpec((4, DA_QK_DIM), lambda b, i: (0, 0)),
            pl.BlockSpec((None, tq, HEADW), lambda b, i: (b, i, COL_Q // HEADW)),
            pl.BlockSpec((None, seq, HEADW), lambda b, i: (b, 0, COL_K // HEADW)),
            pl.BlockSpec((None, seq, HEADW), lambda b, i: (b, 0, COL_V // HEADW)),
            pl.BlockSpec((1, LANES), lambda b, i: (0, 0)),
        ],
        out_specs=pl.BlockSpec((None, tq, HEADW), lambda b, i: (b, i, 0)),
        out_shape=jax.ShapeDtypeStruct((bsz, seq, HEADW), BF16),
        scratch_shapes=[pltpu.VMEM((DA_HEADS, DA_V_DIM, seq), BF16)],
        compiler_params=_cp(("parallel", "arbitrary")),
        name="diff_attn",
    )(lam_vecs, u3, u3, u3, subln_w.reshape(1, LANES))


def _hgrn_body(lbp_ref, z_ref, i_ref, q_ref, g_ref, nw_ref, o_ref, *, layer, nchunks):
    lbp = lbp_ref[...]
    e = jnp.exp(lbp - jnp.max(lbp, axis=0, keepdims=True))
    sm = e / jnp.sum(e, axis=0, keepdims=True)
    cs = sm[0:1]
    for t in range(1, layer + 1):
        cs = cs + sm[t:t + 1]
    lb = cs - sm[0:1]
    log_lb = jnp.log(lb)
    log1m = jnp.log1p(-lb)

    rr = _iota((CHUNK, CHUNK), 0)
    cc = _iota((CHUNK, CHUNK), 1)
    sub_start = jnp.bitwise_and(rr, ~(SUB - 1))
    cs_mat = jnp.concatenate([(cc <= rr).astype(BF16), (cc < sub_start).astype(BF16)], axis=0)
    diag_ok = jnp.logical_and(cc <= rr, jnp.right_shift(cc, 4) == jnp.right_shift(rr, 4))
    row = _iota((CHUNK, 1), 0)
    nsub = CHUNK // SUB
    heads = range(HG_HEADS)

    def hcols(x, h):
        return x[:, h * HG_DIM:(h + 1) * HG_DIM]

    def chunk(c, states):
        sl = pl.ds(pl.multiple_of(c * CHUNK, CHUNK), CHUNK)
        z = z_ref[sl, :].astype(F32)
        y = log1m - _softplus(-z)
        mx = jnp.maximum(log_lb, y)
        lf = mx + jnp.log1p(jnp.exp(-jnp.abs(log_lb - y)))
        key = (1.0 - lb) * _sigmoid(-z)
        qc = q_ref[sl, :].astype(F32)
        vc = i_ref[sl, :]
        vt = vc.astype(F32).T.astype(BF16)

        tot = _dot_exact_lhs(cs_mat, lf)
        b = tot[:CHUNK]
        d = b - tot[CHUNK:]
        b_last = b[CHUNK - 1:CHUNK, :]

        q_d = (qc * jnp.exp(d)).astype(BF16)
        k_d = (key * jnp.exp(-d)).astype(BF16)
        q_js, k_js = [], []
        for jb in range(nsub - 1):
            e_j = b[(jb + 1) * SUB - 1:(jb + 1) * SUB, :]
            q_j = jnp.where(row >= (jb + 1) * SUB, qc * jnp.exp(jnp.minimum(b - e_j, 0.0)), 0.0)
            in_j = jnp.logical_and(row >= jb * SUB, row < (jb + 1) * SUB)
            k_j = jnp.where(in_j, key * jnp.exp(jnp.minimum(e_j - b, 0.0)), 0.0)
            q_js.append(q_j.astype(BF16))
            k_js.append(k_j.astype(BF16))
        q_in = (qc * jnp.exp(b)).astype(BF16)
        k_out = (key * jnp.exp(b_last - b)).astype(BF16)
        decay = jnp.exp(b_last)

        s_diag = [_dot_nt(hcols(q_d, h), hcols(k_d, h)) for h in heads]
        s_off = [_dot_nt(jnp.concatenate([hcols(x, h) for x in q_js], axis=1),
                         jnp.concatenate([hcols(x, h) for x in k_js], axis=1)) for h in heads]
        scores = [(jnp.where(diag_ok, s_diag[h], 0.0) + s_off[h]).astype(BF16) for h in heads]
        o_intra = [_dot(scores[h], hcols(vc, h)) for h in heads]
        o_inter = [_dot_nt(hcols(q_in, h), states[h].astype(BF16)) for h in heads]
        upd = [_dot(vt[h * HG_DIM:(h + 1) * HG_DIM, :], hcols(k_out, h)) for h in heads]
        new_states = tuple(states[h] * hcols(decay, h) + upd[h] for h in heads)

        outs = []
        for h in heads:
            o = o_intra[h] + o_inter[h]
            ms = jnp.mean(o * o, axis=-1, keepdims=True)
            outs.append(o * lax.rsqrt(ms + NORM_EPS))
        g = g_ref[sl, :].astype(F32)
        o = jnp.concatenate(outs, axis=1) * nw_ref[...] * (g * _sigmoid(g))
        o_ref[sl, :] = o.astype(o_ref.dtype)
        return new_states

    init = tuple(jnp.zeros((HG_DIM, HG_DIM), F32) for _ in heads)
    lax.fori_loop(0, nchunks, chunk, init)


def _hgrn_call(u3, hg_lb, norm_w, layer):
    bsz, seq, _ = u3.shape
    nl = hg_lb.shape[0]

    def col(base):
        return lambda b: (b, 0, base // HEADW)

    blk = (None, seq, HEADW)
    return pl.pallas_call(
        functools.partial(_hgrn_body, layer=layer, nchunks=seq // CHUNK),
        grid=(bsz,),
        in_specs=[
            pl.BlockSpec((nl, HEADW), lambda b: (0, 0)),
            pl.BlockSpec(blk, col(COL_HF)),
            pl.BlockSpec(blk, col(COL_HI)),
            pl.BlockSpec(blk, col(COL_HQ)),
            pl.BlockSpec(blk, col(COL_HG)),
            pl.BlockSpec((1, HEADW), lambda b: (0, 0)),
        ],
        out_specs=pl.BlockSpec(blk, lambda b: (b, 0, 0)),
        out_shape=jax.ShapeDtypeStruct((bsz, seq, HEADW), BF16),
        compiler_params=_cp(("parallel",)),
        name="hgrn2",
    )(hg_lb, u3, u3, u3, u3, jnp.tile(norm_w, HG_HEADS).reshape(1, HEADW))


def _head_ones(n, width):
    r = _iota((n, n), 0) // width
    c = _iota((n, n), 1) // width
    return (r == c).astype(BF16)


def _rw_prep_body(*refs, has_vres):
    (r_ref, k_ref, v_ref, l_ref, rp_ref, kp_ref, vp_ref, lp_ref,
     mu_ref, w0_ref, w2_ref, a0_ref, a2_ref, g2_ref, kk_ref, ka_ref) = refs[:16]
    if has_vres:
        vf_ref, v0_ref, v1_ref, v2_ref = refs[16:20]
    r_o, k_o, v_o, lw_o, kkn_o, bk_o, g_o = refs[-7:]
    j = pl.program_id(1)
    ts = r_ref.shape[0]
    row = _iota((ts, 1), 0)
    first = (j > 0).astype(F32)

    def shifted(cur_ref, prev_ref, lo, hi):
        u = cur_ref[...].astype(F32)
        prev_row = prev_ref[SUB - 1:SUB, :].astype(F32) * first
        u_prev = jnp.where(row == 0, prev_row, pltpu.roll(u, 1, 0))
        return u + (u_prev - u) * mu_ref[:, lo:hi]

    r = shifted(r_ref, rp_ref, 0, 512)
    k = shifted(k_ref, kp_ref, 512, 1024)
    v = shifted(v_ref, vp_ref, 1024, 1536)
    lora = shifted(l_ref, lp_ref, 1536, 1792)
    w_lo = lora[:, 0:64]
    a_lo = lora[:, 64:128]
    g_lo = lora[:, 128:256]

    wpre = w0_ref[...] + _dot_hp(jnp.tanh(w_lo), w2_ref[...])
    w = -_softplus(-wpre) - 0.5
    lw = -jnp.exp(w)
    a = _sigmoid(a0_ref[...] + _dot_hp(a_lo, a2_ref[...]))
    g = _dot_hp(_sigmoid(g_lo), g2_ref[...])
    if has_vres:
        mix = _sigmoid(v0_ref[...] + _dot_hp(_dot_hp(v, v1_ref[...]), v2_ref[...]))
        v = v + (vf_ref[...] - v) * mix

    kk = k * kk_ref[...]
    ssq = _dot_exact_rhs(kk * kk, _head_ones(RW_WIDTH, RW_DIM))
    kkn = kk * lax.rsqrt(jnp.maximum(ssq, 1e-24))
    k = k * (1.0 + (a - 1.0) * ka_ref[...])

    r_o[...] = r
    k_o[...] = k
    v_o[...] = v
    lw_o[...] = lw
    kkn_o[...] = kkn
    bk_o[...] = kkn * a
    g_o[...] = g


def _rw_prep_call(u3, mu, w0, w2, a0, a2, g2, k_k, k_a, vres):
    bsz, seq, _ = u3.shape
    ts = min(512, seq)
    w = RW_WIDTH
    has_vres = vres is not None

    def full(arr):
        return pl.BlockSpec(arr.shape, lambda b, j: (0,) * arr.ndim)

    def cur(col, width):
        return pl.BlockSpec((None, ts, width), lambda b, j: (b, j, col // width))

    def prev(col, width):
        return pl.BlockSpec((None, SUB, width),
                            lambda b, j: (b, jnp.maximum(j * (ts // SUB) - 1, 0), col // width))

    params = [mu.reshape(1, -1), w0.reshape(1, w), w2, a0.reshape(1, w), a2, g2,
              k_k.reshape(1, w), k_a.reshape(1, w)]
    in_specs = [cur(COL_RR, w), cur(COL_RK, w), cur(COL_RV, w), cur(COL_RL, RW_LORA_W),
                prev(COL_RR, w), prev(COL_RK, w), prev(COL_RV, w), prev(COL_RL, RW_LORA_W)]
    in_specs += [full(p) for p in params]
    args = [u3] * 8 + params
    if has_vres:
        v_first, v0, v1, v2 = vres
        extra = [v0.reshape(1, w), v1, v2]
        in_specs += [pl.BlockSpec((None, ts, w), lambda b, j: (b, j, 0))] + [full(p) for p in extra]
        args += [v_first] + extra
    out_spec = pl.BlockSpec((None, ts, w), lambda b, j: (b, j, 0))
    return pl.pallas_call(
        functools.partial(_rw_prep_body, has_vres=has_vres),
        grid=(bsz, seq // ts),
        in_specs=in_specs,
        out_specs=[out_spec] * 7,
        out_shape=[jax.ShapeDtypeStruct((bsz, seq, w), F32)] * 7,
        compiler_params=_cp(("parallel", "parallel")),
        name="rwkv_prep",
    )(*args)


def _rw_scan_body(r_ref, k_ref, v_ref, lw_ref, kk_ref, bk_ref, g_ref, rk_ref, gw_ref, gb_ref,
                  o_ref, state_ref, oacc_ref, *, nsub):
    @pl.when(pl.program_id(1) == 0)
    def _():
        state_ref[...] = jnp.zeros_like(state_ref)

    t = CHUNK
    gw = RW_GW
    tc = nsub * t
    units = [(s, grp) for s in range(nsub) for grp in range(RW_GROUPS)]

    rr = _iota((tc, tc), 0)
    cc = _iota((tc, tc), 1)
    tril_bd = jnp.logical_and(cc <= rr, (cc // t) == (rr // t)).astype(BF16)
    rr4 = _iota((t, gw), 0)
    cj = jnp.bitwise_and(_iota((t, gw), 1), t - 1)
    strict = cj < rr4
    incl = cj <= rr4
    eye_c = (cj == rr4).astype(F32)
    bd_mask = (_iota((gw, gw), 0) // t) == (_iota((gw, gw), 1) // RW_DIM)
    zero_b = jnp.zeros((), BF16)

    def bd(xc):
        xb = xc.astype(BF16)
        return jnp.where(bd_mask, jnp.concatenate([xb] * RW_PACK, axis=0), zero_b)

    def cut(x, u):
        s, grp = u
        return x[s * t:(s + 1) * t, grp * gw:(grp + 1) * gw]

    r = r_ref[...]
    k = k_ref[...]
    v = v_ref[...]
    lw = lw_ref[...]
    kk = kk_ref[...]
    bk = bk_ref[...]
    c = _dot_exact_lhs(tril_bd, lw)
    e_neg = jnp.exp(-c)
    a_t = -kk * jnp.exp(c - lw)
    b_t = bk * e_neg
    k_t = k * e_neg
    r_t = r * jnp.exp(c)
    c_last = jnp.concatenate(
        [jnp.broadcast_to(c[(s + 1) * t - 1:(s + 1) * t, :], (t, RW_WIDTH)) for s in range(nsub)], axis=0)
    decay_out = jnp.exp(c_last - c)
    b_g = bk * decay_out
    k_g = k * decay_out
    gamma = jnp.exp(c_last)

    ar = {u: jnp.concatenate([cut(a_t, u), cut(r_t, u)], axis=0).astype(BF16) for u in units}
    x_bk = {u: _dot_nt(ar[u], jnp.concatenate([bd(cut(b_t, u)), bd(cut(k_t, u))], axis=0)) for u in units}
    l_c = {u: jnp.where(strict, x_bk[u][:t, :gw], 0.0) for u in units}
    m_c = {u: jnp.where(strict, x_bk[u][:t, gw:], 0.0).astype(BF16) for u in units}
    rb = {u: jnp.where(incl, x_bk[u][t:, :gw], 0.0).astype(BF16) for u in units}
    rkm = {u: jnp.where(incl, x_bk[u][t:, gw:], 0.0).astype(BF16) for u in units}
    v_bd = {u: bd(cut(v, u)) for u in units}

    p = {u: _dot(l_c[u].astype(BF16), bd(l_c[u])) for u in units}
    s_inv = {u: eye_c + l_c[u] for u in units}
    mv = {u: _dot(m_c[u], v_bd[u]) for u in units}
    o_kv = {u: _dot(rkm[u], v_bd[u]) for u in units}
    for lvl in range(1, 6):
        last = lvl == 5
        nxt_p, nxt_s = {}, {}
        for u in units:
            pb = p[u].astype(BF16)
            if last:
                nxt_s[u] = s_inv[u] + _dot(pb, bd(s_inv[u]))
            else:
                res = _dot(pb, jnp.concatenate([bd(p[u]), bd(s_inv[u])], axis=1))
                nxt_p[u] = res[:, :gw]
                nxt_s[u] = s_inv[u] + res[:, gw:]
        p, s_inv = nxt_p, nxt_s

    wu = {u: _dot(s_inv[u].astype(BF16), jnp.concatenate([bd(cut(a_t, u)), bd(mv[u])], axis=1)) for u in units}
    w_t = {u: wu[u][:, :gw] for u in units}
    u_t = {u: wu[u][:, gw:] for u in units}
    rw = {u: _dot(rb[u], jnp.concatenate([bd(w_t[u]), bd(u_t[u])], axis=1)) for u in units}
    r_hat = {u: (cut(r_t, u) + rw[u][:, :gw]).astype(BF16) for u in units}
    o_hat = {u: rw[u][:, gw:] + o_kv[u] for u in units}
    g_mat = {u: jnp.where(bd_mask, _dot(w_t[u].T.astype(BF16), cut(b_g, u).astype(BF16)), 0.0).astype(BF16)
             for u in units}
    c_mat = {}
    for u in units:
        uv_t = jnp.concatenate([u_t[u], cut(v, u)], axis=0).T.astype(BF16)
        bkg = jnp.concatenate([cut(b_g, u), cut(k_g, u)], axis=0).astype(BF16)
        c_mat[u] = jnp.where(bd_mask, _dot(uv_t, bkg), 0.0)

    for s in range(nsub):
        for grp in range(RW_GROUPS):
            u = (s, grp)
            st = state_ref[grp]
            st_b = st.astype(BF16)
            oacc_ref[s * t:(s + 1) * t, grp * gw:(grp + 1) * gw] = _dot_nt(r_hat[u], st_b) + o_hat[u]
            g_row = gamma[(s + 1) * t - 1:(s + 1) * t, grp * gw:(grp + 1) * gw]
            state_ref[grp] = st * g_row + _dot(st_b, g_mat[u]) + c_mat[u]

    ones_h = _head_ones(RW_WIDTH, RW_DIM)
    o = oacc_ref[...]
    mean = _dot_exact_rhs(o, ones_h) * (1.0 / RW_DIM)
    dlt = o - mean
    var = _dot_exact_rhs(dlt * dlt, ones_h) * (1.0 / RW_DIM)
    on = dlt * lax.rsqrt(var + RW_GN_EPS) * gw_ref[...] + gb_ref[...]
    bonus = _dot_exact_rhs(r * k * rk_ref[...], ones_h)
    o_ref[...] = ((on + bonus * v) * g_ref[...]).astype(o_ref.dtype)


def _rw_scan_call(r, k, v, lw, kkn, bk, g, r_k, gn_w, gn_b):
    bsz, seq, w = r.shape
    tc = min(256, seq)
    spec = pl.BlockSpec((None, tc, w), lambda b, j: (b, j, 0))
    pspec = pl.BlockSpec((1, w), lambda b, j: (0, 0))
    return pl.pallas_call(
        functools.partial(_rw_scan_body, nsub=tc // CHUNK),
        grid=(bsz, seq // tc),
        in_specs=[spec] * 7 + [pspec] * 3,
        out_specs=spec,
        out_shape=jax.ShapeDtypeStruct((bsz, seq, w), BF16),
        scratch_shapes=[pltpu.VMEM((RW_GROUPS, RW_GW, RW_GW), F32), pltpu.VMEM((tc, w), F32)],
        compiler_params=_cp(("parallel", "arbitrary")),
        name="rwkv_scan",
    )(r, k, v, lw, kkn, bk, g, r_k.reshape(1, w), gn_w.reshape(1, w), gn_b.reshape(1, w))


def _merge_body(ga_ref, gb_ref, gc_ref, oa_ref, ob_ref, oc_ref, x_ref, gate_ref,
                wa_ref, wb_ref, wc_ref, wo_ref, o_ref):
    merged = (_sigmoid(ga_ref[...].astype(F32)) * _dot(oa_ref[...], wa_ref[...])
              + _sigmoid(gb_ref[...].astype(F32)) * _dot(ob_ref[...], wb_ref[...])
              + _sigmoid(gc_ref[...].astype(F32)) * _dot(oc_ref[...], wc_ref[...]))
    mix = _dot(merged.astype(BF16), wo_ref[...])
    o_ref[...] = x_ref[...] + gate_ref[...] * mix


def _merge_call(u2, o_a, o_b, o_c, x2, ada3, wa, wb, wc, wo, layer, bsz, seq):
    m, d = x2.shape
    tm = min(512, seq)
    per_b = seq // tm
    base = layer * bsz * 6
    wdt = o_a.shape[1]

    def const(arr):
        return pl.BlockSpec(arr.shape, lambda i: (0, 0))

    return pl.pallas_call(
        _merge_body,
        grid=(m // tm,),
        in_specs=[
            pl.BlockSpec((tm, d), lambda i: (i, 0)),
            pl.BlockSpec((tm, d), lambda i: (i, 1)),
            pl.BlockSpec((tm, d), lambda i: (i, 2)),
            pl.BlockSpec((tm, wdt), lambda i: (i, 0)),
            pl.BlockSpec((tm, wdt), lambda i: (i, 0)),
            pl.BlockSpec((tm, wdt), lambda i: (i, 0)),
            pl.BlockSpec((tm, d), lambda i: (i, 0)),
            pl.BlockSpec((None, 1, d), lambda i: (base + (i // per_b) * 6 + 2, 0, 0)),
            const(wa), const(wb), const(wc), const(wo),
        ],
        out_specs=pl.BlockSpec((tm, d), lambda i: (i, 0)),
        out_shape=jax.ShapeDtypeStruct((m, d), F32),
        compiler_params=_cp(("parallel",)),
        name="merge_out",
    )(u2, u2, u2, o_a, o_b, o_c, x2, ada3, wa, wb, wc, wo)


def _ffn_body(x_ref, nw_ref, shift_ref, scale_ref, gate_ref, wi_ref, wo_ref, fn_ref, o_ref, *, final):
    x = x_ref[...]
    h = _norm_mod(x, nw_ref[...], scale_ref[...], shift_ref[...]).astype(BF16)
    half = FFN_HIDDEN // 2
    acc = jnp.zeros(x.shape, F32)
    for c in range(2):
        gh = _dot(h, wi_ref[:, c * half:(c + 1) * half])
        uh = _dot(h, wi_ref[:, FFN_HIDDEN + c * half:FFN_HIDDEN + (c + 1) * half])
        act = (gh * _sigmoid(gh) * uh).astype(BF16)
        acc = acc + _dot(act, wo_ref[c * half:(c + 1) * half, :])
    y = x + gate_ref[...] * acc
    if final:
        ms = jnp.mean(y * y, axis=-1, keepdims=True)
        y = y * lax.rsqrt(ms + NORM_EPS) * fn_ref[...]
    o_ref[...] = y


def _ffn_call(x2, nw, ada3, wi, wo, final_w, layer, bsz, seq, final):
    m, d = x2.shape
    tm = min(512, seq)
    per_b = seq // tm
    base = layer * bsz * 6

    def ada_spec(k):
        return pl.BlockSpec((None, 1, d), lambda i: (base + (i // per_b) * 6 + k, 0, 0))

    def const(arr):
        return pl.BlockSpec(arr.shape, lambda i: (0, 0), pipeline_mode=pl.Buffered(1))

    return pl.pallas_call(
        functools.partial(_ffn_body, final=final),
        grid=(m // tm,),
        in_specs=[
            pl.BlockSpec((tm, d), lambda i: (i, 0)),
            pl.BlockSpec((1, d), lambda i: (0, 0)),
            ada_spec(3), ada_spec(4), ada_spec(5),
            const(wi), const(wo),
            pl.BlockSpec((1, d), lambda i: (0, 0)),
        ],
        out_specs=pl.BlockSpec((tm, d), lambda i: (i, 0)),
        out_shape=jax.ShapeDtypeStruct((m, d), F32),
        compiler_params=_cp(("parallel",)),
        name="ffn_final" if final else "ffn",
    )(x2, nw, ada3, ada3, ada3, wi, wo, final_w)


def _permute_w_in(w):
    q, k, v = w[:, 0:512], w[:, 512:1024], w[:, 1024:1536]
    hgrn = w[:, 1536:3584]
    rw = w[:, 3584:5376]
    gates = w[:, 5376:8448]
    return jnp.concatenate([gates, hgrn, q, k, v, rw], axis=1)


def kernel(x, c, positions, ada_w, ada_b, norm_mix_w, norm_ffn_w, w_in, da_lambda, da_subln_w, hg_lb, hg_norm_w, rw_mu, rw_w0, rw_w2, rw_a0, rw_a2, rw_g2, rw_k_k, rw_k_a, rw_r_k, rw_gn_w, rw_gn_b, rw_v0, rw_v1, rw_v2, w_branch_a, w_branch_b, w_branch_c, w_out, ffn_w_in, ffn_w_out, final_norm_w):
    bsz, seq, d = x.shape
    depth = ada_w.shape[0]
    m = bsz * seq

    ada = _ada_call(c, ada_w, ada_b)
    ada3 = ada.reshape(depth * bsz * 6, 1, d)
    cos, sin_s = _rope_call(positions)

    x2 = x.reshape(m, d)
    v_first = None
    for l in range(depth):
        w_l = _permute_w_in(w_in[l]).astype(BF16)
        u2 = _proj_in_call(x2, norm_mix_w[l].reshape(1, d), ada3, cos, sin_s, w_l, l, bsz, seq)
        u3 = u2.reshape(bsz, seq, IN_COLS)

        o_a = _attn_call(u3, da_lambda[l], da_subln_w[l], l)
        o_b = _hgrn_call(u3, hg_lb, hg_norm_w[l], l)
        vres = None if l == 0 else (v_first, rw_v0[l - 1], rw_v1[l - 1], rw_v2[l - 1])
        r, k, v, lw, kkn, bk, g = _rw_prep_call(u3, rw_mu[l], rw_w0[l], rw_w2[l], rw_a0[l], rw_a2[l],
                                                rw_g2[l], rw_k_k[l], rw_k_a[l], vres)
        if l == 0:
            v_first = v
        o_c = _rw_scan_call(r, k, v, lw, kkn, bk, g, rw_r_k[l], rw_gn_w[l], rw_gn_b[l])

        x2 = _merge_call(u2, o_a.reshape(m, -1), o_b.reshape(m, -1), o_c.reshape(m, -1), x2, ada3,
                         w_branch_a[l].astype(BF16), w_branch_b[l].astype(BF16),
                         w_branch_c[l].astype(BF16), w_out[l].astype(BF16), l, bsz, seq)
        x2 = _ffn_call(x2, norm_ffn_w[l].reshape(1, d), ada3, ffn_w_in[l].astype(BF16),
                       ffn_w_out[l].astype(BF16), final_norm_w.reshape(1, d), l, bsz, seq,
                       final=(l == depth - 1))
    return x2.reshape(bsz, seq, d)
```

```python
import functools
import math

import jax
import jax.numpy as jnp
from jax import lax
from jax.experimental import pallas as pl
from jax.experimental.pallas import tpu as pltpu

F32 = jnp.float32
BF16 = jnp.bfloat16

D_MODEL = 1024
CHUNK = 64
ROPE_THETA = 10000.0
NORM_EPS = 1e-6

DA_HEADS = 4
DA_QK_DIM = 64
DA_V_DIM = 128
HG_HEADS = 4
HG_DIM = 128
RW_HEADS = 8
RW_DIM = 64
RW_WIDTH = 512
RW_GN_EPS = 64e-5
FFN_HIDDEN = 2816
IN_COLS = 8448

LANES = 128
ATTN_LOCKSTEP = 4
LOG2E = 1.4426950408889634
Q_SCALE = DA_QK_DIM ** -0.5 * LOG2E
VT_ROWS = DA_V_DIM + 16
SUB = 16
RW_PACK = 4
RW_GW = RW_PACK * RW_DIM
RW_GROUPS = RW_HEADS // RW_PACK

COL_GATES = 0
COL_HF = 3072
COL_HI = 3584
COL_HQ = 4096
COL_HG = 4608
COL_Q = 5120
COL_K = 5632
COL_V = 6144
COL_RR = 6656
COL_RK = 7168
COL_RV = 7680
COL_RL = 8192
RW_LORA_W = 256
HEADW = 512
PROJ_TN = 768
GROUPS_PER_TILE = PROJ_TN // LANES
ROPE_Q0 = COL_Q // LANES
ROPE_K0 = COL_K // LANES

VMEM_LIMIT = 56 * 1024 * 1024


def _cp(sem):
    return pltpu.CompilerParams(dimension_semantics=sem, vmem_limit_bytes=VMEM_LIMIT)


def _sigmoid(x):
    return 1.0 / (1.0 + jnp.exp(-x))


def _softplus(x):
    return jnp.maximum(x, 0.0) + jnp.log1p(jnp.exp(-jnp.abs(x)))


def _split3(x):
    hi = x.astype(BF16)
    r1 = x - hi.astype(F32)
    mid = r1.astype(BF16)
    lo = (r1 - mid.astype(F32)).astype(BF16)
    return hi, mid, lo


def _dot(a, b):
    return jnp.dot(a, b, preferred_element_type=F32)


def _dot_nt(a, b):
    return lax.dot_general(a, b, (((1,), (1,)), ((), ())), preferred_element_type=F32)


def _dot_exact_lhs(a01, x):
    n = x.shape[1]
    res = _dot(a01, jnp.concatenate(_split3(x), axis=1))
    return res[:, 0:n] + res[:, n:2 * n] + res[:, 2 * n:3 * n]


def _head_sum(x):
    ones_g = _head_ones(RW_GW, RW_DIM)
    hi = x.astype(BF16)
    lo = (x - hi.astype(F32)).astype(BF16)
    m = x.shape[0]
    outs = []
    for grp in range(x.shape[1] // RW_GW):
        cols = slice(grp * RW_GW, (grp + 1) * RW_GW)
        res = _dot(jnp.concatenate([hi[:, cols], lo[:, cols]], axis=0), ones_g)
        outs.append(res[:m] + res[m:])
    return jnp.concatenate(outs, axis=1)


def _dot_hp(a, b):
    a_hi = a.astype(BF16)
    a_lo = (a - a_hi.astype(F32)).astype(BF16)
    b_hi = b.astype(BF16)
    b_lo = (b - b_hi.astype(F32)).astype(BF16)
    return _dot(a_hi, b_hi) + _dot(a_hi, b_lo) + _dot(a_lo, b_hi)


def _iota(shape, dim):
    return lax.broadcasted_iota(jnp.int32, shape, dim)


def _ada_body(c_ref, w_ref, b_ref, o_ref):
    c = c_ref[...]
    ca = c * _sigmoid(c)
    o_ref[...] = _dot_hp(ca, w_ref[...]) + b_ref[...]


def _ada_call(c, ada_w, ada_b):
    nl, d, n6 = ada_w.shape
    bsz = c.shape[0]
    tn = 1536
    return pl.pallas_call(
        _ada_body,
        grid=(nl, n6 // tn),
        in_specs=[
            pl.BlockSpec((bsz, d), lambda l, j: (0, 0)),
            pl.BlockSpec((None, d, tn), lambda l, j: (l, 0, j)),
            pl.BlockSpec((None, 1, tn), lambda l, j: (l, 0, j)),
        ],
        out_specs=pl.BlockSpec((None, bsz, tn), lambda l, j: (l, 0, j)),
        out_shape=jax.ShapeDtypeStruct((nl, bsz, n6), F32),
        compiler_params=_cp(("parallel", "parallel")),
        name="ada",
    )(c, ada_w, ada_b.reshape(nl, 1, n6))


def _rope_body(pos_ref, invf_ref, sgn_ref, cos_ref, sin_ref):
    ang = pos_ref[...].astype(F32) * invf_ref[...]
    cos_ref[...] = jnp.cos(ang)
    sin_ref[...] = jnp.sin(ang) * sgn_ref[...]


def _rope_call(positions):
    m = positions.size
    tm = min(2048, m)
    inv_freq = ROPE_THETA ** (-jnp.arange(0, DA_QK_DIM, 2, dtype=F32) / DA_QK_DIM)
    invf = jnp.tile(inv_freq, LANES // (DA_QK_DIM // 2)).reshape(1, LANES)
    sgn = jnp.where(jnp.arange(LANES) < LANES // 2, -1.0, 1.0).astype(F32).reshape(1, LANES)
    return pl.pallas_call(
        _rope_body,
        grid=(m // tm,),
        in_specs=[
            pl.BlockSpec((tm, 1), lambda i: (i, 0)),
            pl.BlockSpec((1, LANES), lambda i: (0, 0)),
            pl.BlockSpec((1, LANES), lambda i: (0, 0)),
        ],
        out_specs=[pl.BlockSpec((tm, LANES), lambda i: (i, 0))] * 2,
        out_shape=[jax.ShapeDtypeStruct((m, LANES), F32)] * 2,
        compiler_params=_cp(("parallel",)),
        name="rope_tables",
    )(positions.reshape(m, 1), invf, sgn)


def _norm_mod(x, nw, scale, shift):
    ms = jnp.mean(x * x, axis=-1, keepdims=True)
    y = x * lax.rsqrt(ms + NORM_EPS) * nw
    return y * (1.0 + scale) + shift


def _rope_kind(group):
    if ROPE_Q0 <= group < ROPE_Q0 + DA_HEADS:
        return "q"
    if ROPE_K0 <= group < ROPE_K0 + DA_HEADS:
        return "k"
    return None


_ROPE_TILES = sorted({g // GROUPS_PER_TILE for g in range(IN_COLS // LANES) if _rope_kind(g)})


def _proj_in_body(x_ref, nw_ref, shift_ref, scale_ref, cos_ref, sin_ref, w_ref, o_ref, h_ref):
    j = pl.program_id(1)

    @pl.when(j == 0)
    def _():
        h_ref[...] = _norm_mod(x_ref[...], nw_ref[...], scale_ref[...], shift_ref[...]).astype(BF16)

    acc = _dot(h_ref[...], w_ref[...])

    for jt in _ROPE_TILES:
        @pl.when(j == jt)
        def _(jt=jt):
            cos = cos_ref[...]
            sin_s = sin_ref[...]
            for g in range(GROUPS_PER_TILE):
                slab = acc[:, g * LANES:(g + 1) * LANES]
                kind = _rope_kind(jt * GROUPS_PER_TILE + g)
                if kind is not None:
                    slab = slab * cos + pltpu.roll(slab, LANES // 2, 1) * sin_s
                    if kind == "q":
                        slab = slab * Q_SCALE
                o_ref[:, g * LANES:(g + 1) * LANES] = slab.astype(o_ref.dtype)

    plain = j != _ROPE_TILES[0]
    for jt in _ROPE_TILES[1:]:
        plain = jnp.logical_and(plain, j != jt)

    @pl.when(plain)
    def _():
        o_ref[...] = acc.astype(o_ref.dtype)


def _proj_in_call(x2, nw, ada3, cos, sin_s, w_bf16, layer, bsz, seq):
    m, d = x2.shape
    n = w_bf16.shape[1]
    tm = min(2048, seq)
    per_b = seq // tm
    base = layer * bsz * 6

    def ada_spec(k):
        return pl.BlockSpec((None, 1, d), lambda i, j: (base + (i // per_b) * 6 + k, 0, 0))

    return pl.pallas_call(
        _proj_in_body,
        grid=(m // tm, n // PROJ_TN),
        in_specs=[
            pl.BlockSpec((tm, d), lambda i, j: (i, 0)),
            pl.BlockSpec((1, d), lambda i, j: (0, 0)),
            ada_spec(0),
            ada_spec(1),
            pl.BlockSpec((tm, LANES), lambda i, j: (i, 0)),
            pl.BlockSpec((tm, LANES), lambda i, j: (i, 0)),
            pl.BlockSpec((d, PROJ_TN), lambda i, j: (0, j)),
        ],
        out_specs=pl.BlockSpec((tm, PROJ_TN), lambda i, j: (i, j)),
        out_shape=jax.ShapeDtypeStruct((m, n), BF16),
        scratch_shapes=[pltpu.VMEM((tm, d), BF16)],
        compiler_params=_cp(("parallel", "arbitrary")),
        name="proj_in",
    )(x2, nw, ada3, ada3, cos, sin_s, w_bf16)


def _attn_body(lam_ref, q_ref, k_ref, v_ref, sw_ref, o_ref, vt_ref, *, tq, lam_init):
    i = pl.program_id(1)

    @pl.when(i == 0)
    def _():
        for h in range(DA_HEADS):
            vt_ref[h, 0:DA_V_DIM, :] = v_ref[:, h * LANES:(h + 1) * LANES].astype(F32).T.astype(BF16)
            vt_ref[h, DA_V_DIM:VT_ROWS, :] = jnp.ones((VT_ROWS - DA_V_DIM, vt_ref.shape[2]), BF16)

    lv = lam_ref[...]
    lam = (jnp.exp(jnp.sum(lv[0:1] * lv[1:2], axis=1, keepdims=True))
           - jnp.exp(jnp.sum(lv[2:3] * lv[3:4], axis=1, keepdims=True)) + lam_init)

    map0 = jnp.bitwise_and(_iota((1, LANES), 1), DA_QK_DIM - 1) < DA_QK_DIM // 2
    zero = jnp.zeros((), BF16)
    qs = []
    for h in range(DA_HEADS):
        q = q_ref[:, h * LANES:(h + 1) * LANES]
        q0 = jnp.where(map0, q, zero)
        q1 = jnp.where(map0, zero, q)
        qs.append(jnp.concatenate([q0, q1], axis=0))

    kk = _iota((tq, 2 * tq), 0)
    qq = _iota((tq, 2 * tq), 1)
    allowed = jnp.right_shift(kk, 6) <= jnp.right_shift(jnp.bitwise_and(qq, tq - 1), 6)

    def step(j, carry, masked):
        start = pl.multiple_of(j * tq, tq)
        out = [None] * DA_HEADS
        for g0 in range(0, DA_HEADS, ATTN_LOCKSTEP):
            grp = range(g0, g0 + ATTN_LOCKSTEP)
            ss = {h: _dot_nt(k_ref[pl.ds(start, tq), h * LANES:(h + 1) * LANES], qs[h]) for h in grp}
            ps, stats = {}, {}
            for h in grp:
                m_i, _ = carry[h]
                s = ss[h]
                if masked:
                    s = jnp.where(allowed, s, -jnp.inf)
                m_new = jnp.maximum(m_i, jnp.max(s, axis=0, keepdims=True))
                ps[h] = jnp.exp2(s - m_new).astype(BF16)
                stats[h] = (m_new, jnp.exp2(m_i - m_new))
            for h in grp:
                m_new, alpha = stats[h]
                pv = _dot(vt_ref[h, :, pl.ds(start, tq)], ps[h])
                out[h] = (m_new, alpha * carry[h][1] + pv)
        return tuple(out)

    init = tuple((jnp.full((1, 2 * tq), -jnp.inf, F32), jnp.zeros((VT_ROWS, 2 * tq), F32))
                 for _ in range(DA_HEADS))
    carry = lax.fori_loop(0, i, lambda j, c: step(j, c, False), init)
    carry = step(i, carry, True)

    for h in range(DA_HEADS):
        acc = carry[h][1]
        o_t = acc[:DA_V_DIM] / acc[DA_V_DIM:DA_V_DIM + 1]
        o_t = o_t[:, :tq] - lam * o_t[:, tq:]
        ms = jnp.mean(o_t * o_t, axis=0, keepdims=True)
        o = (o_t * lax.rsqrt(ms + NORM_EPS)).T * sw_ref[...] * (1.0 - lam_init)
        o_ref[:, h * LANES:(h + 1) * LANES] = o.astype(o_ref.dtype)


def _attn_call(u3, lam_vecs, subln_w, layer):
    bsz, seq, _ = u3.shape
    tq = min(256, seq)
    lam_init = 0.8 - 0.6 * math.exp(-0.3 * layer)
    return pl.pallas_call(
        functools.partial(_attn_body, tq=tq, lam_init=lam_init),
        grid=(bsz, seq // tq),
        in_specs=[
            pl.BlockSpec((4, DA_QK_DIM), lambda b, i: (0, 0)),
            pl.BlockSpec((None, tq, HEADW), lambda b, i: (b, i, COL_Q // HEADW)),
            pl.BlockSpec((None, seq, HEADW), lambda b, i: (b, 0, COL_K // HEADW)),
            pl.BlockSpec((None, seq, HEADW), lambda b, i: (b, 0, COL_V // HEADW)),
            pl.BlockSpec((1, LANES), lambda b, i: (0, 0)),
        ],
        out_specs=pl.BlockSpec((None, tq, HEADW), lambda b, i: (b, i, 0)),
        out_shape=jax.ShapeDtypeStruct((bsz, seq, HEADW), BF16),
        scratch_shapes=[pltpu.VMEM((DA_HEADS, VT_ROWS, seq), BF16)],
        compiler_params=_cp(("parallel", "arbitrary")),
        name="diff_attn",
    )(lam_vecs, u3, u3, u3, subln_w.reshape(1, LANES))


def _hgrn_body(lbp_ref, z_ref, i_ref, q_ref, g_ref, nw_ref, o_ref, *, layer, nchunks):
    lbp = lbp_ref[...]
    e = jnp.exp(lbp - jnp.max(lbp, axis=0, keepdims=True))
    sm = e / jnp.sum(e, axis=0, keepdims=True)
    cs = sm[0:1]
    for t in range(1, layer + 1):
        cs = cs + sm[t:t + 1]
    lb = cs - sm[0:1]
    log_lb = jnp.log(lb)
    log1m = jnp.log1p(-lb)

    rr = _iota((CHUNK, CHUNK), 0)
    cc = _iota((CHUNK, CHUNK), 1)
    sub_start = jnp.bitwise_and(rr, ~(SUB - 1))
    cs_mat = jnp.concatenate([(cc <= rr).astype(BF16), (cc < sub_start).astype(BF16)], axis=0)
    diag_ok = jnp.logical_and(cc <= rr, jnp.right_shift(cc, 4) == jnp.right_shift(rr, 4))
    row = _iota((CHUNK, 1), 0)
    nsub = CHUNK // SUB
    heads = range(HG_HEADS)

    def hcols(x, h):
        return x[:, h * HG_DIM:(h + 1) * HG_DIM]

    def chunk(c, states):
        sl = pl.ds(pl.multiple_of(c * CHUNK, CHUNK), CHUNK)
        z = z_ref[sl, :].astype(F32)
        y = log1m - _softplus(-z)
        mx = jnp.maximum(log_lb, y)
        lf = mx + jnp.log1p(jnp.exp(-jnp.abs(log_lb - y)))
        key = (1.0 - lb) * _sigmoid(-z)
        qc = q_ref[sl, :].astype(F32)
        vc = i_ref[sl, :]
        vt = vc.astype(F32).T.astype(BF16)

        tot = _dot_exact_lhs(cs_mat, lf)
        b = tot[:CHUNK]
        d = b - tot[CHUNK:]
        b_last = b[CHUNK - 1:CHUNK, :]

        q_d = (qc * jnp.exp(d)).astype(BF16)
        k_d = (key * jnp.exp(-d)).astype(BF16)
        q_js, k_js = [], []
        for jb in range(nsub - 1):
            e_j = b[(jb + 1) * SUB - 1:(jb + 1) * SUB, :]
            q_j = jnp.where(row >= (jb + 1) * SUB, qc * jnp.exp(jnp.minimum(b - e_j, 0.0)), 0.0)
            in_j = jnp.logical_and(row >= jb * SUB, row < (jb + 1) * SUB)
            k_j = jnp.where(in_j, key * jnp.exp(jnp.minimum(e_j - b, 0.0)), 0.0)
            q_js.append(q_j.astype(BF16))
            k_js.append(k_j.astype(BF16))
        q_in = (qc * jnp.exp(b)).astype(BF16)
        k_out = (key * jnp.exp(b_last - b)).astype(BF16)
        decay = jnp.exp(b_last)

        s_diag = [_dot_nt(hcols(q_d, h), hcols(k_d, h)) for h in heads]
        s_off = [_dot_nt(jnp.concatenate([hcols(x, h) for x in q_js], axis=1),
                         jnp.concatenate([hcols(x, h) for x in k_js], axis=1)) for h in heads]
        scores = [(jnp.where(diag_ok, s_diag[h], 0.0) + s_off[h]).astype(BF16) for h in heads]
        o_intra = [_dot(scores[h], hcols(vc, h)) for h in heads]
        o_inter = [_dot_nt(hcols(q_in, h), states[h].astype(BF16)) for h in heads]
        upd = [_dot(vt[h * HG_DIM:(h + 1) * HG_DIM, :], hcols(k_out, h)) for h in heads]
        new_states = tuple(states[h] * hcols(decay, h) + upd[h] for h in heads)

        outs = []
        for h in heads:
            o = o_intra[h] + o_inter[h]
            ms = jnp.mean(o * o, axis=-1, keepdims=True)
            outs.append(o * lax.rsqrt(ms + NORM_EPS))
        g = g_ref[sl, :].astype(F32)
        o = jnp.concatenate(outs, axis=1) * nw_ref[...] * (g * _sigmoid(g))
        o_ref[sl, :] = o.astype(o_ref.dtype)
        return new_states

    init = tuple(jnp.zeros((HG_DIM, HG_DIM), F32) for _ in heads)
    lax.fori_loop(0, nchunks, chunk, init)


def _hgrn_call(u3, hg_lb, norm_w, layer):
    bsz, seq, _ = u3.shape
    nl = hg_lb.shape[0]

    def col(base):
        return lambda b: (b, 0, base // HEADW)

    blk = (None, seq, HEADW)
    return pl.pallas_call(
        functools.partial(_hgrn_body, layer=layer, nchunks=seq // CHUNK),
        grid=(bsz,),
        in_specs=[
            pl.BlockSpec((nl, HEADW), lambda b: (0, 0)),
            pl.BlockSpec(blk, col(COL_HF)),
            pl.BlockSpec(blk, col(COL_HI)),
            pl.BlockSpec(blk, col(COL_HQ)),
            pl.BlockSpec(blk, col(COL_HG)),
            pl.BlockSpec((1, HEADW), lambda b: (0, 0)),
        ],
        out_specs=pl.BlockSpec(blk, lambda b: (b, 0, 0)),
        out_shape=jax.ShapeDtypeStruct((bsz, seq, HEADW), BF16),
        compiler_params=_cp(("parallel",)),
        name="hgrn2",
    )(hg_lb, u3, u3, u3, u3, jnp.tile(norm_w, HG_HEADS).reshape(1, HEADW))


def _head_ones(n, width):
    r = _iota((n, n), 0) // width
    c = _iota((n, n), 1) // width
    return (r == c).astype(BF16)


def _rw_prep_body(*refs, has_vres):
    (r_ref, k_ref, v_ref, l_ref, rp_ref, kp_ref, vp_ref, lp_ref,
     mu_ref, w0_ref, w2_ref, a0_ref, a2_ref, g2_ref, kk_ref, ka_ref) = refs[:16]
    if has_vres:
        vf_ref, v0_ref, v1_ref, v2_ref = refs[16:20]
    r_o, k_o, v_o, lw_o, kkn_o, bk_o, g_o = refs[-7:]
    j = pl.program_id(1)
    ts = r_ref.shape[0]
    row = _iota((ts, 1), 0)
    first = (j > 0).astype(F32)

    def shifted(cur_ref, prev_ref, lo, hi):
        u = cur_ref[...].astype(F32)
        prev_row = prev_ref[SUB - 1:SUB, :].astype(F32) * first
        u_prev = jnp.where(row == 0, prev_row, pltpu.roll(u, 1, 0))
        return u + (u_prev - u) * mu_ref[:, lo:hi]

    r = shifted(r_ref, rp_ref, 0, 512)
    k = shifted(k_ref, kp_ref, 512, 1024)
    v = shifted(v_ref, vp_ref, 1024, 1536)
    lora = shifted(l_ref, lp_ref, 1536, 1792)
    w_lo = lora[:, 0:64]
    a_lo = lora[:, 64:128]
    g_lo = lora[:, 128:256]

    def lora_dot(act, w_ref):
        return _dot(act.astype(BF16), w_ref[...].astype(BF16))

    wpre = w0_ref[...] + lora_dot(jnp.tanh(w_lo), w2_ref)
    w = -_softplus(-wpre) - 0.5
    lw = -jnp.exp(w)
    a = _sigmoid(a0_ref[...] + lora_dot(a_lo, a2_ref))
    g = lora_dot(_sigmoid(g_lo), g2_ref)
    if has_vres:
        mix = _sigmoid(v0_ref[...] + lora_dot(lora_dot(v, v1_ref), v2_ref))
        v = v + (vf_ref[...] - v) * mix

    kk = k * kk_ref[...]
    ssq = _head_sum(kk * kk)
    kkn = kk * lax.rsqrt(jnp.maximum(ssq, 1e-24))
    k = k * (1.0 + (a - 1.0) * ka_ref[...])

    r_o[...] = r
    k_o[...] = k
    v_o[...] = v
    lw_o[...] = lw
    kkn_o[...] = kkn
    bk_o[...] = kkn * a
    g_o[...] = g


def _rw_prep_call(u3, mu, w0, w2, a0, a2, g2, k_k, k_a, vres):
    bsz, seq, _ = u3.shape
    ts = min(512, seq)
    w = RW_WIDTH
    has_vres = vres is not None

    def full(arr):
        return pl.BlockSpec(arr.shape, lambda b, j: (0,) * arr.ndim)

    def cur(col, width):
        return pl.BlockSpec((None, ts, width), lambda b, j: (b, j, col // width))

    def prev(col, width):
        return pl.BlockSpec((None, SUB, width),
                            lambda b, j: (b, jnp.maximum(j * (ts // SUB) - 1, 0), col // width))

    params = [mu.reshape(1, -1), w0.reshape(1, w), w2, a0.reshape(1, w), a2, g2,
              k_k.reshape(1, w), k_a.reshape(1, w)]
    in_specs = [cur(COL_RR, w), cur(COL_RK, w), cur(COL_RV, w), cur(COL_RL, RW_LORA_W),
                prev(COL_RR, w), prev(COL_RK, w), prev(COL_RV, w), prev(COL_RL, RW_LORA_W)]
    in_specs += [full(p) for p in params]
    args = [u3] * 8 + params
    if has_vres:
        v_first, v0, v1, v2 = vres
        extra = [v0.reshape(1, w), v1, v2]
        in_specs += [pl.BlockSpec((None, ts, w), lambda b, j: (b, j, 0))] + [full(p) for p in extra]
        args += [v_first] + extra
    out_spec = pl.BlockSpec((None, ts, w), lambda b, j: (b, j, 0))
    return pl.pallas_call(
        functools.partial(_rw_prep_body, has_vres=has_vres),
        grid=(bsz, seq // ts),
        in_specs=in_specs,
        out_specs=[out_spec] * 7,
        out_shape=[jax.ShapeDtypeStruct((bsz, seq, w), F32)] * 7,
        compiler_params=_cp(("parallel", "parallel")),
        name="rwkv_prep",
    )(*args)


def _rw_scan_body(r_ref, k_ref, v_ref, lw_ref, kk_ref, bk_ref, g_ref, rk_ref, gw_ref, gb_ref,
                  o_ref, state_ref, oacc_ref, *, nsub):
    @pl.when(pl.program_id(1) == 0)
    def _():
        state_ref[...] = jnp.zeros_like(state_ref)

    t = CHUNK
    gw = RW_GW
    tc = nsub * t
    units = [(s, grp) for s in range(nsub) for grp in range(RW_GROUPS)]

    rr = _iota((tc, tc), 0)
    cc = _iota((tc, tc), 1)
    tril_bd = jnp.logical_and(cc <= rr, (cc // t) == (rr // t)).astype(BF16)
    rr4 = _iota((t, gw), 0)
    cj = jnp.bitwise_and(_iota((t, gw), 1), t - 1)
    strict = cj < rr4
    incl = cj <= rr4
    eye_c = (cj == rr4).astype(F32)
    bd_mask = (_iota((gw, gw), 0) // t) == (_iota((gw, gw), 1) // RW_DIM)
    zero_b = jnp.zeros((), BF16)

    def bd(xc):
        xb = xc.astype(BF16)
        return jnp.where(bd_mask, jnp.concatenate([xb] * RW_PACK, axis=0), zero_b)

    def cut(x, u):
        s, grp = u
        return x[s * t:(s + 1) * t, grp * gw:(grp + 1) * gw]

    r = r_ref[...]
    k = k_ref[...]
    v = v_ref[...]
    lw = lw_ref[...]
    kk = kk_ref[...]
    bk = bk_ref[...]
    c = _dot_exact_lhs(tril_bd, lw)
    e_neg = jnp.exp(-c)
    a_t = -kk * jnp.exp(c - lw)
    b_t = bk * e_neg
    k_t = k * e_neg
    r_t = r * jnp.exp(c)
    c_last = jnp.concatenate(
        [jnp.broadcast_to(c[(s + 1) * t - 1:(s + 1) * t, :], (t, RW_WIDTH)) for s in range(nsub)], axis=0)
    decay_out = jnp.exp(c_last - c)
    b_g = bk * decay_out
    k_g = k * decay_out
    gamma = jnp.exp(c_last)

    ar = {u: jnp.concatenate([cut(a_t, u), cut(r_t, u)], axis=0).astype(BF16) for u in units}
    x_bk = {u: _dot_nt(ar[u], jnp.concatenate([bd(cut(b_t, u)), bd(cut(k_t, u))], axis=0)) for u in units}
    l_c = {u: jnp.where(strict, x_bk[u][:t, :gw], 0.0) for u in units}
    m_c = {u: jnp.where(strict, x_bk[u][:t, gw:], 0.0).astype(BF16) for u in units}
    rb = {u: jnp.where(incl, x_bk[u][t:, :gw], 0.0).astype(BF16) for u in units}
    rkm = {u: jnp.where(incl, x_bk[u][t:, gw:], 0.0).astype(BF16) for u in units}
    v_bd = {u: bd(cut(v, u)) for u in units}

    p = {u: _dot(l_c[u].astype(BF16), bd(l_c[u])) for u in units}
    s_inv = {u: eye_c + l_c[u] for u in units}
    mv = {u: _dot(m_c[u], v_bd[u]) for u in units}
    o_kv = {u: _dot(rkm[u], v_bd[u]) for u in units}
    for lvl in range(1, 6):
        last = lvl == 5
        nxt_p, nxt_s = {}, {}
        for u in units:
            pb = p[u].astype(BF16)
            if last:
                nxt_s[u] = s_inv[u] + _dot(pb, bd(s_inv[u]))
            else:
                res = _dot(pb, jnp.concatenate([bd(p[u]), bd(s_inv[u])], axis=1))
                nxt_p[u] = res[:, :gw]
                nxt_s[u] = s_inv[u] + res[:, gw:]
        p, s_inv = nxt_p, nxt_s

    wu = {u: _dot(s_inv[u].astype(BF16), jnp.concatenate([bd(cut(a_t, u)), bd(mv[u])], axis=1)) for u in units}
    w_t = {u: wu[u][:, :gw] for u in units}
    u_t = {u: wu[u][:, gw:] for u in units}
    rw = {u: _dot(rb[u], jnp.concatenate([bd(w_t[u]), bd(u_t[u])], axis=1)) for u in units}
    r_hat = {u: (cut(r_t, u) + rw[u][:, :gw]).astype(BF16) for u in units}
    o_hat = {u: rw[u][:, gw:] + o_kv[u] for u in units}
    g_mat = {u: jnp.where(bd_mask, _dot(w_t[u].T.astype(BF16), cut(b_g, u).astype(BF16)), 0.0).astype(BF16)
             for u in units}
    c_mat = {}
    for u in units:
        uv_t = jnp.concatenate([u_t[u], cut(v, u)], axis=0).T.astype(BF16)
        bkg = jnp.concatenate([cut(b_g, u), cut(k_g, u)], axis=0).astype(BF16)
        c_mat[u] = jnp.where(bd_mask, _dot(uv_t, bkg), 0.0)

    for s in range(nsub):
        for grp in range(RW_GROUPS):
            u = (s, grp)
            st = state_ref[grp]
            st_b = st.astype(BF16)
            oacc_ref[s * t:(s + 1) * t, grp * gw:(grp + 1) * gw] = _dot_nt(r_hat[u], st_b) + o_hat[u]
            g_row = gamma[(s + 1) * t - 1:(s + 1) * t, grp * gw:(grp + 1) * gw]
            state_ref[grp] = st * g_row + _dot(st_b, g_mat[u]) + c_mat[u]

    o = oacc_ref[...]
    mean = _head_sum(o) * (1.0 / RW_DIM)
    dlt = o - mean
    var = _head_sum(dlt * dlt) * (1.0 / RW_DIM)
    on = dlt * lax.rsqrt(var + RW_GN_EPS) * gw_ref[...] + gb_ref[...]
    bonus = _head_sum(r * k * rk_ref[...])
    o_ref[...] = ((on + bonus * v) * g_ref[...]).astype(o_ref.dtype)


def _rw_scan_call(r, k, v, lw, kkn, bk, g, r_k, gn_w, gn_b):
    bsz, seq, w = r.shape
    tc = min(256, seq)
    spec = pl.BlockSpec((None, tc, w), lambda b, j: (b, j, 0))
    pspec = pl.BlockSpec((1, w), lambda b, j: (0, 0))
    return pl.pallas_call(
        functools.partial(_rw_scan_body, nsub=tc // CHUNK),
        grid=(bsz, seq // tc),
        in_specs=[spec] * 7 + [pspec] * 3,
        out_specs=spec,
        out_shape=jax.ShapeDtypeStruct((bsz, seq, w), BF16),
        scratch_shapes=[pltpu.VMEM((RW_GROUPS, RW_GW, RW_GW), F32), pltpu.VMEM((tc, w), F32)],
        compiler_params=_cp(("parallel", "arbitrary")),
        name="rwkv_scan",
    )(r, k, v, lw, kkn, bk, g, r_k.reshape(1, w), gn_w.reshape(1, w), gn_b.reshape(1, w))


def _merge_body(ga_ref, gb_ref, gc_ref, oa_ref, ob_ref, oc_ref, x_ref, gate_ref,
                wa_ref, wb_ref, wc_ref, wo_ref, o_ref):
    merged = (_sigmoid(ga_ref[...].astype(F32)) * _dot(oa_ref[...], wa_ref[...])
              + _sigmoid(gb_ref[...].astype(F32)) * _dot(ob_ref[...], wb_ref[...])
              + _sigmoid(gc_ref[...].astype(F32)) * _dot(oc_ref[...], wc_ref[...]))
    mix = _dot(merged.astype(BF16), wo_ref[...])
    o_ref[...] = x_ref[...] + gate_ref[...] * mix


def _merge_call(u2, o_a, o_b, o_c, x2, ada3, wa, wb, wc, wo, layer, bsz, seq):
    m, d = x2.shape
    tm = min(512, seq)
    per_b = seq // tm
    base = layer * bsz * 6
    wdt = o_a.shape[1]

    def const(arr):
        return pl.BlockSpec(arr.shape, lambda i: (0, 0))

    return pl.pallas_call(
        _merge_body,
        grid=(m // tm,),
        in_specs=[
            pl.BlockSpec((tm, d), lambda i: (i, 0)),
            pl.BlockSpec((tm, d), lambda i: (i, 1)),
            pl.BlockSpec((tm, d), lambda i: (i, 2)),
            pl.BlockSpec((tm, wdt), lambda i: (i, 0)),
            pl.BlockSpec((tm, wdt), lambda i: (i, 0)),
            pl.BlockSpec((tm, wdt), lambda i: (i, 0)),
            pl.BlockSpec((tm, d), lambda i: (i, 0)),
            pl.BlockSpec((None, 1, d), lambda i: (base + (i // per_b) * 6 + 2, 0, 0)),
            const(wa), const(wb), const(wc), const(wo),
        ],
        out_specs=pl.BlockSpec((tm, d), lambda i: (i, 0)),
        out_shape=jax.ShapeDtypeStruct((m, d), F32),
        compiler_params=_cp(("parallel",)),
        name="merge_out",
    )(u2, u2, u2, o_a, o_b, o_c, x2, ada3, wa, wb, wc, wo)


def _ffn_body(x_ref, nw_ref, shift_ref, scale_ref, gate_ref, wi_ref, wo_ref, fn_ref, o_ref, *, final):
    x = x_ref[...]
    h = _norm_mod(x, nw_ref[...], scale_ref[...], shift_ref[...]).astype(BF16)
    half = FFN_HIDDEN // 2
    acc = jnp.zeros(x.shape, F32)
    for c in range(2):
        gh = _dot(h, wi_ref[:, c * half:(c + 1) * half])
        uh = _dot(h, wi_ref[:, FFN_HIDDEN + c * half:FFN_HIDDEN + (c + 1) * half])
        act = (gh * _sigmoid(gh) * uh).astype(BF16)
        acc = acc + _dot(act, wo_ref[c * half:(c + 1) * half, :])
    y = x + gate_ref[...] * acc
    if final:
        ms = jnp.mean(y * y, axis=-1, keepdims=True)
        y = y * lax.rsqrt(ms + NORM_EPS) * fn_ref[...]
    o_ref[...] = y


def _ffn_call(x2, nw, ada3, wi, wo, final_w, layer, bsz, seq, final):
    m, d = x2.shape
    tm = min(512, seq)
    per_b = seq // tm
    base = layer * bsz * 6

    def ada_spec(k):
        return pl.BlockSpec((None, 1, d), lambda i: (base + (i // per_b) * 6 + k, 0, 0))

    def const(arr):
        return pl.BlockSpec(arr.shape, lambda i: (0, 0), pipeline_mode=pl.Buffered(1))

    return pl.pallas_call(
        functools.partial(_ffn_body, final=final),
        grid=(m // tm,),
        in_specs=[
            pl.BlockSpec((tm, d), lambda i: (i, 0)),
            pl.BlockSpec((1, d), lambda i: (0, 0)),
            ada_spec(3), ada_spec(4), ada_spec(5),
            const(wi), const(wo),
            pl.BlockSpec((1, d), lambda i: (0, 0)),
        ],
        out_specs=pl.BlockSpec((tm, d), lambda i: (i, 0)),
        out_shape=jax.ShapeDtypeStruct((m, d), F32),
        compiler_params=_cp(("parallel",)),
        name="ffn_final" if final else "ffn",
    )(x2, nw, ada3, ada3, ada3, wi, wo, final_w)


def _permute_w_in(w):
    def qk_lanes(t):
        t = t.reshape(t.shape[0], DA_HEADS, 2, 2, DA_QK_DIM // 2)
        return jnp.swapaxes(t, 2, 3).reshape(t.shape[0], DA_HEADS * 2 * DA_QK_DIM)

    q, k, v = qk_lanes(w[:, 0:512]), qk_lanes(w[:, 512:1024]), w[:, 1024:1536]
    hgrn = w[:, 1536:3584]
    rw = w[:, 3584:5376]
    gates = w[:, 5376:8448]
    return jnp.concatenate([gates, hgrn, q, k, v, rw], axis=1)


def kernel(x, c, positions, ada_w, ada_b, norm_mix_w, norm_ffn_w, w_in, da_lambda, da_subln_w, hg_lb, hg_norm_w, rw_mu, rw_w0, rw_w2, rw_a0, rw_a2, rw_g2, rw_k_k, rw_k_a, rw_r_k, rw_gn_w, rw_gn_b, rw_v0, rw_v1, rw_v2, w_branch_a, w_branch_b, w_branch_c, w_out, ffn_w_in, ffn_w_out, final_norm_w):
    bsz, seq, d = x.shape
    depth = ada_w.shape[0]
    m = bsz * seq

    ada = _ada_call(c, ada_w, ada_b)
    ada3 = ada.reshape(depth * bsz * 6, 1, d)
    cos, sin_s = _rope_call(positions)

    x2 = x.reshape(m, d)
    v_first = None
    for l in range(depth):
        w_l = _permute_w_in(w_in[l]).astype(BF16)
        u2 = _proj_in_call(x2, norm_mix_w[l].reshape(1, d), ada3, cos, sin_s, w_l, l, bsz, seq)
        u3 = u2.reshape(bsz, seq, IN_COLS)

        o_a = _attn_call(u3, da_lambda[l], da_subln_w[l], l)
        o_b = _hgrn_call(u3, hg_lb, hg_norm_w[l], l)
        vres = None if l == 0 else (v_first, rw_v0[l - 1], rw_v1[l - 1], rw_v2[l - 1])
        r, k, v, lw, kkn, bk, g = _rw_prep_call(u3, rw_mu[l], rw_w0[l], rw_w2[l], rw_a0[l], rw_a2[l],
                                                rw_g2[l], rw_k_k[l], rw_k_a[l], vres)
        if l == 0:
            v_first = v
        o_c = _rw_scan_call(r, k, v, lw, kkn, bk, g, rw_r_k[l], rw_gn_w[l], rw_gn_b[l])

        x2 = _merge_call(u2, o_a.reshape(m, -1), o_b.reshape(m, -1), o_c.reshape(m, -1), x2, ada3,
                         w_branch_a[l].astype(BF16), w_branch_b[l].astype(BF16),
                         w_branch_c[l].astype(BF16), w_out[l].astype(BF16), l, bsz, seq)
        x2 = _ffn_call(x2, norm_ffn_w[l].reshape(1, d), ada3, ffn_w_in[l].astype(BF16),
                       ffn_w_out[l].astype(BF16), final_norm_w.reshape(1, d), l, bsz, seq,
                       final=(l == depth - 1))
    return x2.reshape(bsz, seq, d)
```

```python
import functools
import math

import jax
import jax.numpy as jnp
from jax import lax
from jax.experimental import pallas as pl
from jax.experimental.pallas import tpu as pltpu

F32 = jnp.float32
BF16 = jnp.bfloat16

D_MODEL = 1024
CHUNK = 64
ROPE_THETA = 10000.0
NORM_EPS = 1e-6

DA_HEADS = 4
DA_QK_DIM = 64
DA_V_DIM = 128
HG_HEADS = 4
HG_DIM = 128
RW_HEADS = 8
RW_DIM = 64
RW_WIDTH = 512
RW_GN_EPS = 64e-5
FFN_HIDDEN = 2816
IN_COLS = 8448

LANES = 128
ATTN_LOCKSTEP = 4
LOG2E = 1.4426950408889634
Q_SCALE = DA_QK_DIM ** -0.5 * LOG2E
VT_ROWS = DA_V_DIM + 16
SUB = 16
RW_PACK = 4
RW_GW = RW_PACK * RW_DIM
RW_GROUPS = RW_HEADS // RW_PACK

COL_GATES = 0
COL_HF = 3072
COL_HI = 3584
COL_HQ = 4096
COL_HG = 4608
COL_Q = 5120
COL_K = 5632
COL_V = 6144
COL_RR = 6656
COL_RK = 7168
COL_RV = 7680
COL_RL = 8192
RW_LORA_W = 256
HEADW = 512
PROJ_TN = 768
GROUPS_PER_TILE = PROJ_TN // LANES
ROPE_Q0 = COL_Q // LANES
ROPE_K0 = COL_K // LANES

VMEM_LIMIT = 56 * 1024 * 1024


def _cp(sem):
    return pltpu.CompilerParams(dimension_semantics=sem, vmem_limit_bytes=VMEM_LIMIT)


def _sigmoid(x):
    return 1.0 / (1.0 + jnp.exp(-x))


def _softplus(x):
    return jnp.maximum(x, 0.0) + jnp.log1p(jnp.exp(-jnp.abs(x)))


def _split3(x):
    hi = x.astype(BF16)
    r1 = x - hi.astype(F32)
    mid = r1.astype(BF16)
    lo = (r1 - mid.astype(F32)).astype(BF16)
    return hi, mid, lo


def _dot(a, b):
    return jnp.dot(a, b, preferred_element_type=F32)


def _dot_nt(a, b):
    return lax.dot_general(a, b, (((1,), (1,)), ((), ())), preferred_element_type=F32)


def _dot_exact_lhs(a01, x):
    n = x.shape[1]
    res = _dot(a01, jnp.concatenate(_split3(x), axis=1))
    return res[:, 0:n] + res[:, n:2 * n] + res[:, 2 * n:3 * n]


def _head_sum(x):
    ones_g = _head_ones(RW_GW, RW_DIM)
    hi = x.astype(BF16)
    lo = (x - hi.astype(F32)).astype(BF16)
    m = x.shape[0]
    outs = []
    for grp in range(x.shape[1] // RW_GW):
        cols = slice(grp * RW_GW, (grp + 1) * RW_GW)
        res = _dot(jnp.concatenate([hi[:, cols], lo[:, cols]], axis=0), ones_g)
        outs.append(res[:m] + res[m:])
    return jnp.concatenate(outs, axis=1)


def _dot_hp(a, b):
    a_hi = a.astype(BF16)
    a_lo = (a - a_hi.astype(F32)).astype(BF16)
    b_hi = b.astype(BF16)
    b_lo = (b - b_hi.astype(F32)).astype(BF16)
    return _dot(a_hi, b_hi) + _dot(a_hi, b_lo) + _dot(a_lo, b_hi)


def _iota(shape, dim):
    return lax.broadcasted_iota(jnp.int32, shape, dim)


def _ada_body(c_ref, w_ref, b_ref, o_ref):
    c = c_ref[...]
    ca = c * _sigmoid(c)
    o_ref[...] = _dot_hp(ca, w_ref[...]) + b_ref[...]


def _ada_call(c, ada_w, ada_b):
    nl, d, n6 = ada_w.shape
    bsz = c.shape[0]
    tn = 1536
    return pl.pallas_call(
        _ada_body,
        grid=(nl, n6 // tn),
        in_specs=[
            pl.BlockSpec((bsz, d), lambda l, j: (0, 0)),
            pl.BlockSpec((None, d, tn), lambda l, j: (l, 0, j)),
            pl.BlockSpec((None, 1, tn), lambda l, j: (l, 0, j)),
        ],
        out_specs=pl.BlockSpec((None, bsz, tn), lambda l, j: (l, 0, j)),
        out_shape=jax.ShapeDtypeStruct((nl, bsz, n6), F32),
        compiler_params=_cp(("parallel", "parallel")),
        name="ada",
    )(c, ada_w, ada_b.reshape(nl, 1, n6))


def _rope_body(pos_ref, invf_ref, sgn_ref, cos_ref, sin_ref):
    ang = pos_ref[...].astype(F32) * invf_ref[...]
    cos_ref[...] = jnp.cos(ang)
    sin_ref[...] = jnp.sin(ang) * sgn_ref[...]


def _rope_call(positions):
    m = positions.size
    tm = min(2048, m)
    inv_freq = ROPE_THETA ** (-jnp.arange(0, DA_QK_DIM, 2, dtype=F32) / DA_QK_DIM)
    invf = jnp.tile(inv_freq, LANES // (DA_QK_DIM // 2)).reshape(1, LANES)
    sgn = jnp.where(jnp.arange(LANES) < LANES // 2, -1.0, 1.0).astype(F32).reshape(1, LANES)
    return pl.pallas_call(
        _rope_body,
        grid=(m // tm,),
        in_specs=[
            pl.BlockSpec((tm, 1), lambda i: (i, 0)),
            pl.BlockSpec((1, LANES), lambda i: (0, 0)),
            pl.BlockSpec((1, LANES), lambda i: (0, 0)),
        ],
        out_specs=[pl.BlockSpec((tm, LANES), lambda i: (i, 0))] * 2,
        out_shape=[jax.ShapeDtypeStruct((m, LANES), F32)] * 2,
        compiler_params=_cp(("parallel",)),
        name="rope_tables",
    )(positions.reshape(m, 1), invf, sgn)


def _norm_mod(x, nw, scale, shift):
    ms = jnp.mean(x * x, axis=-1, keepdims=True)
    y = x * lax.rsqrt(ms + NORM_EPS) * nw
    return y * (1.0 + scale) + shift


def _rope_kind(group):
    if ROPE_Q0 <= group < ROPE_Q0 + DA_HEADS:
        return "q"
    if ROPE_K0 <= group < ROPE_K0 + DA_HEADS:
        return "k"
    return None


_ROPE_TILES = sorted({g // GROUPS_PER_TILE for g in range(IN_COLS // LANES) if _rope_kind(g)})


def _proj_in_body(x_ref, nw_ref, shift_ref, scale_ref, cos_ref, sin_ref, w_ref, o_ref, h_ref):
    j = pl.program_id(1)

    @pl.when(j == 0)
    def _():
        h_ref[...] = _norm_mod(x_ref[...], nw_ref[...], scale_ref[...], shift_ref[...]).astype(BF16)

    for jt in _ROPE_TILES:
        @pl.when(j == jt)
        def _(jt=jt):
            acc = _dot(h_ref[...], w_ref[...])
            cos = cos_ref[...]
            sin_s = sin_ref[...]
            for g in range(GROUPS_PER_TILE):
                slab = acc[:, g * LANES:(g + 1) * LANES]
                kind = _rope_kind(jt * GROUPS_PER_TILE + g)
                if kind is not None:
                    slab = slab * cos + pltpu.roll(slab, LANES // 2, 1) * sin_s
                    if kind == "q":
                        slab = slab * Q_SCALE
                o_ref[:, g * LANES:(g + 1) * LANES] = slab.astype(o_ref.dtype)

    plain = j != _ROPE_TILES[0]
    for jt in _ROPE_TILES[1:]:
        plain = jnp.logical_and(plain, j != jt)

    @pl.when(plain)
    def _():
        o_ref[...] = _dot(h_ref[...], w_ref[...]).astype(o_ref.dtype)


def _proj_in_call(x2, nw, ada3, cos, sin_s, w_bf16, layer, bsz, seq):
    m, d = x2.shape
    n = w_bf16.shape[1]
    tm = min(2048, seq)
    per_b = seq // tm
    base = layer * bsz * 6

    def ada_spec(k):
        return pl.BlockSpec((None, 1, d), lambda i, j: (base + (i // per_b) * 6 + k, 0, 0))

    return pl.pallas_call(
        _proj_in_body,
        grid=(m // tm, n // PROJ_TN),
        in_specs=[
            pl.BlockSpec((tm, d), lambda i, j: (i, 0)),
            pl.BlockSpec((1, d), lambda i, j: (0, 0)),
            ada_spec(0),
            ada_spec(1),
            pl.BlockSpec((tm, LANES), lambda i, j: (i, 0)),
            pl.BlockSpec((tm, LANES), lambda i, j: (i, 0)),
            pl.BlockSpec((d, PROJ_TN), lambda i, j: (0, j)),
        ],
        out_specs=pl.BlockSpec((tm, PROJ_TN), lambda i, j: (i, j)),
        out_shape=jax.ShapeDtypeStruct((m, n), BF16),
        scratch_shapes=[pltpu.VMEM((tm, d), BF16)],
        compiler_params=_cp(("parallel", "arbitrary")),
        name="proj_in",
    )(x2, nw, ada3, ada3, cos, sin_s, w_bf16)


def _attn_body(lam_ref, q_ref, k_ref, v_ref, sw_ref, o_ref, vt_ref, *, tq, lam_init):
    i = pl.program_id(1)

    @pl.when(i == 0)
    def _():
        for h in range(DA_HEADS):
            vt_ref[h, 0:DA_V_DIM, :] = v_ref[:, h * LANES:(h + 1) * LANES].astype(F32).T.astype(BF16)
            vt_ref[h, DA_V_DIM:VT_ROWS, :] = jnp.ones((VT_ROWS - DA_V_DIM, vt_ref.shape[2]), BF16)

    lv = lam_ref[...]
    lam = (jnp.exp(jnp.sum(lv[0:1] * lv[1:2], axis=1, keepdims=True))
           - jnp.exp(jnp.sum(lv[2:3] * lv[3:4], axis=1, keepdims=True)) + lam_init)

    map0 = jnp.bitwise_and(_iota((1, LANES), 1), DA_QK_DIM - 1) < DA_QK_DIM // 2
    zero = jnp.zeros((), BF16)
    qs = []
    for h in range(DA_HEADS):
        q = q_ref[:, h * LANES:(h + 1) * LANES]
        q0 = jnp.where(map0, q, zero)
        q1 = jnp.where(map0, zero, q)
        qs.append(jnp.concatenate([q0, q1], axis=0))

    def mask_for(nk):
        kk = _iota((nk, 2 * tq), 0)
        qq = _iota((nk, 2 * tq), 1)
        q_chunk = jnp.right_shift(jnp.bitwise_and(qq, tq - 1), 6) + (nk - tq) // CHUNK
        return jnp.right_shift(kk, 6) <= q_chunk

    def step(start, nk, carry, allowed):
        out = [None] * DA_HEADS
        for g0 in range(0, DA_HEADS, ATTN_LOCKSTEP):
            grp = range(g0, g0 + ATTN_LOCKSTEP)
            ss = {h: _dot_nt(k_ref[pl.ds(start, nk), h * LANES:(h + 1) * LANES], qs[h]) for h in grp}
            ps, stats = {}, {}
            for h in grp:
                m_i, _ = carry[h]
                s = ss[h]
                if allowed is not None:
                    s = jnp.where(allowed, s, -jnp.inf)
                m_new = jnp.maximum(m_i, jnp.max(s, axis=0, keepdims=True))
                ps[h] = jnp.exp2(s - m_new).astype(BF16)
                stats[h] = (m_new, jnp.exp2(m_i - m_new))
            for h in grp:
                m_new, alpha = stats[h]
                pv = _dot(vt_ref[h, :, pl.ds(start, nk)], ps[h])
                out[h] = (m_new, alpha * carry[h][1] + pv)
        return tuple(out)

    init = tuple((jnp.full((1, 2 * tq), -jnp.inf, F32), jnp.zeros((VT_ROWS, 2 * tq), F32))
                 for _ in range(DA_HEADS))
    carry = lax.fori_loop(
        0, i // 2, lambda j, c: step(pl.multiple_of(j * (2 * tq), 2 * tq), 2 * tq, c, None), init)
    carry = lax.cond(
        i % 2 == 1,
        lambda c: step(pl.multiple_of((i - 1) * tq, tq), 2 * tq, c, mask_for(2 * tq)),
        lambda c: step(pl.multiple_of(i * tq, tq), tq, c, mask_for(tq)),
        carry)

    for h in range(DA_HEADS):
        acc = carry[h][1]
        o_t = acc[:DA_V_DIM] / acc[DA_V_DIM:DA_V_DIM + 1]
        o_t = o_t[:, :tq] - lam * o_t[:, tq:]
        ms = jnp.mean(o_t * o_t, axis=0, keepdims=True)
        o = (o_t * lax.rsqrt(ms + NORM_EPS)).T * sw_ref[...] * (1.0 - lam_init)
        o_ref[:, h * LANES:(h + 1) * LANES] = o.astype(o_ref.dtype)


def _attn_call(u3, lam_vecs, subln_w, layer):
    bsz, seq, _ = u3.shape
    tq = min(256, seq)
    lam_init = 0.8 - 0.6 * math.exp(-0.3 * layer)
    return pl.pallas_call(
        functools.partial(_attn_body, tq=tq, lam_init=lam_init),
        grid=(bsz, seq // tq),
        in_specs=[
            pl.BlockSpec((4, DA_QK_DIM), lambda b, i: (0, 0)),
            pl.BlockSpec((None, tq, HEADW), lambda b, i: (b, i, COL_Q // HEADW)),
            pl.BlockSpec((None, seq, HEADW), lambda b, i: (b, 0, COL_K // HEADW)),
            pl.BlockSpec((None, seq, HEADW), lambda b, i: (b, 0, COL_V // HEADW)),
            pl.BlockSpec((1, LANES), lambda b, i: (0, 0)),
        ],
        out_specs=pl.BlockSpec((None, tq, HEADW), lambda b, i: (b, i, 0)),
        out_shape=jax.ShapeDtypeStruct((bsz, seq, HEADW), BF16),
        scratch_shapes=[pltpu.VMEM((DA_HEADS, VT_ROWS, seq), BF16)],
        compiler_params=_cp(("parallel", "arbitrary")),
        name="diff_attn",
    )(lam_vecs, u3, u3, u3, subln_w.reshape(1, LANES))


def _hgrn_body(lbp_ref, z_ref, i_ref, q_ref, g_ref, nw_ref, o_ref, *, layer, nchunks):
    lbp = lbp_ref[...]
    e = jnp.exp(lbp - jnp.max(lbp, axis=0, keepdims=True))
    sm = e / jnp.sum(e, axis=0, keepdims=True)
    cs = sm[0:1]
    for t in range(1, layer + 1):
        cs = cs + sm[t:t + 1]
    lb = cs - sm[0:1]
    log_lb = jnp.log(lb)
    log1m = jnp.log1p(-lb)

    rr = _iota((CHUNK, CHUNK), 0)
    cc = _iota((CHUNK, CHUNK), 1)
    sub_start = jnp.bitwise_and(rr, ~(SUB - 1))
    cs_mat = jnp.concatenate([(cc <= rr).astype(BF16), (cc < sub_start).astype(BF16)], axis=0)
    diag_ok = jnp.logical_and(cc <= rr, jnp.right_shift(cc, 4) == jnp.right_shift(rr, 4))
    row = _iota((CHUNK, 1), 0)
    nsub = CHUNK // SUB
    heads = range(HG_HEADS)

    def hcols(x, h):
        return x[:, h * HG_DIM:(h + 1) * HG_DIM]

    def chunk(c, states):
        sl = pl.ds(pl.multiple_of(c * CHUNK, CHUNK), CHUNK)
        z = z_ref[sl, :].astype(F32)
        y = log1m - _softplus(-z)
        mx = jnp.maximum(log_lb, y)
        lf = mx + jnp.log1p(jnp.exp(-jnp.abs(log_lb - y)))
        key = (1.0 - lb) * _sigmoid(-z)
        qc = q_ref[sl, :].astype(F32)
        vc = i_ref[sl, :]
        vt = vc.astype(F32).T.astype(BF16)

        tot = _dot_exact_lhs(cs_mat, lf)
        b = tot[:CHUNK]
        d = b - tot[CHUNK:]
        b_last = b[CHUNK - 1:CHUNK, :]

        q_d = (qc * jnp.exp(d)).astype(BF16)
        k_d = (key * jnp.exp(-d)).astype(BF16)
        q_js, k_js = [], []
        for jb in range(nsub - 1):
            e_j = b[(jb + 1) * SUB - 1:(jb + 1) * SUB, :]
            q_j = jnp.where(row >= (jb + 1) * SUB, qc * jnp.exp(jnp.minimum(b - e_j, 0.0)), 0.0)
            in_j = jnp.logical_and(row >= jb * SUB, row < (jb + 1) * SUB)
            k_j = jnp.where(in_j, key * jnp.exp(jnp.minimum(e_j - b, 0.0)), 0.0)
            q_js.append(q_j.astype(BF16))
            k_js.append(k_j.astype(BF16))
        q_in = (qc * jnp.exp(b)).astype(BF16)
        k_out = (key * jnp.exp(b_last - b)).astype(BF16)
        decay = jnp.exp(b_last)

        s_diag = [_dot_nt(hcols(q_d, h), hcols(k_d, h)) for h in heads]
        s_off = [_dot_nt(jnp.concatenate([hcols(x, h) for x in q_js], axis=1),
                         jnp.concatenate([hcols(x, h) for x in k_js], axis=1)) for h in heads]
        scores = [(jnp.where(diag_ok, s_diag[h], 0.0) + s_off[h]).astype(BF16) for h in heads]
        o_intra = [_dot(scores[h], hcols(vc, h)) for h in heads]
        o_inter = [_dot_nt(hcols(q_in, h), states[h].astype(BF16)) for h in heads]
        upd = [_dot(vt[h * HG_DIM:(h + 1) * HG_DIM, :], hcols(k_out, h)) for h in heads]
        new_states = tuple(states[h] * hcols(decay, h) + upd[h] for h in heads)

        outs = []
        for h in heads:
            o = o_intra[h] + o_inter[h]
            ms = jnp.mean(o * o, axis=-1, keepdims=True)
            outs.append(o * lax.rsqrt(ms + NORM_EPS))
        g = g_ref[sl, :].astype(F32)
        o = jnp.concatenate(outs, axis=1) * nw_ref[...] * (g * _sigmoid(g))
        o_ref[sl, :] = o.astype(o_ref.dtype)
        return new_states

    init = tuple(jnp.zeros((HG_DIM, HG_DIM), F32) for _ in heads)
    lax.fori_loop(0, nchunks, chunk, init, unroll=2)


def _hgrn_call(u3, hg_lb, norm_w, layer):
    bsz, seq, _ = u3.shape
    nl = hg_lb.shape[0]

    def col(base):
        return lambda b: (b, 0, base // HEADW)

    blk = (None, seq, HEADW)
    return pl.pallas_call(
        functools.partial(_hgrn_body, layer=layer, nchunks=seq // CHUNK),
        grid=(bsz,),
        in_specs=[
            pl.BlockSpec((nl, HEADW), lambda b: (0, 0)),
            pl.BlockSpec(blk, col(COL_HF)),
            pl.BlockSpec(blk, col(COL_HI)),
            pl.BlockSpec(blk, col(COL_HQ)),
            pl.BlockSpec(blk, col(COL_HG)),
            pl.BlockSpec((1, HEADW), lambda b: (0, 0)),
        ],
        out_specs=pl.BlockSpec(blk, lambda b: (b, 0, 0)),
        out_shape=jax.ShapeDtypeStruct((bsz, seq, HEADW), BF16),
        compiler_params=_cp(("parallel",)),
        name="hgrn2",
    )(hg_lb, u3, u3, u3, u3, jnp.tile(norm_w, HG_HEADS).reshape(1, HEADW))


def _head_ones(n, width):
    r = _iota((n, n), 0) // width
    c = _iota((n, n), 1) // width
    return (r == c).astype(BF16)


def _rw_prep_body(*refs, has_vres):
    (r_ref, k_ref, v_ref, l_ref, rp_ref, kp_ref, vp_ref, lp_ref,
     mu_ref, w0_ref, w2_ref, a0_ref, a2_ref, g2_ref, kk_ref, ka_ref) = refs[:16]
    if has_vres:
        vf_ref, v0_ref, v1_ref, v2_ref = refs[16:20]
    r_o, k_o, v_o, lw_o, kkn_o, bk_o, g_o = refs[-7:]
    j = pl.program_id(1)
    ts = r_ref.shape[0]
    row = _iota((ts, 1), 0)
    first = (j > 0).astype(F32)

    def shifted(cur_ref, prev_ref, lo, hi):
        u = cur_ref[...].astype(F32)
        prev_row = prev_ref[SUB - 1:SUB, :].astype(F32) * first
        u_prev = jnp.where(row == 0, prev_row, pltpu.roll(u, 1, 0))
        return u + (u_prev - u) * mu_ref[:, lo:hi]

    r = shifted(r_ref, rp_ref, 0, 512)
    k = shifted(k_ref, kp_ref, 512, 1024)
    v = shifted(v_ref, vp_ref, 1024, 1536)
    lora = shifted(l_ref, lp_ref, 1536, 1792)
    w_lo = lora[:, 0:64]
    a_lo = lora[:, 64:128]
    g_lo = lora[:, 128:256]

    def lora_dot(act, w_ref):
        return _dot(act.astype(BF16), w_ref[...].astype(BF16))

    wpre = w0_ref[...] + lora_dot(jnp.tanh(w_lo), w2_ref)
    w = -_softplus(-wpre) - 0.5
    lw = -jnp.exp(w)
    a = _sigmoid(a0_ref[...] + lora_dot(a_lo, a2_ref))
    g = lora_dot(_sigmoid(g_lo), g2_ref)
    if has_vres:
        mix = _sigmoid(v0_ref[...] + lora_dot(lora_dot(v, v1_ref), v2_ref))
        v = v + (vf_ref[...] - v) * mix

    kk = k * kk_ref[...]
    ssq = _head_sum(kk * kk)
    kkn = kk * lax.rsqrt(jnp.maximum(ssq, 1e-24))
    k = k * (1.0 + (a - 1.0) * ka_ref[...])

    r_o[...] = r
    k_o[...] = k
    v_o[...] = v
    lw_o[...] = lw
    kkn_o[...] = kkn
    bk_o[...] = kkn * a
    g_o[...] = g


def _rw_prep_call(u3, mu, w0, w2, a0, a2, g2, k_k, k_a, vres):
    bsz, seq, _ = u3.shape
    ts = min(512, seq)
    w = RW_WIDTH
    has_vres = vres is not None

    def full(arr):
        return pl.BlockSpec(arr.shape, lambda b, j: (0,) * arr.ndim)

    def cur(col, width):
        return pl.BlockSpec((None, ts, width), lambda b, j: (b, j, col // width))

    def prev(col, width):
        return pl.BlockSpec((None, SUB, width),
                            lambda b, j: (b, jnp.maximum(j * (ts // SUB) - 1, 0), col // width))

    params = [mu.reshape(1, -1), w0.reshape(1, w), w2, a0.reshape(1, w), a2, g2,
              k_k.reshape(1, w), k_a.reshape(1, w)]
    in_specs = [cur(COL_RR, w), cur(COL_RK, w), cur(COL_RV, w), cur(COL_RL, RW_LORA_W),
                prev(COL_RR, w), prev(COL_RK, w), prev(COL_RV, w), prev(COL_RL, RW_LORA_W)]
    in_specs += [full(p) for p in params]
    args = [u3] * 8 + params
    if has_vres:
        v_first, v0, v1, v2 = vres
        extra = [v0.reshape(1, w), v1, v2]
        in_specs += [pl.BlockSpec((None, ts, w), lambda b, j: (b, j, 0))] + [full(p) for p in extra]
        args += [v_first] + extra
    out_spec = pl.BlockSpec((None, ts, w), lambda b, j: (b, j, 0))
    return pl.pallas_call(
        functools.partial(_rw_prep_body, has_vres=has_vres),
        grid=(bsz, seq // ts),
        in_specs=in_specs,
        out_specs=[out_spec] * 7,
        out_shape=[jax.ShapeDtypeStruct((bsz, seq, w), F32)] * 7,
        compiler_params=_cp(("parallel", "parallel")),
        name="rwkv_prep",
    )(*args)


def _rw_scan_body(r_ref, k_ref, v_ref, lw_ref, kk_ref, bk_ref, g_ref, rk_ref, gw_ref, gb_ref,
                  o_ref, state_ref, oacc_ref, *, nsub):
    @pl.when(pl.program_id(1) == 0)
    def _():
        state_ref[...] = jnp.zeros_like(state_ref)

    t = CHUNK
    gw = RW_GW
    tc = nsub * t
    units = [(s, grp) for s in range(nsub) for grp in range(RW_GROUPS)]

    rr = _iota((tc, tc), 0)
    cc = _iota((tc, tc), 1)
    tril_bd = jnp.logical_and(cc <= rr, (cc // t) == (rr // t)).astype(BF16)
    rr4 = _iota((t, gw), 0)
    cj = jnp.bitwise_and(_iota((t, gw), 1), t - 1)
    strict = cj < rr4
    incl = cj <= rr4
    eye_c = (cj == rr4).astype(F32)
    bd_mask = (_iota((gw, gw), 0) // t) == (_iota((gw, gw), 1) // RW_DIM)
    zero_b = jnp.zeros((), BF16)

    def bd(xc):
        xb = xc.astype(BF16)
        return jnp.where(bd_mask, jnp.concatenate([xb] * RW_PACK, axis=0), zero_b)

    def cut(x, u):
        s, grp = u
        return x[s * t:(s + 1) * t, grp * gw:(grp + 1) * gw]

    r = r_ref[...]
    k = k_ref[...]
    v = v_ref[...]
    lw = lw_ref[...]
    kk = kk_ref[...]
    bk = bk_ref[...]
    c = _dot_exact_lhs(tril_bd, lw)
    e_neg = jnp.exp(-c)
    a_t = -kk * jnp.exp(c - lw)
    b_t = bk * e_neg
    k_t = k * e_neg
    r_t = r * jnp.exp(c)
    c_last = jnp.concatenate(
        [jnp.broadcast_to(c[(s + 1) * t - 1:(s + 1) * t, :], (t, RW_WIDTH)) for s in range(nsub)], axis=0)
    decay_out = jnp.exp(c_last - c)
    b_g = bk * decay_out
    k_g = k * decay_out
    gamma = jnp.exp(c_last)

    ar = {u: jnp.concatenate([cut(a_t, u), cut(r_t, u)], axis=0).astype(BF16) for u in units}
    x_bk = {u: _dot_nt(ar[u], jnp.concatenate([bd(cut(b_t, u)), bd(cut(k_t, u))], axis=0)) for u in units}
    l_c = {u: jnp.where(strict, x_bk[u][:t, :gw], 0.0) for u in units}
    m_c = {u: jnp.where(strict, x_bk[u][:t, gw:], 0.0).astype(BF16) for u in units}
    rb = {u: jnp.where(incl, x_bk[u][t:, :gw], 0.0).astype(BF16) for u in units}
    rkm = {u: jnp.where(incl, x_bk[u][t:, gw:], 0.0).astype(BF16) for u in units}
    v_bd = {u: bd(cut(v, u)) for u in units}

    p = {u: _dot(l_c[u].astype(BF16), bd(l_c[u])) for u in units}
    s_inv = {u: eye_c + l_c[u] for u in units}
    mv = {u: _dot(m_c[u], v_bd[u]) for u in units}
    o_kv = {u: _dot(rkm[u], v_bd[u]) for u in units}
    for lvl in range(1, 6):
        last = lvl == 5
        nxt_p, nxt_s = {}, {}
        for u in units:
            pb = p[u].astype(BF16)
            if last:
                nxt_s[u] = s_inv[u] + _dot(pb, bd(s_inv[u]))
            else:
                res = _dot(pb, jnp.concatenate([bd(p[u]), bd(s_inv[u])], axis=1))
                nxt_p[u] = res[:, :gw]
                nxt_s[u] = s_inv[u] + res[:, gw:]
        p, s_inv = nxt_p, nxt_s

    wu = {u: _dot(s_inv[u].astype(BF16), jnp.concatenate([bd(cut(a_t, u)), bd(mv[u])], axis=1)) for u in units}
    w_t = {u: wu[u][:, :gw] for u in units}
    u_t = {u: wu[u][:, gw:] for u in units}
    rw = {u: _dot(rb[u], jnp.concatenate([bd(w_t[u]), bd(u_t[u])], axis=1)) for u in units}
    r_hat = {u: (cut(r_t, u) + rw[u][:, :gw]).astype(BF16) for u in units}
    o_hat = {u: rw[u][:, gw:] + o_kv[u] for u in units}
    g_mat = {u: jnp.where(bd_mask, _dot(w_t[u].T.astype(BF16), cut(b_g, u).astype(BF16)), 0.0).astype(BF16)
             for u in units}
    c_mat = {}
    for u in units:
        uv_t = jnp.concatenate([u_t[u], cut(v, u)], axis=0).T.astype(BF16)
        bkg = jnp.concatenate([cut(b_g, u), cut(k_g, u)], axis=0).astype(BF16)
        c_mat[u] = jnp.where(bd_mask, _dot(uv_t, bkg), 0.0)

    for s in range(nsub):
        for grp in range(RW_GROUPS):
            u = (s, grp)
            st = state_ref[grp]
            st_b = st.astype(BF16)
            oacc_ref[s * t:(s + 1) * t, grp * gw:(grp + 1) * gw] = _dot_nt(r_hat[u], st_b) + o_hat[u]
            g_row = gamma[(s + 1) * t - 1:(s + 1) * t, grp * gw:(grp + 1) * gw]
            state_ref[grp] = st * g_row + _dot(st_b, g_mat[u]) + c_mat[u]

    o = oacc_ref[...]
    mean = _head_sum(o) * (1.0 / RW_DIM)
    dlt = o - mean
    var = _head_sum(dlt * dlt) * (1.0 / RW_DIM)
    on = dlt * lax.rsqrt(var + RW_GN_EPS) * gw_ref[...] + gb_ref[...]
    bonus = _head_sum(r * k * rk_ref[...])
    o_ref[...] = ((on + bonus * v) * g_ref[...]).astype(o_ref.dtype)


def _rw_scan_call(r, k, v, lw, kkn, bk, g, r_k, gn_w, gn_b):
    bsz, seq, w = r.shape
    tc = min(256, seq)
    spec = pl.BlockSpec((None, tc, w), lambda b, j: (b, j, 0))
    pspec = pl.BlockSpec((1, w), lambda b, j: (0, 0))
    return pl.pallas_call(
        functools.partial(_rw_scan_body, nsub=tc // CHUNK),
        grid=(bsz, seq // tc),
        in_specs=[spec] * 7 + [pspec] * 3,
        out_specs=spec,
        out_shape=jax.ShapeDtypeStruct((bsz, seq, w), BF16),
        scratch_shapes=[pltpu.VMEM((RW_GROUPS, RW_GW, RW_GW), F32), pltpu.VMEM((tc, w), F32)],
        compiler_params=_cp(("parallel", "arbitrary")),
        name="rwkv_scan",
    )(r, k, v, lw, kkn, bk, g, r_k.reshape(1, w), gn_w.reshape(1, w), gn_b.reshape(1, w))


def _merge_body(ga_ref, gb_ref, gc_ref, oa_ref, ob_ref, oc_ref, x_ref, gate_ref,
                wa_ref, wb_ref, wc_ref, wo_ref, o_ref):
    merged = (_sigmoid(ga_ref[...].astype(F32)) * _dot(oa_ref[...], wa_ref[...])
              + _sigmoid(gb_ref[...].astype(F32)) * _dot(ob_ref[...], wb_ref[...])
              + _sigmoid(gc_ref[...].astype(F32)) * _dot(oc_ref[...], wc_ref[...]))
    mix = _dot(merged.astype(BF16), wo_ref[...])
    o_ref[...] = x_ref[...] + gate_ref[...] * mix


def _merge_call(u2, o_a, o_b, o_c, x2, ada3, wa, wb, wc, wo, layer, bsz, seq):
    m, d = x2.shape
    tm = min(512, seq)
    per_b = seq // tm
    base = layer * bsz * 6
    wdt = o_a.shape[1]

    def const(arr):
        return pl.BlockSpec(arr.shape, lambda i: (0, 0))

    return pl.pallas_call(
        _merge_body,
        grid=(m // tm,),
        in_specs=[
            pl.BlockSpec((tm, d), lambda i: (i, 0)),
            pl.BlockSpec((tm, d), lambda i: (i, 1)),
            pl.BlockSpec((tm, d), lambda i: (i, 2)),
            pl.BlockSpec((tm, wdt), lambda i: (i, 0)),
            pl.BlockSpec((tm, wdt), lambda i: (i, 0)),
            pl.BlockSpec((tm, wdt), lambda i: (i, 0)),
            pl.BlockSpec((tm, d), lambda i: (i, 0)),
            pl.BlockSpec((None, 1, d), lambda i: (base + (i // per_b) * 6 + 2, 0, 0)),
            const(wa), const(wb), const(wc), const(wo),
        ],
        out_specs=pl.BlockSpec((tm, d), lambda i: (i, 0)),
        out_shape=jax.ShapeDtypeStruct((m, d), F32),
        compiler_params=_cp(("parallel",)),
        name="merge_out",
    )(u2, u2, u2, o_a, o_b, o_c, x2, ada3, wa, wb, wc, wo)


def _ffn_body(x_ref, nw_ref, shift_ref, scale_ref, gate_ref, wi_ref, wo_ref, fn_ref, o_ref, *, final):
    x = x_ref[...]
    h = _norm_mod(x, nw_ref[...], scale_ref[...], shift_ref[...]).astype(BF16)
    half = FFN_HIDDEN // 2
    acc = jnp.zeros(x.shape, F32)
    for c in range(2):
        gh = _dot(h, wi_ref[:, c * half:(c + 1) * half])
        uh = _dot(h, wi_ref[:, FFN_HIDDEN + c * half:FFN_HIDDEN + (c + 1) * half])
        act = (gh * _sigmoid(gh) * uh).astype(BF16)
        acc = acc + _dot(act, wo_ref[c * half:(c + 1) * half, :])
    y = x + gate_ref[...] * acc
    if final:
        ms = jnp.mean(y * y, axis=-1, keepdims=True)
        y = y * lax.rsqrt(ms + NORM_EPS) * fn_ref[...]
    o_ref[...] = y


def _ffn_call(x2, nw, ada3, wi, wo, final_w, layer, bsz, seq, final):
    m, d = x2.shape
    tm = min(512, seq)
    per_b = seq // tm
    base = layer * bsz * 6

    def ada_spec(k):
        return pl.BlockSpec((None, 1, d), lambda i: (base + (i // per_b) * 6 + k, 0, 0))

    def const(arr):
        return pl.BlockSpec(arr.shape, lambda i: (0, 0), pipeline_mode=pl.Buffered(1))

    return pl.pallas_call(
        functools.partial(_ffn_body, final=final),
        grid=(m // tm,),
        in_specs=[
            pl.BlockSpec((tm, d), lambda i: (i, 0)),
            pl.BlockSpec((1, d), lambda i: (0, 0)),
            ada_spec(3), ada_spec(4), ada_spec(5),
            const(wi), const(wo),
            pl.BlockSpec((1, d), lambda i: (0, 0)),
        ],
        out_specs=pl.BlockSpec((tm, d), lambda i: (i, 0)),
        out_shape=jax.ShapeDtypeStruct((m, d), F32),
        compiler_params=_cp(("parallel",)),
        name="ffn_final" if final else "ffn",
    )(x2, nw, ada3, ada3, ada3, wi, wo, final_w)


def _permute_w_in(w):
    def qk_lanes(t):
        t = t.reshape(t.shape[0], DA_HEADS, 2, 2, DA_QK_DIM // 2)
        return jnp.swapaxes(t, 2, 3).reshape(t.shape[0], DA_HEADS * 2 * DA_QK_DIM)

    q, k, v = qk_lanes(w[:, 0:512]), qk_lanes(w[:, 512:1024]), w[:, 1024:1536]
    hgrn = w[:, 1536:3584]
    rw = w[:, 3584:5376]
    gates = w[:, 5376:8448]
    return jnp.concatenate([gates, hgrn, q, k, v, rw], axis=1)


def kernel(x, c, positions, ada_w, ada_b, norm_mix_w, norm_ffn_w, w_in, da_lambda, da_subln_w, hg_lb, hg_norm_w, rw_mu, rw_w0, rw_w2, rw_a0, rw_a2, rw_g2, rw_k_k, rw_k_a, rw_r_k, rw_gn_w, rw_gn_b, rw_v0, rw_v1, rw_v2, w_branch_a, w_branch_b, w_branch_c, w_out, ffn_w_in, ffn_w_out, final_norm_w):
    bsz, seq, d = x.shape
    depth = ada_w.shape[0]
    m = bsz * seq

    ada = _ada_call(c, ada_w, ada_b)
    ada3 = ada.reshape(depth * bsz * 6, 1, d)
    cos, sin_s = _rope_call(positions)

    x2 = x.reshape(m, d)
    v_first = None
    for l in range(depth):
        w_l = _permute_w_in(w_in[l]).astype(BF16)
        u2 = _proj_in_call(x2, norm_mix_w[l].reshape(1, d), ada3, cos, sin_s, w_l, l, bsz, seq)
        u3 = u2.reshape(bsz, seq, IN_COLS)

        o_a = _attn_call(u3, da_lambda[l], da_subln_w[l], l)
        o_b = _hgrn_call(u3, hg_lb, hg_norm_w[l], l)
        vres = None if l == 0 else (v_first, rw_v0[l - 1], rw_v1[l - 1], rw_v2[l - 1])
        r, k, v, lw, kkn, bk, g = _rw_prep_call(u3, rw_mu[l], rw_w0[l], rw_w2[l], rw_a0[l], rw_a2[l],
                                                rw_g2[l], rw_k_k[l], rw_k_a[l], vres)
        if l == 0:
            v_first = v
        o_c = _rw_scan_call(r, k, v, lw, kkn, bk, g, rw_r_k[l], rw_gn_w[l], rw_gn_b[l])

        x2 = _merge_call(u2, o_a.reshape(m, -1), o_b.reshape(m, -1), o_c.reshape(m, -1), x2, ada3,
                         w_branch_a[l].astype(BF16), w_branch_b[l].astype(BF16),
                         w_branch_c[l].astype(BF16), w_out[l].astype(BF16), l, bsz, seq)
        x2 = _ffn_call(x2, norm_ffn_w[l].reshape(1, d), ada3, ffn_w_in[l].astype(BF16),
                       ffn_w_out[l].astype(BF16), final_norm_w.reshape(1, d), l, bsz, seq,
                       final=(l == depth - 1))
    return x2.reshape(bsz, seq, d)
```

```python
import functools
import math

import jax
import jax.numpy as jnp
from jax import lax
from jax.experimental import pallas as pl
from jax.experimental.pallas import tpu as pltpu

F32 = jnp.float32
BF16 = jnp.bfloat16

D_MODEL = 1024
CHUNK = 64
ROPE_THETA = 10000.0
NORM_EPS = 1e-6

DA_HEADS = 4
DA_QK_DIM = 64
DA_V_DIM = 128
HG_HEADS = 4
HG_DIM = 128
RW_HEADS = 8
RW_DIM = 64
RW_WIDTH = 512
RW_GN_EPS = 64e-5
FFN_HIDDEN = 2816
IN_COLS = 8448

LANES = 128
ATTN_LOCKSTEP = 4
LOG2E = 1.4426950408889634
Q_SCALE = DA_QK_DIM ** -0.5 * LOG2E
VT_ROWS = DA_V_DIM + 16
SUB = 16
RW_PACK = 4
RW_GW = RW_PACK * RW_DIM
RW_GROUPS = RW_HEADS // RW_PACK

COL_GATES = 0
COL_HF = 3072
COL_HI = 3584
COL_HQ = 4096
COL_HG = 4608
COL_Q = 5120
COL_K = 5632
COL_V = 6144
COL_RR = 6656
COL_RK = 7168
COL_RV = 7680
COL_RL = 8192
RW_LORA_W = 256
HEADW = 512
PROJ_TN = 768
GROUPS_PER_TILE = PROJ_TN // LANES
ROPE_Q0 = COL_Q // LANES
ROPE_K0 = COL_K // LANES

VMEM_LIMIT = 56 * 1024 * 1024


def _cp(sem):
    return pltpu.CompilerParams(dimension_semantics=sem, vmem_limit_bytes=VMEM_LIMIT)


def _sigmoid(x):
    return 1.0 / (1.0 + jnp.exp(-x))


def _softplus(x):
    return jnp.maximum(x, 0.0) + jnp.log(1.0 + jnp.exp(-jnp.abs(x)))


def _dot(a, b):
    return jnp.dot(a, b, preferred_element_type=F32)


def _dot_nt(a, b):
    return lax.dot_general(a, b, (((1,), (1,)), ((), ())), preferred_element_type=F32)


def _dot_exact_lhs(a01, x):
    n = x.shape[1]
    hi = x.astype(BF16)
    lo = (x - hi.astype(F32)).astype(BF16)
    res = _dot(a01, jnp.concatenate([hi, lo], axis=1))
    return res[:, 0:n] + res[:, n:2 * n]


def _head_sum(x):
    ones_g = _head_ones(RW_GW, RW_DIM)
    hi = x.astype(BF16)
    lo = (x - hi.astype(F32)).astype(BF16)
    m = x.shape[0]
    outs = []
    for grp in range(x.shape[1] // RW_GW):
        cols = slice(grp * RW_GW, (grp + 1) * RW_GW)
        res = _dot(jnp.concatenate([hi[:, cols], lo[:, cols]], axis=0), ones_g)
        outs.append(res[:m] + res[m:])
    return jnp.concatenate(outs, axis=1)


def _dot_hp(a, b):
    a_hi = a.astype(BF16)
    a_lo = (a - a_hi.astype(F32)).astype(BF16)
    b_hi = b.astype(BF16)
    b_lo = (b - b_hi.astype(F32)).astype(BF16)
    return _dot(a_hi, b_hi) + _dot(a_hi, b_lo) + _dot(a_lo, b_hi)


def _iota(shape, dim):
    return lax.broadcasted_iota(jnp.int32, shape, dim)


def _ada_body(c_ref, w_ref, b_ref, o_ref):
    c = c_ref[...]
    ca = c * _sigmoid(c)
    o_ref[...] = _dot_hp(ca, w_ref[...]) + b_ref[...]


def _ada_call(c, ada_w, ada_b):
    nl, d, n6 = ada_w.shape
    bsz = c.shape[0]
    tn = 1536
    return pl.pallas_call(
        _ada_body,
        grid=(nl, n6 // tn),
        in_specs=[
            pl.BlockSpec((bsz, d), lambda l, j: (0, 0)),
            pl.BlockSpec((None, d, tn), lambda l, j: (l, 0, j)),
            pl.BlockSpec((None, 1, tn), lambda l, j: (l, 0, j)),
        ],
        out_specs=pl.BlockSpec((None, bsz, tn), lambda l, j: (l, 0, j)),
        out_shape=jax.ShapeDtypeStruct((nl, bsz, n6), F32),
        compiler_params=_cp(("parallel", "parallel")),
        name="ada",
    )(c, ada_w, ada_b.reshape(nl, 1, n6))


def _rope_body(pos_ref, invf_ref, sgn_ref, cos_ref, sin_ref):
    ang = pos_ref[...].astype(F32) * invf_ref[...]
    cos_ref[...] = jnp.cos(ang)
    sin_ref[...] = jnp.sin(ang) * sgn_ref[...]


def _rope_call(positions):
    m = positions.size
    tm = min(2048, m)
    inv_freq = ROPE_THETA ** (-jnp.arange(0, DA_QK_DIM, 2, dtype=F32) / DA_QK_DIM)
    invf = jnp.tile(inv_freq, LANES // (DA_QK_DIM // 2)).reshape(1, LANES)
    sgn = jnp.where(jnp.arange(LANES) < LANES // 2, -1.0, 1.0).astype(F32).reshape(1, LANES)
    return pl.pallas_call(
        _rope_body,
        grid=(m // tm,),
        in_specs=[
            pl.BlockSpec((tm, 1), lambda i: (i, 0)),
            pl.BlockSpec((1, LANES), lambda i: (0, 0)),
            pl.BlockSpec((1, LANES), lambda i: (0, 0)),
        ],
        out_specs=[pl.BlockSpec((tm, LANES), lambda i: (i, 0))] * 2,
        out_shape=[jax.ShapeDtypeStruct((m, LANES), F32)] * 2,
        compiler_params=_cp(("parallel",)),
        name="rope_tables",
    )(positions.reshape(m, 1), invf, sgn)


def _norm_mod(x, nw, scale, shift):
    ms = jnp.mean(x * x, axis=-1, keepdims=True)
    y = x * lax.rsqrt(ms + NORM_EPS) * nw
    return y * (1.0 + scale) + shift


def _rope_kind(group):
    if ROPE_Q0 <= group < ROPE_Q0 + DA_HEADS:
        return "q"
    if ROPE_K0 <= group < ROPE_K0 + DA_HEADS:
        return "k"
    return None


_ROPE_TILES = sorted({g // GROUPS_PER_TILE for g in range(IN_COLS // LANES) if _rope_kind(g)})


def _proj_in_body(x_ref, nw_ref, shift_ref, scale_ref, cos_ref, sin_ref, w_ref, o_ref, h_ref):
    j = pl.program_id(1)

    @pl.when(j == 0)
    def _():
        h_ref[...] = _norm_mod(x_ref[...], nw_ref[...], scale_ref[...], shift_ref[...]).astype(BF16)

    for jt in _ROPE_TILES:
        @pl.when(j == jt)
        def _(jt=jt):
            acc = _dot(h_ref[...], w_ref[...])
            cos = cos_ref[...]
            sin_s = sin_ref[...]
            for g in range(GROUPS_PER_TILE):
                slab = acc[:, g * LANES:(g + 1) * LANES]
                kind = _rope_kind(jt * GROUPS_PER_TILE + g)
                if kind is not None:
                    slab = slab * cos + pltpu.roll(slab, LANES // 2, 1) * sin_s
                    if kind == "q":
                        slab = slab * Q_SCALE
                o_ref[:, g * LANES:(g + 1) * LANES] = slab.astype(o_ref.dtype)

    plain = j != _ROPE_TILES[0]
    for jt in _ROPE_TILES[1:]:
        plain = jnp.logical_and(plain, j != jt)

    @pl.when(plain)
    def _():
        o_ref[...] = _dot(h_ref[...], w_ref[...]).astype(o_ref.dtype)


def _proj_in_call(x2, nw, ada3, cos, sin_s, w_bf16, layer, bsz, seq):
    m, d = x2.shape
    n = w_bf16.shape[1]
    tm = min(2048, seq)
    per_b = seq // tm
    base = layer * bsz * 6

    def ada_spec(k):
        return pl.BlockSpec((None, 1, d), lambda i, j: (base + (i // per_b) * 6 + k, 0, 0))

    return pl.pallas_call(
        _proj_in_body,
        grid=(m // tm, n // PROJ_TN),
        in_specs=[
            pl.BlockSpec((tm, d), lambda i, j: (i, 0)),
            pl.BlockSpec((1, d), lambda i, j: (0, 0)),
            ada_spec(0),
            ada_spec(1),
            pl.BlockSpec((tm, LANES), lambda i, j: (i, 0)),
            pl.BlockSpec((tm, LANES), lambda i, j: (i, 0)),
            pl.BlockSpec((d, PROJ_TN), lambda i, j: (0, j)),
        ],
        out_specs=pl.BlockSpec((tm, PROJ_TN), lambda i, j: (i, j)),
        out_shape=jax.ShapeDtypeStruct((m, n), BF16),
        scratch_shapes=[pltpu.VMEM((tm, d), BF16)],
        compiler_params=_cp(("parallel", "arbitrary")),
        name="proj_in",
    )(x2, nw, ada3, ada3, cos, sin_s, w_bf16)


def _attn_body(lam_ref, q_ref, k_ref, v_ref, sw_ref, o_ref, vt_ref, *, tq, lam_init):
    i = pl.program_id(1)

    @pl.when(i == 0)
    def _():
        for h in range(DA_HEADS):
            vt_ref[h, 0:DA_V_DIM, :] = v_ref[:, h * LANES:(h + 1) * LANES].astype(F32).T.astype(BF16)
            vt_ref[h, DA_V_DIM:VT_ROWS, :] = jnp.ones((VT_ROWS - DA_V_DIM, vt_ref.shape[2]), BF16)

    lv = lam_ref[...]
    lam = (jnp.exp(jnp.sum(lv[0:1] * lv[1:2], axis=1, keepdims=True))
           - jnp.exp(jnp.sum(lv[2:3] * lv[3:4], axis=1, keepdims=True)) + lam_init)

    map0 = jnp.bitwise_and(_iota((1, LANES), 1), DA_QK_DIM - 1) < DA_QK_DIM // 2
    zero = jnp.zeros((), BF16)
    qs = []
    for h in range(DA_HEADS):
        q = q_ref[:, h * LANES:(h + 1) * LANES]
        q0 = jnp.where(map0, q, zero)
        q1 = jnp.where(map0, zero, q)
        qs.append(jnp.concatenate([q0, q1], axis=0))

    def mask_for(nk):
        kk = _iota((nk, 2 * tq), 0)
        qq = _iota((nk, 2 * tq), 1)
        q_chunk = jnp.right_shift(jnp.bitwise_and(qq, tq - 1), 6) + (nk - tq) // CHUNK
        return jnp.right_shift(kk, 6) <= q_chunk

    def step(start, nk, carry, allowed):
        out = [None] * DA_HEADS
        for g0 in range(0, DA_HEADS, ATTN_LOCKSTEP):
            grp = range(g0, g0 + ATTN_LOCKSTEP)
            ss = {h: _dot_nt(k_ref[pl.ds(start, nk), h * LANES:(h + 1) * LANES], qs[h]) for h in grp}
            ps, stats = {}, {}
            for h in grp:
                m_i, _ = carry[h]
                s = ss[h]
                if allowed is not None:
                    s = jnp.where(allowed, s, -jnp.inf)
                m_new = jnp.maximum(m_i, jnp.max(s, axis=0, keepdims=True))
                ps[h] = jnp.exp2(s - m_new).astype(BF16)
                stats[h] = (m_new, jnp.exp2(m_i - m_new))
            for h in grp:
                m_new, alpha = stats[h]
                pv = _dot(vt_ref[h, :, pl.ds(start, nk)], ps[h])
                out[h] = (m_new, alpha * carry[h][1] + pv)
        return tuple(out)

    init = tuple((jnp.full((1, 2 * tq), -jnp.inf, F32), jnp.zeros((VT_ROWS, 2 * tq), F32))
                 for _ in range(DA_HEADS))
    carry = lax.fori_loop(
        0, i // 2, lambda j, c: step(pl.multiple_of(j * (2 * tq), 2 * tq), 2 * tq, c, None), init)
    carry = lax.cond(
        i % 2 == 1,
        lambda c: step(pl.multiple_of((i - 1) * tq, tq), 2 * tq, c, mask_for(2 * tq)),
        lambda c: step(pl.multiple_of(i * tq, tq), tq, c, mask_for(tq)),
        carry)

    for h in range(DA_HEADS):
        acc = carry[h][1]
        o_t = acc[:DA_V_DIM] / acc[DA_V_DIM:DA_V_DIM + 1]
        o_t = o_t[:, :tq] - lam * o_t[:, tq:]
        ms = jnp.mean(o_t * o_t, axis=0, keepdims=True)
        o = (o_t * lax.rsqrt(ms + NORM_EPS)).T * sw_ref[...] * (1.0 - lam_init)
        o_ref[:, h * LANES:(h + 1) * LANES] = o.astype(o_ref.dtype)


def _attn_call(u3, lam_vecs, subln_w, layer):
    bsz, seq, _ = u3.shape
    tq = min(256, seq)
    lam_init = 0.8 - 0.6 * math.exp(-0.3 * layer)
    return pl.pallas_call(
        functools.partial(_attn_body, tq=tq, lam_init=lam_init),
        grid=(bsz, seq // tq),
        in_specs=[
            pl.BlockSpec((4, DA_QK_DIM), lambda b, i: (0, 0)),
            pl.BlockSpec((None, tq, HEADW), lambda b, i: (b, i, COL_Q // HEADW)),
            pl.BlockSpec((None, seq, HEADW), lambda b, i: (b, 0, COL_K // HEADW)),
            pl.BlockSpec((None, seq, HEADW), lambda b, i: (b, 0, COL_V // HEADW)),
            pl.BlockSpec((1, LANES), lambda b, i: (0, 0)),
        ],
        out_specs=pl.BlockSpec((None, tq, HEADW), lambda b, i: (b, i, 0)),
        out_shape=jax.ShapeDtypeStruct((bsz, seq, HEADW), BF16),
        scratch_shapes=[pltpu.VMEM((DA_HEADS, VT_ROWS, seq), BF16)],
        compiler_params=_cp(("parallel", "arbitrary")),
        name="diff_attn",
    )(lam_vecs, u3, u3, u3, subln_w.reshape(1, LANES))


def _hgrn_body(lbp_ref, z_ref, i_ref, q_ref, g_ref, nw_ref, o_ref, *, layer, nchunks):
    lbp = lbp_ref[...]
    e = jnp.exp(lbp - jnp.max(lbp, axis=0, keepdims=True))
    sm = e / jnp.sum(e, axis=0, keepdims=True)
    cs = sm[0:1]
    for t in range(1, layer + 1):
        cs = cs + sm[t:t + 1]
    lb = cs - sm[0:1]
    log_lb = jnp.log(lb)
    log1m = jnp.log1p(-lb)

    rr = _iota((CHUNK, CHUNK), 0)
    cc = _iota((CHUNK, CHUNK), 1)
    sub_start = jnp.bitwise_and(rr, ~(SUB - 1))
    cs_mat = jnp.concatenate([(cc <= rr).astype(BF16), (cc < sub_start).astype(BF16)], axis=0)
    diag_ok = jnp.logical_and(cc <= rr, jnp.right_shift(cc, 4) == jnp.right_shift(rr, 4))
    nsub = CHUNK // SUB
    heads = range(HG_HEADS)

    def hcols(x, h):
        return x[:, h * HG_DIM:(h + 1) * HG_DIM]

    def chunk(c, states):
        sl = pl.ds(pl.multiple_of(c * CHUNK, CHUNK), CHUNK)
        z = z_ref[sl, :].astype(F32)
        y = log1m - _softplus(-z)
        mx = jnp.maximum(log_lb, y)
        lf = mx + jnp.log(1.0 + jnp.exp(-jnp.abs(log_lb - y)))
        key = (1.0 - lb) * _sigmoid(-z)
        qc = q_ref[sl, :].astype(F32)
        vc = i_ref[sl, :]
        vt = vc.astype(F32).T.astype(BF16)

        tot = _dot_exact_lhs(cs_mat, lf * LOG2E)
        b = tot[:CHUNK]
        d = b - tot[CHUNK:]
        b_last = b[CHUNK - 1:CHUNK, :]

        q_d = (qc * jnp.exp2(d)).astype(BF16)
        k_d = (key * jnp.exp2(-d)).astype(BF16)
        q_js, k_js = [], []
        for jb in range(nsub - 1):
            lo, hi = jb * SUB, (jb + 1) * SUB
            e_j = b[hi - 1:hi, :]
            q_j = (qc[hi:] * jnp.exp2(b[hi:] - e_j)).astype(BF16)
            k_j = (key[lo:hi] * jnp.exp2(e_j - b[lo:hi])).astype(BF16)
            q_js.append(jnp.concatenate([jnp.zeros((hi, HEADW), BF16), q_j], axis=0))
            pieces = [k_j, jnp.zeros((CHUNK - hi, HEADW), BF16)]
            if lo:
                pieces.insert(0, jnp.zeros((lo, HEADW), BF16))
            k_js.append(jnp.concatenate(pieces, axis=0))
        q_in = (qc * jnp.exp2(b)).astype(BF16)
        k_out = (key * jnp.exp2(b_last - b)).astype(BF16)
        decay = jnp.exp2(b_last)

        s_diag = [_dot_nt(hcols(q_d, h), hcols(k_d, h)) for h in heads]
        s_off = [_dot_nt(jnp.concatenate([hcols(x, h) for x in q_js], axis=1),
                         jnp.concatenate([hcols(x, h) for x in k_js], axis=1)) for h in heads]
        scores = [(jnp.where(diag_ok, s_diag[h], 0.0) + s_off[h]).astype(BF16) for h in heads]
        o_intra = [_dot(scores[h], hcols(vc, h)) for h in heads]
        o_inter = [_dot_nt(hcols(q_in, h), states[h].astype(BF16)) for h in heads]
        upd = [_dot(vt[h * HG_DIM:(h + 1) * HG_DIM, :], hcols(k_out, h)) for h in heads]
        new_states = tuple(states[h] * hcols(decay, h) + upd[h] for h in heads)

        outs = []
        for h in heads:
            o = o_intra[h] + o_inter[h]
            ms = jnp.mean(o * o, axis=-1, keepdims=True)
            outs.append(o * lax.rsqrt(ms + NORM_EPS))
        g = g_ref[sl, :].astype(F32)
        o = jnp.concatenate(outs, axis=1) * nw_ref[...] * (g * _sigmoid(g))
        o_ref[sl, :] = o.astype(o_ref.dtype)
        return new_states

    init = tuple(jnp.zeros((HG_DIM, HG_DIM), F32) for _ in heads)
    lax.fori_loop(0, nchunks, chunk, init, unroll=4)


def _hgrn_call(u3, hg_lb, norm_w, layer):
    bsz, seq, _ = u3.shape
    nl = hg_lb.shape[0]

    def col(base):
        return lambda b: (b, 0, base // HEADW)

    blk = (None, seq, HEADW)
    return pl.pallas_call(
        functools.partial(_hgrn_body, layer=layer, nchunks=seq // CHUNK),
        grid=(bsz,),
        in_specs=[
            pl.BlockSpec((nl, HEADW), lambda b: (0, 0)),
            pl.BlockSpec(blk, col(COL_HF)),
            pl.BlockSpec(blk, col(COL_HI)),
            pl.BlockSpec(blk, col(COL_HQ)),
            pl.BlockSpec(blk, col(COL_HG)),
            pl.BlockSpec((1, HEADW), lambda b: (0, 0)),
        ],
        out_specs=pl.BlockSpec(blk, lambda b: (b, 0, 0)),
        out_shape=jax.ShapeDtypeStruct((bsz, seq, HEADW), BF16),
        compiler_params=_cp(("parallel",)),
        name="hgrn2",
    )(hg_lb, u3, u3, u3, u3, jnp.tile(norm_w, HG_HEADS).reshape(1, HEADW))


def _head_ones(n, width):
    r = _iota((n, n), 0) // width
    c = _iota((n, n), 1) // width
    return (r == c).astype(BF16)


def _rw_prep_body(*refs, has_vres):
    (r_ref, k_ref, v_ref, l_ref, rp_ref, kp_ref, vp_ref, lp_ref,
     mu_ref, w0_ref, w2_ref, a0_ref, a2_ref, g2_ref, kk_ref, ka_ref) = refs[:16]
    if has_vres:
        vf_ref, v0_ref, v1_ref, v2_ref = refs[16:20]
    r_o, k_o, v_o, lw_o, kkn_o, bk_o, g_o = refs[-7:]
    j = pl.program_id(1)
    ts = r_ref.shape[0]
    row = _iota((ts, 1), 0)
    first = (j > 0).astype(F32)

    def shifted(cur_ref, prev_ref, lo, hi):
        u = cur_ref[...].astype(F32)
        prev_row = prev_ref[SUB - 1:SUB, :].astype(F32) * first
        u_prev = jnp.where(row == 0, prev_row, pltpu.roll(u, 1, 0))
        return u + (u_prev - u) * mu_ref[:, lo:hi]

    r = shifted(r_ref, rp_ref, 0, 512)
    k = shifted(k_ref, kp_ref, 512, 1024)
    v = shifted(v_ref, vp_ref, 1024, 1536)
    lora = shifted(l_ref, lp_ref, 1536, 1792)
    w_lo = lora[:, 0:64]
    a_lo = lora[:, 64:128]
    g_lo = lora[:, 128:256]

    def lora_dot(act, w_ref):
        return _dot(act.astype(BF16), w_ref[...].astype(BF16))

    wpre = w0_ref[...] + lora_dot(jnp.tanh(w_lo), w2_ref)
    w = -_softplus(-wpre) - 0.5
    lw = -jnp.exp(w)
    a = _sigmoid(a0_ref[...] + lora_dot(a_lo, a2_ref))
    g = lora_dot(_sigmoid(g_lo), g2_ref)
    if has_vres:
        mix = _sigmoid(v0_ref[...] + lora_dot(lora_dot(v, v1_ref), v2_ref))
        v = v + (vf_ref[...].astype(F32) - v) * mix

    kk = k * kk_ref[...]
    ssq = _head_sum(kk * kk)
    kkn = kk * lax.rsqrt(jnp.maximum(ssq, 1e-24))
    k = k * (1.0 + (a - 1.0) * ka_ref[...])

    r_o[...] = r.astype(r_o.dtype)
    k_o[...] = k.astype(k_o.dtype)
    v_o[...] = v.astype(v_o.dtype)
    lw_o[...] = lw
    kkn_o[...] = kkn.astype(kkn_o.dtype)
    bk_o[...] = (kkn * a).astype(bk_o.dtype)
    g_o[...] = g.astype(g_o.dtype)


def _rw_prep_call(u3, mu, w0, w2, a0, a2, g2, k_k, k_a, vres):
    bsz, seq, _ = u3.shape
    ts = min(512, seq)
    w = RW_WIDTH
    has_vres = vres is not None

    def full(arr):
        return pl.BlockSpec(arr.shape, lambda b, j: (0,) * arr.ndim)

    def cur(col, width):
        return pl.BlockSpec((None, ts, width), lambda b, j: (b, j, col // width))

    def prev(col, width):
        return pl.BlockSpec((None, SUB, width),
                            lambda b, j: (b, jnp.maximum(j * (ts // SUB) - 1, 0), col // width))

    params = [mu.reshape(1, -1), w0.reshape(1, w), w2, a0.reshape(1, w), a2, g2,
              k_k.reshape(1, w), k_a.reshape(1, w)]
    in_specs = [cur(COL_RR, w), cur(COL_RK, w), cur(COL_RV, w), cur(COL_RL, RW_LORA_W),
                prev(COL_RR, w), prev(COL_RK, w), prev(COL_RV, w), prev(COL_RL, RW_LORA_W)]
    in_specs += [full(p) for p in params]
    args = [u3] * 8 + params
    if has_vres:
        v_first, v0, v1, v2 = vres
        extra = [v0.reshape(1, w), v1, v2]
        in_specs += [pl.BlockSpec((None, ts, w), lambda b, j: (b, j, 0))] + [full(p) for p in extra]
        args += [v_first] + extra
    out_spec = pl.BlockSpec((None, ts, w), lambda b, j: (b, j, 0))
    return pl.pallas_call(
        functools.partial(_rw_prep_body, has_vres=has_vres),
        grid=(bsz, seq // ts),
        in_specs=in_specs,
        out_specs=[out_spec] * 7,
        out_shape=[jax.ShapeDtypeStruct((bsz, seq, w), F32 if i == 3 else BF16) for i in range(7)],
        compiler_params=_cp(("parallel", "parallel")),
        name="rwkv_prep",
    )(*args)


def _rw_scan_body(r_ref, k_ref, v_ref, lw_ref, kk_ref, bk_ref, g_ref, rk_ref, gw_ref, gb_ref,
                  o_ref, state_ref, oacc_ref, *, nsub):
    @pl.when(pl.program_id(1) == 0)
    def _():
        state_ref[...] = jnp.zeros_like(state_ref)

    t = CHUNK
    gw = RW_GW
    tc = nsub * t
    units = [(s, grp) for s in range(nsub) for grp in range(RW_GROUPS)]

    rr = _iota((tc, tc), 0)
    cc = _iota((tc, tc), 1)
    tril_bd = jnp.logical_and(cc <= rr, (cc // t) == (rr // t)).astype(BF16)
    rr4 = _iota((t, gw), 0)
    cj = jnp.bitwise_and(_iota((t, gw), 1), t - 1)
    strict = cj < rr4
    incl = cj <= rr4
    eye_c = (cj == rr4).astype(F32)
    bd_mask = (_iota((gw, gw), 0) // t) == (_iota((gw, gw), 1) // RW_DIM)
    zero_b = jnp.zeros((), BF16)

    def bd(xc):
        xb = xc.astype(BF16)
        return jnp.where(bd_mask, jnp.concatenate([xb] * RW_PACK, axis=0), zero_b)

    def cut(x, u):
        s, grp = u
        return x[s * t:(s + 1) * t, grp * gw:(grp + 1) * gw]

    r = r_ref[...].astype(F32)
    k = k_ref[...].astype(F32)
    v = v_ref[...].astype(F32)
    lw = lw_ref[...] * LOG2E
    kk = kk_ref[...].astype(F32)
    bk = bk_ref[...].astype(F32)
    c = _dot_exact_lhs(tril_bd, lw)
    e_neg = jnp.exp2(-c)
    a_t = -kk * jnp.exp2(c - lw)
    b_t = bk * e_neg
    k_t = k * e_neg
    r_t = r * jnp.exp2(c)
    c_last = jnp.concatenate(
        [jnp.broadcast_to(c[(s + 1) * t - 1:(s + 1) * t, :], (t, RW_WIDTH)) for s in range(nsub)], axis=0)
    decay_out = jnp.exp2(c_last - c)
    b_g = bk * decay_out
    k_g = k * decay_out
    gamma = jnp.exp2(c_last)

    ar = {u: jnp.concatenate([cut(a_t, u), cut(r_t, u)], axis=0).astype(BF16) for u in units}
    x_bk = {u: _dot_nt(ar[u], jnp.concatenate([bd(cut(b_t, u)), bd(cut(k_t, u))], axis=0)) for u in units}
    l_c = {u: jnp.where(strict, x_bk[u][:t, :gw], 0.0) for u in units}
    m_c = {u: jnp.where(strict, x_bk[u][:t, gw:], 0.0).astype(BF16) for u in units}
    rb = {u: jnp.where(incl, x_bk[u][t:, :gw], 0.0).astype(BF16) for u in units}
    rkm = {u: jnp.where(incl, x_bk[u][t:, gw:], 0.0).astype(BF16) for u in units}
    v_bd = {u: bd(cut(v, u)) for u in units}

    p = {u: _dot(l_c[u].astype(BF16), bd(l_c[u])) for u in units}
    s_inv = {u: eye_c + l_c[u] for u in units}
    mo = {u: _dot(jnp.concatenate([m_c[u], rkm[u]], axis=0), v_bd[u]) for u in units}
    mv = {u: mo[u][:t] for u in units}
    o_kv = {u: mo[u][t:] for u in units}
    for lvl in range(1, 6):
        last = lvl == 5
        nxt_p, nxt_s = {}, {}
        for u in units:
            if last:
                nxt_s[u] = s_inv[u] + _dot(s_inv[u].astype(BF16), bd(p[u]))
            else:
                res = _dot(jnp.concatenate([s_inv[u], p[u]], axis=0).astype(BF16), bd(p[u]))
                nxt_s[u] = s_inv[u] + res[:t]
                nxt_p[u] = res[t:]
        p, s_inv = nxt_p, nxt_s

    rbs = {u: _dot(rb[u], bd(s_inv[u])) for u in units}
    wu = {u: _dot(jnp.concatenate([s_inv[u], rbs[u]], axis=0).astype(BF16),
                  jnp.concatenate([bd(cut(a_t, u)), bd(mv[u])], axis=1)) for u in units}
    w_t = {u: wu[u][:t, :gw] for u in units}
    u_t = {u: wu[u][:t, gw:] for u in units}
    r_hat = {u: (cut(r_t, u) + wu[u][t:, :gw]).astype(BF16) for u in units}
    o_hat = {u: wu[u][t:, gw:] + o_kv[u] for u in units}
    g_mat = {u: jnp.where(bd_mask, _dot(w_t[u].T.astype(BF16), cut(b_g, u).astype(BF16)), 0.0).astype(BF16)
             for u in units}
    c_mat = {}
    for u in units:
        uv_t = jnp.concatenate([u_t[u], cut(v, u)], axis=0).T.astype(BF16)
        bkg = jnp.concatenate([cut(b_g, u), cut(k_g, u)], axis=0).astype(BF16)
        full = jnp.where(bd_mask, _dot(uv_t, bkg), 0.0)
        c_mat[u] = full[0:t] + full[t:2 * t] + full[2 * t:3 * t] + full[3 * t:4 * t]

    for s in range(nsub):
        for grp in range(RW_GROUPS):
            u = (s, grp)
            st = state_ref[grp]
            st_b = st.astype(BF16)
            oacc_ref[s * t:(s + 1) * t, grp * gw:(grp + 1) * gw] = _dot_nt(r_hat[u], bd(st_b)) + o_hat[u]
            g_row = gamma[(s + 1) * t - 1:(s + 1) * t, grp * gw:(grp + 1) * gw]
            state_ref[grp] = st * g_row + _dot(st_b, g_mat[u]) + c_mat[u]

    o = oacc_ref[...]
    mean = _head_sum(o) * (1.0 / RW_DIM)
    dlt = o - mean
    var = _head_sum(dlt * dlt) * (1.0 / RW_DIM)
    on = dlt * lax.rsqrt(var + RW_GN_EPS) * gw_ref[...] + gb_ref[...]
    bonus = _head_sum(r * k * rk_ref[...])
    o_ref[...] = ((on + bonus * v) * g_ref[...]).astype(o_ref.dtype)


def _rw_scan_call(r, k, v, lw, kkn, bk, g, r_k, gn_w, gn_b):
    bsz, seq, w = r.shape
    tc = min(256, seq)
    spec = pl.BlockSpec((None, tc, w), lambda b, j: (b, j, 0))
    pspec = pl.BlockSpec((1, w), lambda b, j: (0, 0))
    return pl.pallas_call(
        functools.partial(_rw_scan_body, nsub=tc // CHUNK),
        grid=(bsz, seq // tc),
        in_specs=[spec] * 7 + [pspec] * 3,
        out_specs=spec,
        out_shape=jax.ShapeDtypeStruct((bsz, seq, w), BF16),
        scratch_shapes=[pltpu.VMEM((RW_GROUPS, RW_DIM, RW_GW), F32), pltpu.VMEM((tc, w), F32)],
        compiler_params=_cp(("parallel", "arbitrary")),
        name="rwkv_scan",
    )(r, k, v, lw, kkn, bk, g, r_k.reshape(1, w), gn_w.reshape(1, w), gn_b.reshape(1, w))


def _merge_body(ga_ref, gb_ref, gc_ref, oa_ref, ob_ref, oc_ref, x_ref, gate_ref,
                wa_ref, wb_ref, wc_ref, wo_ref, o_ref):
    merged = (_sigmoid(ga_ref[...].astype(F32)) * _dot(oa_ref[...], wa_ref[...])
              + _sigmoid(gb_ref[...].astype(F32)) * _dot(ob_ref[...], wb_ref[...])
              + _sigmoid(gc_ref[...].astype(F32)) * _dot(oc_ref[...], wc_ref[...]))
    mix = _dot(merged.astype(BF16), wo_ref[...])
    o_ref[...] = x_ref[...] + gate_ref[...] * mix


def _merge_call(u2, o_a, o_b, o_c, x2, ada3, wa, wb, wc, wo, layer, bsz, seq):
    m, d = x2.shape
    tm = min(512, seq)
    per_b = seq // tm
    base = layer * bsz * 6
    wdt = o_a.shape[1]

    def const(arr):
        return pl.BlockSpec(arr.shape, lambda i: (0, 0))

    return pl.pallas_call(
        _merge_body,
        grid=(m // tm,),
        in_specs=[
            pl.BlockSpec((tm, d), lambda i: (i, 0)),
            pl.BlockSpec((tm, d), lambda i: (i, 1)),
            pl.BlockSpec((tm, d), lambda i: (i, 2)),
            pl.BlockSpec((tm, wdt), lambda i: (i, 0)),
            pl.BlockSpec((tm, wdt), lambda i: (i, 0)),
            pl.BlockSpec((tm, wdt), lambda i: (i, 0)),
            pl.BlockSpec((tm, d), lambda i: (i, 0)),
            pl.BlockSpec((None, 1, d), lambda i: (base + (i // per_b) * 6 + 2, 0, 0)),
            const(wa), const(wb), const(wc), const(wo),
        ],
        out_specs=pl.BlockSpec((tm, d), lambda i: (i, 0)),
        out_shape=jax.ShapeDtypeStruct((m, d), F32),
        compiler_params=_cp(("parallel",)),
        name="merge_out",
    )(u2, u2, u2, o_a, o_b, o_c, x2, ada3, wa, wb, wc, wo)


def _ffn_body(x_ref, nw_ref, shift_ref, scale_ref, gate_ref, wi_ref, wo_ref, fn_ref, o_ref, *, final):
    x = x_ref[...]
    h = _norm_mod(x, nw_ref[...], scale_ref[...], shift_ref[...]).astype(BF16)
    half = FFN_HIDDEN // 2
    acc = jnp.zeros(x.shape, F32)
    for c in range(2):
        gh = _dot(h, wi_ref[:, c * half:(c + 1) * half])
        uh = _dot(h, wi_ref[:, FFN_HIDDEN + c * half:FFN_HIDDEN + (c + 1) * half])
        act = (gh * _sigmoid(gh) * uh).astype(BF16)
        acc = acc + _dot(act, wo_ref[c * half:(c + 1) * half, :])
    y = x + gate_ref[...] * acc
    if final:
        ms = jnp.mean(y * y, axis=-1, keepdims=True)
        y = y * lax.rsqrt(ms + NORM_EPS) * fn_ref[...]
    o_ref[...] = y


def _ffn_call(x2, nw, ada3, wi, wo, final_w, layer, bsz, seq, final):
    m, d = x2.shape
    tm = min(512, seq)
    per_b = seq // tm
    base = layer * bsz * 6

    def ada_spec(k):
        return pl.BlockSpec((None, 1, d), lambda i: (base + (i // per_b) * 6 + k, 0, 0))

    def const(arr):
        return pl.BlockSpec(arr.shape, lambda i: (0, 0), pipeline_mode=pl.Buffered(1))

    return pl.pallas_call(
        functools.partial(_ffn_body, final=final),
        grid=(m // tm,),
        in_specs=[
            pl.BlockSpec((tm, d), lambda i: (i, 0)),
            pl.BlockSpec((1, d), lambda i: (0, 0)),
            ada_spec(3), ada_spec(4), ada_spec(5),
            const(wi), const(wo),
            pl.BlockSpec((1, d), lambda i: (0, 0)),
        ],
        out_specs=pl.BlockSpec((tm, d), lambda i: (i, 0)),
        out_shape=jax.ShapeDtypeStruct((m, d), F32),
        compiler_params=_cp(("parallel",)),
        name="ffn_final" if final else "ffn",
    )(x2, nw, ada3, ada3, ada3, wi, wo, final_w)


def _permute_w_in(w):
    def qk_lanes(t):
        t = t.reshape(t.shape[0], DA_HEADS, 2, 2, DA_QK_DIM // 2)
        return jnp.swapaxes(t, 2, 3).reshape(t.shape[0], DA_HEADS * 2 * DA_QK_DIM)

    q, k, v = qk_lanes(w[:, 0:512]), qk_lanes(w[:, 512:1024]), w[:, 1024:1536]
    hgrn = w[:, 1536:3584]
    rw = w[:, 3584:5376]
    gates = w[:, 5376:8448]
    return jnp.concatenate([gates, hgrn, q, k, v, rw], axis=1)


def kernel(x, c, positions, ada_w, ada_b, norm_mix_w, norm_ffn_w, w_in, da_lambda, da_subln_w, hg_lb, hg_norm_w, rw_mu, rw_w0, rw_w2, rw_a0, rw_a2, rw_g2, rw_k_k, rw_k_a, rw_r_k, rw_gn_w, rw_gn_b, rw_v0, rw_v1, rw_v2, w_branch_a, w_branch_b, w_branch_c, w_out, ffn_w_in, ffn_w_out, final_norm_w):
    bsz, seq, d = x.shape
    depth = ada_w.shape[0]
    m = bsz * seq

    ada = _ada_call(c, ada_w, ada_b)
    ada3 = ada.reshape(depth * bsz * 6, 1, d)
    cos, sin_s = _rope_call(positions)

    x2 = x.reshape(m, d)
    v_first = None
    for l in range(depth):
        w_l = _permute_w_in(w_in[l]).astype(BF16)
        u2 = _proj_in_call(x2, norm_mix_w[l].reshape(1, d), ada3, cos, sin_s, w_l, l, bsz, seq)
        u3 = u2.reshape(bsz, seq, IN_COLS)

        o_a = _attn_call(u3, da_lambda[l], da_subln_w[l], l)
        o_b = _hgrn_call(u3, hg_lb, hg_norm_w[l], l)
        vres = None if l == 0 else (v_first, rw_v0[l - 1], rw_v1[l - 1], rw_v2[l - 1])
        r, k, v, lw, kkn, bk, g = _rw_prep_call(u3, rw_mu[l], rw_w0[l], rw_w2[l], rw_a0[l], rw_a2[l],
                                                rw_g2[l], rw_k_k[l], rw_k_a[l], vres)
        if l == 0:
            v_first = v
        o_c = _rw_scan_call(r, k, v, lw, kkn, bk, g, rw_r_k[l], rw_gn_w[l], rw_gn_b[l])

        x2 = _merge_call(u2, o_a.reshape(m, -1), o_b.reshape(m, -1), o_c.reshape(m, -1), x2, ada3,
                         w_branch_a[l].astype(BF16), w_branch_b[l].astype(BF16),
                         w_branch_c[l].astype(BF16), w_out[l].astype(BF16), l, bsz, seq)
        x2 = _ffn_call(x2, norm_ffn_w[l].reshape(1, d), ada3, ffn_w_in[l].astype(BF16),
                       ffn_w_out[l].astype(BF16), final_norm_w.reshape(1, d), l, bsz, seq,
                       final=(l == depth - 1))
    return x2.reshape(bsz, seq, d)
```

```python
import functools
import math

import jax
import jax.numpy as jnp
from jax import lax
from jax.experimental import pallas as pl
from jax.experimental.pallas import tpu as pltpu

F32 = jnp.float32
BF16 = jnp.bfloat16

D_MODEL = 1024
CHUNK = 64
ROPE_THETA = 10000.0
NORM_EPS = 1e-6

DA_HEADS = 4
DA_QK_DIM = 64
DA_V_DIM = 128
HG_HEADS = 4
HG_DIM = 128
RW_HEADS = 8
RW_DIM = 64
RW_WIDTH = 512
RW_GN_EPS = 64e-5
FFN_HIDDEN = 2816
IN_COLS = 8448

LANES = 128
ATTN_LOCKSTEP = 4
LOG2E = 1.4426950408889634
Q_SCALE = DA_QK_DIM ** -0.5 * LOG2E
VT_ROWS = DA_V_DIM + 16
SUB = 16
RW_PACK = 4
RW_GW = RW_PACK * RW_DIM
RW_GROUPS = RW_HEADS // RW_PACK

COL_GATES = 0
COL_HF = 3072
COL_HI = 3584
COL_HQ = 4096
COL_HG = 4608
COL_Q = 5120
COL_K = 5632
COL_V = 6144
COL_RR = 6656
COL_RK = 7168
COL_RV = 7680
COL_RL = 8192
RW_LORA_W = 256
HEADW = 512
PROJ_TN = 768
GROUPS_PER_TILE = PROJ_TN // LANES
ROPE_Q0 = COL_Q // LANES
ROPE_K0 = COL_K // LANES

VMEM_LIMIT = 56 * 1024 * 1024


def _cp(sem):
    return pltpu.CompilerParams(dimension_semantics=sem, vmem_limit_bytes=VMEM_LIMIT)


def _sigmoid(x):
    return 1.0 / (1.0 + jnp.exp(-x))


def _softplus(x):
    return jnp.maximum(x, 0.0) + jnp.log(1.0 + jnp.exp(-jnp.abs(x)))


def _dot(a, b):
    return jnp.dot(a, b, preferred_element_type=F32)


def _dot_nt(a, b):
    return lax.dot_general(a, b, (((1,), (1,)), ((), ())), preferred_element_type=F32)


def _dot_exact_lhs(a01, x):
    n = x.shape[1]
    hi = x.astype(BF16)
    lo = (x - hi.astype(F32)).astype(BF16)
    res = _dot(a01, jnp.concatenate([hi, lo], axis=1))
    return res[:, 0:n] + res[:, n:2 * n]


def _head_sum(x):
    ones_g = _head_ones(RW_GW, RW_DIM)
    hi = x.astype(BF16)
    lo = (x - hi.astype(F32)).astype(BF16)
    m = x.shape[0]
    outs = []
    for grp in range(x.shape[1] // RW_GW):
        cols = slice(grp * RW_GW, (grp + 1) * RW_GW)
        res = _dot(jnp.concatenate([hi[:, cols], lo[:, cols]], axis=0), ones_g)
        outs.append(res[:m] + res[m:])
    return jnp.concatenate(outs, axis=1)


def _dot_hp(a, b):
    a_hi = a.astype(BF16)
    a_lo = (a - a_hi.astype(F32)).astype(BF16)
    b_hi = b.astype(BF16)
    b_lo = (b - b_hi.astype(F32)).astype(BF16)
    return _dot(a_hi, b_hi) + _dot(a_hi, b_lo) + _dot(a_lo, b_hi)


def _iota(shape, dim):
    return lax.broadcasted_iota(jnp.int32, shape, dim)


def _ada_body(c_ref, w_ref, b_ref, o_ref):
    c = c_ref[...]
    ca = c * _sigmoid(c)
    o_ref[...] = _dot_hp(ca, w_ref[...]) + b_ref[...]


def _ada_call(c, ada_w, ada_b):
    nl, d, n6 = ada_w.shape
    bsz = c.shape[0]
    tn = 1536
    return pl.pallas_call(
        _ada_body,
        grid=(nl, n6 // tn),
        in_specs=[
            pl.BlockSpec((bsz, d), lambda l, j: (0, 0)),
            pl.BlockSpec((None, d, tn), lambda l, j: (l, 0, j)),
            pl.BlockSpec((None, 1, tn), lambda l, j: (l, 0, j)),
        ],
        out_specs=pl.BlockSpec((None, bsz, tn), lambda l, j: (l, 0, j)),
        out_shape=jax.ShapeDtypeStruct((nl, bsz, n6), F32),
        compiler_params=_cp(("parallel", "parallel")),
        name="ada",
    )(c, ada_w, ada_b.reshape(nl, 1, n6))


def _rope_body(pos_ref, invf_ref, sgn_ref, cos_ref, sin_ref):
    ang = pos_ref[...].astype(F32) * invf_ref[...]
    cos_ref[...] = jnp.cos(ang)
    sin_ref[...] = jnp.sin(ang) * sgn_ref[...]


def _rope_call(positions):
    m = positions.size
    tm = min(2048, m)
    inv_freq = ROPE_THETA ** (-jnp.arange(0, DA_QK_DIM, 2, dtype=F32) / DA_QK_DIM)
    invf = jnp.tile(inv_freq, LANES // (DA_QK_DIM // 2)).reshape(1, LANES)
    sgn = jnp.where(jnp.arange(LANES) < LANES // 2, -1.0, 1.0).astype(F32).reshape(1, LANES)
    return pl.pallas_call(
        _rope_body,
        grid=(m // tm,),
        in_specs=[
            pl.BlockSpec((tm, 1), lambda i: (i, 0)),
            pl.BlockSpec((1, LANES), lambda i: (0, 0)),
            pl.BlockSpec((1, LANES), lambda i: (0, 0)),
        ],
        out_specs=[pl.BlockSpec((tm, LANES), lambda i: (i, 0))] * 2,
        out_shape=[jax.ShapeDtypeStruct((m, LANES), F32)] * 2,
        compiler_params=_cp(("parallel",)),
        name="rope_tables",
    )(positions.reshape(m, 1), invf, sgn)


def _norm_mod(x, nw, scale, shift):
    ms = jnp.mean(x * x, axis=-1, keepdims=True)
    y = x * lax.rsqrt(ms + NORM_EPS) * nw
    return y * (1.0 + scale) + shift


def _rope_kind(group):
    if ROPE_Q0 <= group < ROPE_Q0 + DA_HEADS:
        return "q"
    if ROPE_K0 <= group < ROPE_K0 + DA_HEADS:
        return "k"
    return None


_ROPE_TILES = sorted({g // GROUPS_PER_TILE for g in range(IN_COLS // LANES) if _rope_kind(g)})


def _proj_in_body(x_ref, nw_ref, shift_ref, scale_ref, cos_ref, sin_ref, w_ref, o_ref, h_ref):
    j = pl.program_id(1)

    @pl.when(j == 0)
    def _():
        h_ref[...] = _norm_mod(x_ref[...], nw_ref[...], scale_ref[...], shift_ref[...]).astype(BF16)

    for jt in _ROPE_TILES:
        @pl.when(j == jt)
        def _(jt=jt):
            acc = _dot(h_ref[...], w_ref[...])
            cos = cos_ref[...]
            sin_s = sin_ref[...]
            for g in range(GROUPS_PER_TILE):
                slab = acc[:, g * LANES:(g + 1) * LANES]
                kind = _rope_kind(jt * GROUPS_PER_TILE + g)
                if kind is not None:
                    slab = slab * cos + pltpu.roll(slab, LANES // 2, 1) * sin_s
                    if kind == "q":
                        slab = slab * Q_SCALE
                o_ref[:, g * LANES:(g + 1) * LANES] = slab.astype(o_ref.dtype)

    plain = j != _ROPE_TILES[0]
    for jt in _ROPE_TILES[1:]:
        plain = jnp.logical_and(plain, j != jt)

    @pl.when(plain)
    def _():
        o_ref[...] = _dot(h_ref[...], w_ref[...]).astype(o_ref.dtype)


def _proj_in_call(x2, nw, ada3, cos, sin_s, w_bf16, layer, bsz, seq):
    m, d = x2.shape
    n = w_bf16.shape[1]
    tm = min(2048, seq)
    per_b = seq // tm
    base = layer * bsz * 6

    def ada_spec(k):
        return pl.BlockSpec((None, 1, d), lambda i, j: (base + (i // per_b) * 6 + k, 0, 0))

    return pl.pallas_call(
        _proj_in_body,
        grid=(m // tm, n // PROJ_TN),
        in_specs=[
            pl.BlockSpec((tm, d), lambda i, j: (i, 0)),
            pl.BlockSpec((1, d), lambda i, j: (0, 0)),
            ada_spec(0),
            ada_spec(1),
            pl.BlockSpec((tm, LANES), lambda i, j: (i, 0)),
            pl.BlockSpec((tm, LANES), lambda i, j: (i, 0)),
            pl.BlockSpec((d, PROJ_TN), lambda i, j: (0, j)),
        ],
        out_specs=pl.BlockSpec((tm, PROJ_TN), lambda i, j: (i, j)),
        out_shape=jax.ShapeDtypeStruct((m, n), BF16),
        scratch_shapes=[pltpu.VMEM((tm, d), BF16)],
        compiler_params=_cp(("parallel", "arbitrary")),
        name="proj_in",
    )(x2, nw, ada3, ada3, cos, sin_s, w_bf16)


def _attn_body(lam_ref, q_ref, k_ref, v_ref, sw_ref, o_ref, vt_ref, *, tq, lam_init):
    i = pl.program_id(1)

    @pl.when(i == 0)
    def _():
        for h in range(DA_HEADS):
            vt_ref[h, 0:DA_V_DIM, :] = v_ref[:, h * LANES:(h + 1) * LANES].astype(F32).T.astype(BF16)
            vt_ref[h, DA_V_DIM:VT_ROWS, :] = jnp.ones((VT_ROWS - DA_V_DIM, vt_ref.shape[2]), BF16)

    lv = lam_ref[...]
    lam = (jnp.exp(jnp.sum(lv[0:1] * lv[1:2], axis=1, keepdims=True))
           - jnp.exp(jnp.sum(lv[2:3] * lv[3:4], axis=1, keepdims=True)) + lam_init)

    map0 = jnp.bitwise_and(_iota((1, LANES), 1), DA_QK_DIM - 1) < DA_QK_DIM // 2
    zero = jnp.zeros((), BF16)
    qs = []
    for h in range(DA_HEADS):
        q = q_ref[:, h * LANES:(h + 1) * LANES]
        q0 = jnp.where(map0, q, zero)
        q1 = jnp.where(map0, zero, q)
        qs.append(jnp.concatenate([q0, q1], axis=0))

    def mask_for(nk):
        kk = _iota((nk, 2 * tq), 0)
        qq = _iota((nk, 2 * tq), 1)
        q_chunk = jnp.right_shift(jnp.bitwise_and(qq, tq - 1), 6) + (nk - tq) // CHUNK
        return jnp.right_shift(kk, 6) <= q_chunk

    def step(start, nk, carry, allowed):
        out = [None] * DA_HEADS
        for g0 in range(0, DA_HEADS, ATTN_LOCKSTEP):
            grp = range(g0, g0 + ATTN_LOCKSTEP)
            ss = {h: _dot_nt(k_ref[pl.ds(start, nk), h * LANES:(h + 1) * LANES], qs[h]) for h in grp}
            ps, stats = {}, {}
            for h in grp:
                m_i, _ = carry[h]
                s = ss[h]
                if allowed is not None:
                    s = jnp.where(allowed, s, -jnp.inf)
                m_new = jnp.maximum(m_i, jnp.max(s, axis=0, keepdims=True))
                ps[h] = jnp.exp2(s - m_new).astype(BF16)
                stats[h] = (m_new, jnp.exp2(m_i - m_new))
            for h in grp:
                m_new, alpha = stats[h]
                pv = _dot(vt_ref[h, :, pl.ds(start, nk)], ps[h])
                out[h] = (m_new, alpha * carry[h][1] + pv)
        return tuple(out)

    init = tuple((jnp.full((1, 2 * tq), -jnp.inf, F32), jnp.zeros((VT_ROWS, 2 * tq), F32))
                 for _ in range(DA_HEADS))
    carry = lax.fori_loop(
        0, i // 2, lambda j, c: step(pl.multiple_of(j * (2 * tq), 2 * tq), 2 * tq, c, None), init)
    carry = lax.cond(
        i % 2 == 1,
        lambda c: step(pl.multiple_of((i - 1) * tq, tq), 2 * tq, c, mask_for(2 * tq)),
        lambda c: step(pl.multiple_of(i * tq, tq), tq, c, mask_for(tq)),
        carry)

    for h in range(DA_HEADS):
        acc = carry[h][1]
        o_t = acc[:DA_V_DIM] / acc[DA_V_DIM:DA_V_DIM + 1]
        o_t = o_t[:, :tq] - lam * o_t[:, tq:]
        ms = jnp.mean(o_t * o_t, axis=0, keepdims=True)
        o = (o_t * lax.rsqrt(ms + NORM_EPS)).T * sw_ref[...] * (1.0 - lam_init)
        o_ref[:, h * LANES:(h + 1) * LANES] = o.astype(o_ref.dtype)


def _attn_call(u3, lam_vecs, subln_w, layer):
    bsz, seq, _ = u3.shape
    tq = min(256, seq)
    lam_init = 0.8 - 0.6 * math.exp(-0.3 * layer)
    return pl.pallas_call(
        functools.partial(_attn_body, tq=tq, lam_init=lam_init),
        grid=(bsz, seq // tq),
        in_specs=[
            pl.BlockSpec((4, DA_QK_DIM), lambda b, i: (0, 0)),
            pl.BlockSpec((None, tq, HEADW), lambda b, i: (b, i, COL_Q // HEADW)),
            pl.BlockSpec((None, seq, HEADW), lambda b, i: (b, 0, COL_K // HEADW)),
            pl.BlockSpec((None, seq, HEADW), lambda b, i: (b, 0, COL_V // HEADW)),
            pl.BlockSpec((1, LANES), lambda b, i: (0, 0)),
        ],
        out_specs=pl.BlockSpec((None, tq, HEADW), lambda b, i: (b, i, 0)),
        out_shape=jax.ShapeDtypeStruct((bsz, seq, HEADW), BF16),
        scratch_shapes=[pltpu.VMEM((DA_HEADS, VT_ROWS, seq), BF16)],
        compiler_params=_cp(("parallel", "arbitrary")),
        name="diff_attn",
    )(lam_vecs, u3, u3, u3, subln_w.reshape(1, LANES))


def _hgrn_body(lbp_ref, z_ref, i_ref, q_ref, g_ref, nw_ref, o_ref, *, layer, nchunks):
    lbp = lbp_ref[...]
    e = jnp.exp(lbp - jnp.max(lbp, axis=0, keepdims=True))
    sm = e / jnp.sum(e, axis=0, keepdims=True)
    cs = sm[0:1]
    for t in range(1, layer + 1):
        cs = cs + sm[t:t + 1]
    lb = cs - sm[0:1]
    log_lb = jnp.log(lb)
    log1m = jnp.log1p(-lb)

    rr = _iota((CHUNK, CHUNK), 0)
    cc = _iota((CHUNK, CHUNK), 1)
    sub_start = jnp.bitwise_and(rr, ~(SUB - 1))
    cs_mat = jnp.concatenate([(cc <= rr).astype(BF16), (cc < sub_start).astype(BF16)], axis=0)
    diag_ok = jnp.logical_and(cc <= rr, jnp.right_shift(cc, 4) == jnp.right_shift(rr, 4))
    nsub = CHUNK // SUB
    heads = range(HG_HEADS)

    def hcols(x, h):
        return x[:, h * HG_DIM:(h + 1) * HG_DIM]

    def chunk(c, states):
        sl = pl.ds(pl.multiple_of(c * CHUNK, CHUNK), CHUNK)
        z = z_ref[sl, :].astype(F32)
        y = log1m - _softplus(-z)
        mx = jnp.maximum(log_lb, y)
        lf = mx + jnp.log(1.0 + jnp.exp(-jnp.abs(log_lb - y)))
        key = (1.0 - lb) * _sigmoid(-z)
        qc = q_ref[sl, :].astype(F32)
        vc = i_ref[sl, :]
        vt = vc.astype(F32).T.astype(BF16)

        tot = _dot_exact_lhs(cs_mat, lf * LOG2E)
        b = tot[:CHUNK]
        d = b - tot[CHUNK:]
        b_last = b[CHUNK - 1:CHUNK, :]

        q_d = (qc * jnp.exp2(d)).astype(BF16)
        k_d = (key * jnp.exp2(-d)).astype(BF16)
        q_js, k_js = [], []
        for jb in range(nsub - 1):
            lo, hi = jb * SUB, (jb + 1) * SUB
            e_j = b[hi - 1:hi, :]
            q_j = (qc[hi:] * jnp.exp2(b[hi:] - e_j)).astype(BF16)
            k_j = (key[lo:hi] * jnp.exp2(e_j - b[lo:hi])).astype(BF16)
            q_js.append(jnp.concatenate([jnp.zeros((hi, HEADW), BF16), q_j], axis=0))
            pieces = [k_j, jnp.zeros((CHUNK - hi, HEADW), BF16)]
            if lo:
                pieces.insert(0, jnp.zeros((lo, HEADW), BF16))
            k_js.append(jnp.concatenate(pieces, axis=0))
        q_in = (qc * jnp.exp2(b)).astype(BF16)
        k_out = (key * jnp.exp2(b_last - b)).astype(BF16)
        decay = jnp.exp2(b_last)

        s_diag = [_dot_nt(hcols(q_d, h), hcols(k_d, h)) for h in heads]
        s_off = [_dot_nt(jnp.concatenate([hcols(x, h) for x in q_js], axis=1),
                         jnp.concatenate([hcols(x, h) for x in k_js], axis=1)) for h in heads]
        scores = [(jnp.where(diag_ok, s_diag[h], 0.0) + s_off[h]).astype(BF16) for h in heads]
        o_intra = [_dot(scores[h], hcols(vc, h)) for h in heads]
        o_inter = [_dot_nt(hcols(q_in, h), states[h].astype(BF16)) for h in heads]
        upd = [_dot(vt[h * HG_DIM:(h + 1) * HG_DIM, :], hcols(k_out, h)) for h in heads]
        new_states = tuple(states[h] * hcols(decay, h) + upd[h] for h in heads)

        outs = []
        for h in heads:
            o = o_intra[h] + o_inter[h]
            ms = jnp.mean(o * o, axis=-1, keepdims=True)
            outs.append(o * lax.rsqrt(ms + NORM_EPS))
        g = g_ref[sl, :].astype(F32)
        o = jnp.concatenate(outs, axis=1) * nw_ref[...] * (g * _sigmoid(g))
        o_ref[sl, :] = o.astype(o_ref.dtype)
        return new_states

    init = tuple(jnp.zeros((HG_DIM, HG_DIM), F32) for _ in heads)
    lax.fori_loop(0, nchunks, chunk, init, unroll=4)


def _hgrn_call(u3, hg_lb, norm_w, layer):
    bsz, seq, _ = u3.shape
    nl = hg_lb.shape[0]

    def col(base):
        return lambda b: (b, 0, base // HEADW)

    blk = (None, seq, HEADW)
    return pl.pallas_call(
        functools.partial(_hgrn_body, layer=layer, nchunks=seq // CHUNK),
        grid=(bsz,),
        in_specs=[
            pl.BlockSpec((nl, HEADW), lambda b: (0, 0)),
            pl.BlockSpec(blk, col(COL_HF)),
            pl.BlockSpec(blk, col(COL_HI)),
            pl.BlockSpec(blk, col(COL_HQ)),
            pl.BlockSpec(blk, col(COL_HG)),
            pl.BlockSpec((1, HEADW), lambda b: (0, 0)),
        ],
        out_specs=pl.BlockSpec(blk, lambda b: (b, 0, 0)),
        out_shape=jax.ShapeDtypeStruct((bsz, seq, HEADW), BF16),
        compiler_params=_cp(("parallel",)),
        name="hgrn2",
    )(hg_lb, u3, u3, u3, u3, jnp.tile(norm_w, HG_HEADS).reshape(1, HEADW))


def _head_ones(n, width):
    r = _iota((n, n), 0) // width
    c = _iota((n, n), 1) // width
    return (r == c).astype(BF16)


def _rw_prep(cur, prev, prm, vres, has_prev):
    mu_ref, w0_ref, w2_ref, a0_ref, a2_ref, g2_ref, kk_ref, ka_ref = prm
    ts = cur[0].shape[0]
    row = _iota((ts, 1), 0)

    def shifted(idx, lo, hi):
        u = cur[idx][...].astype(F32)
        if prev is None:
            prev_row = jnp.zeros((1, u.shape[1]), F32)
        else:
            prev_row = prev[idx][SUB - 1:SUB, :].astype(F32) * has_prev.astype(F32)
        u_prev = jnp.where(row == 0, prev_row, pltpu.roll(u, 1, 0))
        return u + (u_prev - u) * mu_ref[:, lo:hi]

    r = shifted(0, 0, 512)
    k = shifted(1, 512, 1024)
    v = shifted(2, 1024, 1536)
    lora = shifted(3, 1536, 1792)
    w_lo = lora[:, 0:64]
    a_lo = lora[:, 64:128]
    g_lo = lora[:, 128:256]

    def lora_dot(act, w_ref):
        return _dot(act.astype(BF16), w_ref[...].astype(BF16))

    wpre = w0_ref[...] + lora_dot(jnp.tanh(w_lo), w2_ref)
    w = -_softplus(-wpre) - 0.5
    lw = -jnp.exp(w)
    a = _sigmoid(a0_ref[...] + lora_dot(a_lo, a2_ref))
    g = lora_dot(_sigmoid(g_lo), g2_ref)
    if vres is not None:
        vf_ref, v0_ref, v1_ref, v2_ref = vres
        mix = _sigmoid(v0_ref[...] + lora_dot(lora_dot(v, v1_ref), v2_ref))
        v = v + (vf_ref[...].astype(F32) - v) * mix

    kk = k * kk_ref[...]
    ssq = _head_sum(kk * kk)
    kkn = kk * lax.rsqrt(jnp.maximum(ssq, 1e-24))
    k = k * (1.0 + (a - 1.0) * ka_ref[...])
    return r, k, v, lw, kkn, kkn * a, g


def _rw_scan(r_ref, k_ref, v_ref, lw_ref, kk_ref, bk_ref, g_ref, rk_ref, gw_ref, gb_ref,
             o_ref, state_ref, oacc_ref, nsub):
    t = CHUNK
    gw = RW_GW
    tc = nsub * t
    units = [(s, grp) for s in range(nsub) for grp in range(RW_GROUPS)]

    rr = _iota((tc, tc), 0)
    cc = _iota((tc, tc), 1)
    tril_bd = jnp.logical_and(cc <= rr, (cc // t) == (rr // t)).astype(BF16)
    rr4 = _iota((t, gw), 0)
    cj = jnp.bitwise_and(_iota((t, gw), 1), t - 1)
    strict = cj < rr4
    incl = cj <= rr4
    eye_c = (cj == rr4).astype(F32)
    bd_mask = (_iota((gw, gw), 0) // t) == (_iota((gw, gw), 1) // RW_DIM)
    zero_b = jnp.zeros((), BF16)

    left_head = _iota((t, LANES), 1) < RW_DIM
    zeros_tile = jnp.zeros((t, LANES), BF16)

    def bd(xc):
        xb = xc.astype(BF16)
        rows = []
        for h in range(RW_PACK):
            tile = xb[:, (h // 2) * LANES:(h // 2 + 1) * LANES]
            kept = jnp.where(left_head, tile, zero_b) if h % 2 == 0 else jnp.where(left_head, zero_b, tile)
            rows.append(jnp.concatenate([kept, zeros_tile] if h < 2 else [zeros_tile, kept], axis=1))
        return jnp.concatenate(rows, axis=0)

    def cut(x, u):
        s, grp = u
        return x[s * t:(s + 1) * t, grp * gw:(grp + 1) * gw]

    r = r_ref[...].astype(F32)
    k = k_ref[...].astype(F32)
    v = v_ref[...].astype(F32)
    lw = lw_ref[...] * LOG2E
    kk = kk_ref[...].astype(F32)
    bk = bk_ref[...].astype(F32)
    c = _dot_exact_lhs(tril_bd, lw)
    e_neg = jnp.exp2(-c)
    a_t = -kk * jnp.exp2(c - lw)
    b_t = bk * e_neg
    k_t = k * e_neg
    r_t = r * jnp.exp2(c)
    c_last = jnp.concatenate(
        [jnp.broadcast_to(c[(s + 1) * t - 1:(s + 1) * t, :], (t, RW_WIDTH)) for s in range(nsub)], axis=0)
    decay_out = jnp.exp2(c_last - c)
    b_g = bk * decay_out
    k_g = k * decay_out
    gamma = jnp.exp2(c_last)

    ar = {u: jnp.concatenate([cut(a_t, u), cut(r_t, u)], axis=0).astype(BF16) for u in units}
    x_bk = {u: _dot_nt(ar[u], jnp.concatenate([bd(cut(b_t, u)), bd(cut(k_t, u))], axis=0)) for u in units}
    l_c = {u: jnp.where(strict, x_bk[u][:t, :gw], 0.0) for u in units}
    m_c = {u: jnp.where(strict, x_bk[u][:t, gw:], 0.0).astype(BF16) for u in units}
    rb = {u: jnp.where(incl, x_bk[u][t:, :gw], 0.0).astype(BF16) for u in units}
    rkm = {u: jnp.where(incl, x_bk[u][t:, gw:], 0.0).astype(BF16) for u in units}
    v_bd = {u: bd(cut(v, u)) for u in units}

    p = {u: _dot(l_c[u].astype(BF16), bd(l_c[u])) for u in units}
    s_inv = {u: eye_c + l_c[u] for u in units}
    mo = {u: _dot(jnp.concatenate([m_c[u], rkm[u]], axis=0), v_bd[u]) for u in units}
    mv = {u: mo[u][:t] for u in units}
    o_kv = {u: mo[u][t:] for u in units}
    for lvl in range(1, 6):
        last = lvl == 5
        nxt_p, nxt_s = {}, {}
        for u in units:
            if last:
                nxt_s[u] = s_inv[u] + _dot(s_inv[u].astype(BF16), bd(p[u]))
            else:
                res = _dot(jnp.concatenate([s_inv[u], p[u]], axis=0).astype(BF16), bd(p[u]))
                nxt_s[u] = s_inv[u] + res[:t]
                nxt_p[u] = res[t:]
        p, s_inv = nxt_p, nxt_s

    rbs = {u: _dot(rb[u], bd(s_inv[u])) for u in units}
    wu = {u: _dot(jnp.concatenate([s_inv[u], rbs[u]], axis=0).astype(BF16),
                  jnp.concatenate([bd(cut(a_t, u)), bd(mv[u])], axis=1)) for u in units}
    w_t = {u: wu[u][:t, :gw] for u in units}
    u_t = {u: wu[u][:t, gw:] for u in units}
    r_hat = {u: (cut(r_t, u) + wu[u][t:, :gw]).astype(BF16) for u in units}
    o_hat = {u: wu[u][t:, gw:] + o_kv[u] for u in units}
    g_mat = {u: jnp.where(bd_mask, _dot(w_t[u].T.astype(BF16), cut(b_g, u).astype(BF16)), 0.0).astype(BF16)
             for u in units}
    c_mat = {}
    for u in units:
        uv_t = jnp.concatenate([u_t[u], cut(v, u)], axis=0).T.astype(BF16)
        bkg = jnp.concatenate([cut(b_g, u), cut(k_g, u)], axis=0).astype(BF16)
        full = jnp.where(bd_mask, _dot(uv_t, bkg), 0.0)
        c_mat[u] = full[0:t] + full[t:2 * t] + full[2 * t:3 * t] + full[3 * t:4 * t]

    for s in range(nsub):
        for grp in range(RW_GROUPS):
            u = (s, grp)
            st = state_ref[grp]
            st_b = st.astype(BF16)
            oacc_ref[s * t:(s + 1) * t, grp * gw:(grp + 1) * gw] = _dot_nt(r_hat[u], bd(st_b)) + o_hat[u]
            g_row = gamma[(s + 1) * t - 1:(s + 1) * t, grp * gw:(grp + 1) * gw]
            state_ref[grp] = st * g_row + _dot(st_b, g_mat[u]) + c_mat[u]

    o = oacc_ref[...]
    mean = _head_sum(o) * (1.0 / RW_DIM)
    dlt = o - mean
    var = _head_sum(dlt * dlt) * (1.0 / RW_DIM)
    on = dlt * lax.rsqrt(var + RW_GN_EPS) * gw_ref[...] + gb_ref[...]
    bonus = _head_sum(r * k * rk_ref[...])
    o_ref[...] = ((on + bonus * v) * g_ref[...]).astype(o_ref.dtype)


RW_SLOT_DTYPES = (BF16, BF16, BF16, F32, BF16, BF16, BF16)


def _rw_mix_body(*refs, has_vres, emit_v, nsub):
    nxt, nxt_prev, first, prm = refs[0:4], refs[4:8], refs[8:12], refs[12:20]
    pos = 20
    vres_nxt = vres_first = None
    if has_vres:
        vf_nxt, vf_first, v0, v1, v2 = refs[pos:pos + 5]
        vres_nxt, vres_first = (vf_nxt, v0, v1, v2), (vf_first, v0, v1, v2)
        pos += 5
    rk_ref, gw_ref, gb_ref = refs[pos:pos + 3]
    pos += 3
    o_ref = refs[pos]
    pos += 1
    v_out = None
    if emit_v:
        v_out = refs[pos]
        pos += 1
    state_ref, oacc_ref = refs[pos:pos + 2]
    slots = refs[pos + 2:pos + 9]

    j = pl.program_id(1)
    last = pl.num_programs(1) - 1

    def put(slot, vals):
        for ref, val in zip(slots, vals):
            ref[slot] = val.astype(ref.dtype)

    @pl.when(j == 0)
    def _():
        state_ref[...] = jnp.zeros_like(state_ref)
        put(0, _rw_prep(first, None, prm, vres_first, None))

    put((j + 1) % 2, _rw_prep(nxt, nxt_prev, prm, vres_nxt, jnp.minimum(j + 1, last) > 0))

    cur = [ref.at[j % 2] for ref in slots]
    _rw_scan(*cur, rk_ref, gw_ref, gb_ref, o_ref, state_ref, oacc_ref, nsub)
    if emit_v:
        v_out[...] = cur[2][...]


def _rw_mix_call(u3, mu, w0, w2, a0, a2, g2, k_k, k_a, vres, r_k, gn_w, gn_b, emit_v):
    bsz, seq, _ = u3.shape
    tc = min(256, seq)
    n = seq // tc
    w = RW_WIDTH
    has_vres = vres is not None

    def full(arr):
        return pl.BlockSpec(arr.shape, lambda b, j: (0,) * arr.ndim)

    def nxt(col, width):
        return pl.BlockSpec((None, tc, width), lambda b, j: (b, jnp.minimum(j + 1, n - 1), col // width))

    def nxt_prev(col, width):
        return pl.BlockSpec(
            (None, SUB, width),
            lambda b, j: (b, jnp.maximum(jnp.minimum(j + 1, n - 1) * (tc // SUB) - 1, 0), col // width))

    def first(col, width):
        return pl.BlockSpec((None, tc, width), lambda b, j: (b, 0, col // width))

    cols = [(COL_RR, w), (COL_RK, w), (COL_RV, w), (COL_RL, RW_LORA_W)]
    params = [mu.reshape(1, -1), w0.reshape(1, w), w2, a0.reshape(1, w), a2, g2,
              k_k.reshape(1, w), k_a.reshape(1, w)]
    in_specs = [f(c, wd) for f in (nxt, nxt_prev, first) for c, wd in cols] + [full(p) for p in params]
    args = [u3] * 12 + params
    if has_vres:
        v_first, v0, v1, v2 = vres
        extra = [v0.reshape(1, w), v1, v2]
        in_specs += [pl.BlockSpec((None, tc, w), lambda b, j: (b, jnp.minimum(j + 1, n - 1), 0)),
                     pl.BlockSpec((None, tc, w), lambda b, j: (b, 0, 0))] + [full(p) for p in extra]
        args += [v_first, v_first] + extra
    tail = [r_k.reshape(1, w), gn_w.reshape(1, w), gn_b.reshape(1, w)]
    in_specs += [full(p) for p in tail]
    args += tail

    out_spec = pl.BlockSpec((None, tc, w), lambda b, j: (b, j, 0))
    out_shape = jax.ShapeDtypeStruct((bsz, seq, w), BF16)
    scratch = [pltpu.VMEM((RW_GROUPS, RW_DIM, RW_GW), F32), pltpu.VMEM((tc, w), F32)]
    scratch += [pltpu.VMEM((2, tc, w), dt) for dt in RW_SLOT_DTYPES]
    return pl.pallas_call(
        functools.partial(_rw_mix_body, has_vres=has_vres, emit_v=emit_v, nsub=tc // CHUNK),
        grid=(bsz, n),
        in_specs=in_specs,
        out_specs=[out_spec, out_spec] if emit_v else out_spec,
        out_shape=[out_shape, out_shape] if emit_v else out_shape,
        scratch_shapes=scratch,
        compiler_params=_cp(("parallel", "arbitrary")),
        name="rwkv7",
    )(*args)


def _merge_body(ga_ref, gb_ref, gc_ref, oa_ref, ob_ref, oc_ref, x_ref, gate_ref,
                wa_ref, wb_ref, wc_ref, wo_ref, o_ref):
    merged = (_sigmoid(ga_ref[...].astype(F32)) * _dot(oa_ref[...], wa_ref[...])
              + _sigmoid(gb_ref[...].astype(F32)) * _dot(ob_ref[...], wb_ref[...])
              + _sigmoid(gc_ref[...].astype(F32)) * _dot(oc_ref[...], wc_ref[...]))
    mix = _dot(merged.astype(BF16), wo_ref[...])
    o_ref[...] = x_ref[...] + gate_ref[...] * mix


def _merge_call(u2, o_a, o_b, o_c, x2, ada3, wa, wb, wc, wo, layer, bsz, seq):
    m, d = x2.shape
    tm = min(512, seq)
    per_b = seq // tm
    base = layer * bsz * 6
    wdt = o_a.shape[1]

    def const(arr):
        return pl.BlockSpec(arr.shape, lambda i: (0, 0))

    return pl.pallas_call(
        _merge_body,
        grid=(m // tm,),
        in_specs=[
            pl.BlockSpec((tm, d), lambda i: (i, 0)),
            pl.BlockSpec((tm, d), lambda i: (i, 1)),
            pl.BlockSpec((tm, d), lambda i: (i, 2)),
            pl.BlockSpec((tm, wdt), lambda i: (i, 0)),
            pl.BlockSpec((tm, wdt), lambda i: (i, 0)),
            pl.BlockSpec((tm, wdt), lambda i: (i, 0)),
            pl.BlockSpec((tm, d), lambda i: (i, 0)),
            pl.BlockSpec((None, 1, d), lambda i: (base + (i // per_b) * 6 + 2, 0, 0)),
            const(wa), const(wb), const(wc), const(wo),
        ],
        out_specs=pl.BlockSpec((tm, d), lambda i: (i, 0)),
        out_shape=jax.ShapeDtypeStruct((m, d), F32),
        compiler_params=_cp(("parallel",)),
        name="merge_out",
    )(u2, u2, u2, o_a, o_b, o_c, x2, ada3, wa, wb, wc, wo)


def _ffn_body(x_ref, nw_ref, shift_ref, scale_ref, gate_ref, wi_ref, wo_ref, fn_ref, o_ref, *, final):
    x = x_ref[...]
    h = _norm_mod(x, nw_ref[...], scale_ref[...], shift_ref[...]).astype(BF16)
    half = FFN_HIDDEN // 2
    acc = jnp.zeros(x.shape, F32)
    for c in range(2):
        gh = _dot(h, wi_ref[:, c * half:(c + 1) * half])
        uh = _dot(h, wi_ref[:, FFN_HIDDEN + c * half:FFN_HIDDEN + (c + 1) * half])
        act = (gh * _sigmoid(gh) * uh).astype(BF16)
        acc = acc + _dot(act, wo_ref[c * half:(c + 1) * half, :])
    y = x + gate_ref[...] * acc
    if final:
        ms = jnp.mean(y * y, axis=-1, keepdims=True)
        y = y * lax.rsqrt(ms + NORM_EPS) * fn_ref[...]
    o_ref[...] = y


def _ffn_call(x2, nw, ada3, wi, wo, final_w, layer, bsz, seq, final):
    m, d = x2.shape
    tm = min(512, seq)
    per_b = seq // tm
    base = layer * bsz * 6

    def ada_spec(k):
        return pl.BlockSpec((None, 1, d), lambda i: (base + (i // per_b) * 6 + k, 0, 0))

    def const(arr):
        return pl.BlockSpec(arr.shape, lambda i: (0, 0), pipeline_mode=pl.Buffered(1))

    return pl.pallas_call(
        functools.partial(_ffn_body, final=final),
        grid=(m // tm,),
        in_specs=[
            pl.BlockSpec((tm, d), lambda i: (i, 0)),
            pl.BlockSpec((1, d), lambda i: (0, 0)),
            ada_spec(3), ada_spec(4), ada_spec(5),
            const(wi), const(wo),
            pl.BlockSpec((1, d), lambda i: (0, 0)),
        ],
        out_specs=pl.BlockSpec((tm, d), lambda i: (i, 0)),
        out_shape=jax.ShapeDtypeStruct((m, d), F32),
        compiler_params=_cp(("parallel",)),
        name="ffn_final" if final else "ffn",
    )(x2, nw, ada3, ada3, ada3, wi, wo, final_w)


def _permute_w_in(w):
    def qk_lanes(t):
        t = t.reshape(t.shape[0], DA_HEADS, 2, 2, DA_QK_DIM // 2)
        return jnp.swapaxes(t, 2, 3).reshape(t.shape[0], DA_HEADS * 2 * DA_QK_DIM)

    q, k, v = qk_lanes(w[:, 0:512]), qk_lanes(w[:, 512:1024]), w[:, 1024:1536]
    hgrn = w[:, 1536:3584]
    rw = w[:, 3584:5376]
    gates = w[:, 5376:8448]
    return jnp.concatenate([gates, hgrn, q, k, v, rw], axis=1)


def kernel(x, c, positions, ada_w, ada_b, norm_mix_w, norm_ffn_w, w_in, da_lambda, da_subln_w, hg_lb, hg_norm_w, rw_mu, rw_w0, rw_w2, rw_a0, rw_a2, rw_g2, rw_k_k, rw_k_a, rw_r_k, rw_gn_w, rw_gn_b, rw_v0, rw_v1, rw_v2, w_branch_a, w_branch_b, w_branch_c, w_out, ffn_w_in, ffn_w_out, final_norm_w):
    bsz, seq, d = x.shape
    depth = ada_w.shape[0]
    m = bsz * seq

    ada = _ada_call(c, ada_w, ada_b)
    ada3 = ada.reshape(depth * bsz * 6, 1, d)
    cos, sin_s = _rope_call(positions)

    x2 = x.reshape(m, d)
    v_first = None
    for l in range(depth):
        w_l = _permute_w_in(w_in[l]).astype(BF16)
        u2 = _proj_in_call(x2, norm_mix_w[l].reshape(1, d), ada3, cos, sin_s, w_l, l, bsz, seq)
        u3 = u2.reshape(bsz, seq, IN_COLS)

        o_a = _attn_call(u3, da_lambda[l], da_subln_w[l], l)
        o_b = _hgrn_call(u3, hg_lb, hg_norm_w[l], l)
        vres = None if l == 0 else (v_first, rw_v0[l - 1], rw_v1[l - 1], rw_v2[l - 1])
        emit_v = l == 0 and depth > 1
        res = _rw_mix_call(u3, rw_mu[l], rw_w0[l], rw_w2[l], rw_a0[l], rw_a2[l], rw_g2[l], rw_k_k[l],
                           rw_k_a[l], vres, rw_r_k[l], rw_gn_w[l], rw_gn_b[l], emit_v)
        if emit_v:
            o_c, v_first = res
        else:
            o_c = res

        x2 = _merge_call(u2, o_a.reshape(m, -1), o_b.reshape(m, -1), o_c.reshape(m, -1), x2, ada3,
                         w_branch_a[l].astype(BF16), w_branch_b[l].astype(BF16),
                         w_branch_c[l].astype(BF16), w_out[l].astype(BF16), l, bsz, seq)
        x2 = _ffn_call(x2, norm_ffn_w[l].reshape(1, d), ada3, ffn_w_in[l].astype(BF16),
                       ffn_w_out[l].astype(BF16), final_norm_w.reshape(1, d), l, bsz, seq,
                       final=(l == depth - 1))
    return x2.reshape(bsz, seq, d)
```

```python
import functools
import math

import jax
import jax.numpy as jnp
from jax import lax
from jax.experimental import pallas as pl
from jax.experimental.pallas import tpu as pltpu

F32 = jnp.float32
BF16 = jnp.bfloat16

D_MODEL = 1024
CHUNK = 64
ROPE_THETA = 10000.0
NORM_EPS = 1e-6

DA_HEADS = 4
DA_QK_DIM = 64
DA_V_DIM = 128
HG_HEADS = 4
HG_DIM = 128
RW_HEADS = 8
RW_DIM = 64
RW_WIDTH = 512
RW_GN_EPS = 64e-5
FFN_HIDDEN = 2816
IN_COLS = 8448

LANES = 128
ATTN_LOCKSTEP = 4
LOG2E = 1.4426950408889634
Q_SCALE = DA_QK_DIM ** -0.5 * LOG2E
VT_ROWS = DA_V_DIM + 16
SUB = 16
RW_PACK = 4
RW_GW = RW_PACK * RW_DIM
RW_GROUPS = RW_HEADS // RW_PACK

COL_GATES = 0
COL_HF = 3072
COL_HI = 3584
COL_HQ = 4096
COL_HG = 4608
COL_Q = 5120
COL_K = 5632
COL_V = 6144
COL_RR = 6656
COL_RK = 7168
COL_RV = 7680
COL_RL = 8192
RW_LORA_W = 256
HEADW = 512
PROJ_TN = 2816
GROUPS_PER_TILE = PROJ_TN // LANES
ROPE_Q0 = COL_Q // LANES
ROPE_K0 = COL_K // LANES

VMEM_LIMIT = 56 * 1024 * 1024


def _cp(sem):
    return pltpu.CompilerParams(dimension_semantics=sem, vmem_limit_bytes=VMEM_LIMIT)


def _sigmoid(x):
    return 1.0 / (1.0 + jnp.exp(-x))


def _softplus(x):
    return jnp.maximum(x, 0.0) + jnp.log(1.0 + jnp.exp(-jnp.abs(x)))


def _dot(a, b):
    return jnp.dot(a, b, preferred_element_type=F32)


def _dot_nt(a, b):
    return lax.dot_general(a, b, (((1,), (1,)), ((), ())), preferred_element_type=F32)


def _dot_exact_lhs(a01, x):
    n = x.shape[1]
    hi = x.astype(BF16)
    lo = (x - hi.astype(F32)).astype(BF16)
    res = _dot(a01, jnp.concatenate([hi, lo], axis=1))
    return res[:, 0:n] + res[:, n:2 * n]


def _head_sum(x):
    ones_g = _head_ones(RW_GW, RW_DIM)
    hi = x.astype(BF16)
    lo = (x - hi.astype(F32)).astype(BF16)
    m = x.shape[0]
    outs = []
    for grp in range(x.shape[1] // RW_GW):
        cols = slice(grp * RW_GW, (grp + 1) * RW_GW)
        res = _dot(jnp.concatenate([hi[:, cols], lo[:, cols]], axis=0), ones_g)
        outs.append(res[:m] + res[m:])
    return jnp.concatenate(outs, axis=1)


def _dot_hp(a, b):
    a_hi = a.astype(BF16)
    a_lo = (a - a_hi.astype(F32)).astype(BF16)
    b_hi = b.astype(BF16)
    b_lo = (b - b_hi.astype(F32)).astype(BF16)
    return _dot(a_hi, b_hi) + _dot(a_hi, b_lo) + _dot(a_lo, b_hi)


def _iota(shape, dim):
    return lax.broadcasted_iota(jnp.int32, shape, dim)


def _ada_body(c_ref, w_ref, b_ref, o_ref):
    c = c_ref[...]
    ca = c * _sigmoid(c)
    o_ref[...] = _dot_hp(ca, w_ref[...]) + b_ref[...]


def _ada_call(c, ada_w, ada_b):
    nl, d, n6 = ada_w.shape
    bsz = c.shape[0]
    tn = 1536
    return pl.pallas_call(
        _ada_body,
        grid=(nl, n6 // tn),
        in_specs=[
            pl.BlockSpec((bsz, d), lambda l, j: (0, 0)),
            pl.BlockSpec((None, d, tn), lambda l, j: (l, 0, j)),
            pl.BlockSpec((None, 1, tn), lambda l, j: (l, 0, j)),
        ],
        out_specs=pl.BlockSpec((None, bsz, tn), lambda l, j: (l, 0, j)),
        out_shape=jax.ShapeDtypeStruct((nl, bsz, n6), F32),
        compiler_params=_cp(("parallel", "parallel")),
        name="ada",
    )(c, ada_w, ada_b.reshape(nl, 1, n6))


def _rope_body(pos_ref, invf_ref, sgn_ref, cos_ref, sin_ref):
    ang = pos_ref[...].astype(F32) * invf_ref[...]
    cos_ref[...] = jnp.cos(ang)
    sin_ref[...] = jnp.sin(ang) * sgn_ref[...]


def _rope_call(positions):
    m = positions.size
    tm = min(2048, m)
    inv_freq = ROPE_THETA ** (-jnp.arange(0, DA_QK_DIM, 2, dtype=F32) / DA_QK_DIM)
    invf = jnp.tile(inv_freq, LANES // (DA_QK_DIM // 2)).reshape(1, LANES)
    sgn = jnp.where(jnp.arange(LANES) < LANES // 2, -1.0, 1.0).astype(F32).reshape(1, LANES)
    return pl.pallas_call(
        _rope_body,
        grid=(m // tm,),
        in_specs=[
            pl.BlockSpec((tm, 1), lambda i: (i, 0)),
            pl.BlockSpec((1, LANES), lambda i: (0, 0)),
            pl.BlockSpec((1, LANES), lambda i: (0, 0)),
        ],
        out_specs=[pl.BlockSpec((tm, LANES), lambda i: (i, 0))] * 2,
        out_shape=[jax.ShapeDtypeStruct((m, LANES), F32)] * 2,
        compiler_params=_cp(("parallel",)),
        name="rope_tables",
    )(positions.reshape(m, 1), invf, sgn)


def _norm_mod(x, nw, scale, shift):
    ms = jnp.mean(x * x, axis=-1, keepdims=True)
    y = x * lax.rsqrt(ms + NORM_EPS) * nw
    return y * (1.0 + scale) + shift


def _rope_kind(group):
    if ROPE_Q0 <= group < ROPE_Q0 + DA_HEADS:
        return "q"
    if ROPE_K0 <= group < ROPE_K0 + DA_HEADS:
        return "k"
    return None


_ROPE_TILES = sorted({g // GROUPS_PER_TILE for g in range(IN_COLS // LANES) if _rope_kind(g)})


def _proj_in_body(x_ref, nw_ref, shift_ref, scale_ref, cos_ref, sin_ref, w_ref, o_ref, h_ref):
    j = pl.program_id(1)

    @pl.when(j == 0)
    def _():
        h_ref[...] = _norm_mod(x_ref[...], nw_ref[...], scale_ref[...], shift_ref[...]).astype(BF16)

    for jt in _ROPE_TILES:
        @pl.when(j == jt)
        def _(jt=jt):
            acc = _dot(h_ref[...], w_ref[...])
            cos = cos_ref[...]
            sin_s = sin_ref[...]
            for g in range(GROUPS_PER_TILE):
                slab = acc[:, g * LANES:(g + 1) * LANES]
                kind = _rope_kind(jt * GROUPS_PER_TILE + g)
                if kind is not None:
                    slab = slab * cos + pltpu.roll(slab, LANES // 2, 1) * sin_s
                    if kind == "q":
                        slab = slab * Q_SCALE
                o_ref[:, g * LANES:(g + 1) * LANES] = slab.astype(o_ref.dtype)

    plain = j != _ROPE_TILES[0]
    for jt in _ROPE_TILES[1:]:
        plain = jnp.logical_and(plain, j != jt)

    @pl.when(plain)
    def _():
        o_ref[...] = _dot(h_ref[...], w_ref[...]).astype(o_ref.dtype)


def _proj_in_call(x2, nw, ada3, cos, sin_s, w_bf16, layer, bsz, seq):
    m, d = x2.shape
    n = w_bf16.shape[1]
    tm = min(1024, seq)
    per_b = seq // tm
    base = layer * bsz * 6

    def ada_spec(k):
        return pl.BlockSpec((None, 1, d), lambda i, j: (base + (i // per_b) * 6 + k, 0, 0))

    return pl.pallas_call(
        _proj_in_body,
        grid=(m // tm, n // PROJ_TN),
        in_specs=[
            pl.BlockSpec((tm, d), lambda i, j: (i, 0)),
            pl.BlockSpec((1, d), lambda i, j: (0, 0)),
            ada_spec(0),
            ada_spec(1),
            pl.BlockSpec((tm, LANES), lambda i, j: (i, 0)),
            pl.BlockSpec((tm, LANES), lambda i, j: (i, 0)),
            pl.BlockSpec((d, PROJ_TN), lambda i, j: (0, j)),
        ],
        out_specs=pl.BlockSpec((tm, PROJ_TN), lambda i, j: (i, j)),
        out_shape=jax.ShapeDtypeStruct((m, n), BF16),
        scratch_shapes=[pltpu.VMEM((tm, d), BF16)],
        compiler_params=_cp(("parallel", "arbitrary")),
        name="proj_in",
    )(x2, nw, ada3, ada3, cos, sin_s, w_bf16)


def _attn_body(lam_ref, q_ref, k_ref, v_ref, sw_ref, o_ref, vt_ref, *, tq, lam_init):
    i = pl.program_id(1)

    @pl.when(i == 0)
    def _():
        for h in range(DA_HEADS):
            vt_ref[h, 0:DA_V_DIM, :] = v_ref[:, h * LANES:(h + 1) * LANES].astype(F32).T.astype(BF16)
            vt_ref[h, DA_V_DIM:VT_ROWS, :] = jnp.ones((VT_ROWS - DA_V_DIM, vt_ref.shape[2]), BF16)

    lv = lam_ref[...]
    lam = (jnp.exp(jnp.sum(lv[0:1] * lv[1:2], axis=1, keepdims=True))
           - jnp.exp(jnp.sum(lv[2:3] * lv[3:4], axis=1, keepdims=True)) + lam_init)

    map0 = jnp.bitwise_and(_iota((1, LANES), 1), DA_QK_DIM - 1) < DA_QK_DIM // 2
    zero = jnp.zeros((), BF16)
    qs = []
    for h in range(DA_HEADS):
        q = q_ref[:, h * LANES:(h + 1) * LANES]
        q0 = jnp.where(map0, q, zero)
        q1 = jnp.where(map0, zero, q)
        qs.append(jnp.concatenate([q0, q1], axis=0))

    def mask_for(nk):
        kk = _iota((nk, 2 * tq), 0)
        qq = _iota((nk, 2 * tq), 1)
        q_chunk = jnp.right_shift(jnp.bitwise_and(qq, tq - 1), 6) + (nk - tq) // CHUNK
        return jnp.right_shift(kk, 6) <= q_chunk

    def step(start, nk, carry, allowed):
        out = [None] * DA_HEADS
        for g0 in range(0, DA_HEADS, ATTN_LOCKSTEP):
            grp = range(g0, g0 + ATTN_LOCKSTEP)
            ss = {h: _dot_nt(k_ref[pl.ds(start, nk), h * LANES:(h + 1) * LANES], qs[h]) for h in grp}
            ps, stats = {}, {}
            for h in grp:
                m_i, _ = carry[h]
                s = ss[h]
                if allowed is not None:
                    s = jnp.where(allowed, s, -jnp.inf)
                m_new = jnp.maximum(m_i, jnp.max(s, axis=0, keepdims=True))
                ps[h] = jnp.exp2(s - m_new).astype(BF16)
                stats[h] = (m_new, jnp.exp2(m_i - m_new))
            for h in grp:
                m_new, alpha = stats[h]
                pv = _dot(vt_ref[h, :, pl.ds(start, nk)], ps[h])
                out[h] = (m_new, alpha * carry[h][1] + pv)
        return tuple(out)

    init = tuple((jnp.full((1, 2 * tq), -jnp.inf, F32), jnp.zeros((VT_ROWS, 2 * tq), F32))
                 for _ in range(DA_HEADS))
    carry = lax.fori_loop(
        0, i // 2, lambda j, c: step(pl.multiple_of(j * (2 * tq), 2 * tq), 2 * tq, c, None), init)
    carry = lax.cond(
        i % 2 == 1,
        lambda c: step(pl.multiple_of((i - 1) * tq, tq), 2 * tq, c, mask_for(2 * tq)),
        lambda c: step(pl.multiple_of(i * tq, tq), tq, c, mask_for(tq)),
        carry)

    for h in range(DA_HEADS):
        acc = carry[h][1]
        o_t = acc[:DA_V_DIM] / acc[DA_V_DIM:DA_V_DIM + 1]
        o_t = o_t[:, :tq] - lam * o_t[:, tq:]
        ms = jnp.mean(o_t * o_t, axis=0, keepdims=True)
        o = (o_t * lax.rsqrt(ms + NORM_EPS)).T * sw_ref[...] * (1.0 - lam_init)
        o_ref[:, h * LANES:(h + 1) * LANES] = o.astype(o_ref.dtype)


def _attn_call(u3, lam_vecs, subln_w, layer):
    bsz, seq, _ = u3.shape
    tq = min(256, seq)
    lam_init = 0.8 - 0.6 * math.exp(-0.3 * layer)
    return pl.pallas_call(
        functools.partial(_attn_body, tq=tq, lam_init=lam_init),
        grid=(bsz, seq // tq),
        in_specs=[
            pl.BlockSpec((4, DA_QK_DIM), lambda b, i: (0, 0)),
            pl.BlockSpec((None, tq, HEADW), lambda b, i: (b, i, COL_Q // HEADW)),
            pl.BlockSpec((None, seq, HEADW), lambda b, i: (b, 0, COL_K // HEADW)),
            pl.BlockSpec((None, seq, HEADW), lambda b, i: (b, 0, COL_V // HEADW)),
            pl.BlockSpec((1, LANES), lambda b, i: (0, 0)),
        ],
        out_specs=pl.BlockSpec((None, tq, HEADW), lambda b, i: (b, i, 0)),
        out_shape=jax.ShapeDtypeStruct((bsz, seq, HEADW), BF16),
        scratch_shapes=[pltpu.VMEM((DA_HEADS, VT_ROWS, seq), BF16)],
        compiler_params=_cp(("parallel", "arbitrary")),
        name="diff_attn",
    )(lam_vecs, u3, u3, u3, subln_w.reshape(1, LANES))


def _hgrn_body(lbp_ref, z_ref, i_ref, q_ref, g_ref, nw_ref, o_ref, *, layer, nchunks):
    lbp = lbp_ref[...]
    e = jnp.exp(lbp - jnp.max(lbp, axis=0, keepdims=True))
    sm = e / jnp.sum(e, axis=0, keepdims=True)
    cs = sm[0:1]
    for t in range(1, layer + 1):
        cs = cs + sm[t:t + 1]
    lb = cs - sm[0:1]
    log_lb = jnp.log(lb)
    log1m = jnp.log1p(-lb)

    rr = _iota((CHUNK, CHUNK), 0)
    cc = _iota((CHUNK, CHUNK), 1)
    sub_start = jnp.bitwise_and(rr, ~(SUB - 1))
    cs_mat = jnp.concatenate([(cc <= rr).astype(BF16), (cc < sub_start).astype(BF16)], axis=0)
    diag_ok = jnp.logical_and(cc <= rr, jnp.right_shift(cc, 4) == jnp.right_shift(rr, 4))
    nsub = CHUNK // SUB
    heads = range(HG_HEADS)

    def hcols(x, h):
        return x[:, h * HG_DIM:(h + 1) * HG_DIM]

    def chunk(c, states):
        sl = pl.ds(pl.multiple_of(c * CHUNK, CHUNK), CHUNK)
        z = z_ref[sl, :].astype(F32)
        y = log1m - _softplus(-z)
        mx = jnp.maximum(log_lb, y)
        lf = mx + jnp.log(1.0 + jnp.exp(-jnp.abs(log_lb - y)))
        key = (1.0 - lb) * _sigmoid(-z)
        qc = q_ref[sl, :].astype(F32)
        vc = i_ref[sl, :]
        vt = vc.astype(F32).T.astype(BF16)

        tot = _dot_exact_lhs(cs_mat, lf * LOG2E)
        b = tot[:CHUNK]
        d = b - tot[CHUNK:]
        b_last = b[CHUNK - 1:CHUNK, :]

        q_d = (qc * jnp.exp2(d)).astype(BF16)
        k_d = (key * jnp.exp2(-d)).astype(BF16)
        q_js, k_js = [], []
        for jb in range(nsub - 1):
            lo, hi = jb * SUB, (jb + 1) * SUB
            e_j = b[hi - 1:hi, :]
            q_j = (qc[hi:] * jnp.exp2(b[hi:] - e_j)).astype(BF16)
            k_j = (key[lo:hi] * jnp.exp2(e_j - b[lo:hi])).astype(BF16)
            q_js.append(jnp.concatenate([jnp.zeros((hi, HEADW), BF16), q_j], axis=0))
            pieces = [k_j, jnp.zeros((CHUNK - hi, HEADW), BF16)]
            if lo:
                pieces.insert(0, jnp.zeros((lo, HEADW), BF16))
            k_js.append(jnp.concatenate(pieces, axis=0))
        q_in = (qc * jnp.exp2(b)).astype(BF16)
        k_out = (key * jnp.exp2(b_last - b)).astype(BF16)
        decay = jnp.exp2(b_last)

        s_diag = [_dot_nt(hcols(q_d, h), hcols(k_d, h)) for h in heads]
        s_off = [_dot_nt(jnp.concatenate([hcols(x, h) for x in q_js], axis=1),
                         jnp.concatenate([hcols(x, h) for x in k_js], axis=1)) for h in heads]
        scores = [(jnp.where(diag_ok, s_diag[h], 0.0) + s_off[h]).astype(BF16) for h in heads]
        o_intra = [_dot(scores[h], hcols(vc, h)) for h in heads]
        o_inter = [_dot_nt(hcols(q_in, h), states[h].astype(BF16)) for h in heads]
        upd = [_dot(vt[h * HG_DIM:(h + 1) * HG_DIM, :], hcols(k_out, h)) for h in heads]
        new_states = tuple(states[h] * hcols(decay, h) + upd[h] for h in heads)

        outs = []
        for h in heads:
            o = o_intra[h] + o_inter[h]
            ms = jnp.mean(o * o, axis=-1, keepdims=True)
            outs.append(o * lax.rsqrt(ms + NORM_EPS))
        g = g_ref[sl, :].astype(F32)
        o = jnp.concatenate(outs, axis=1) * nw_ref[...] * (g * _sigmoid(g))
        o_ref[sl, :] = o.astype(o_ref.dtype)
        return new_states

    init = tuple(jnp.zeros((HG_DIM, HG_DIM), F32) for _ in heads)
    lax.fori_loop(0, nchunks, chunk, init, unroll=8)


def _hgrn_call(u3, hg_lb, norm_w, layer):
    bsz, seq, _ = u3.shape
    nl = hg_lb.shape[0]

    def col(base):
        return lambda b: (b, 0, base // HEADW)

    blk = (None, seq, HEADW)
    return pl.pallas_call(
        functools.partial(_hgrn_body, layer=layer, nchunks=seq // CHUNK),
        grid=(bsz,),
        in_specs=[
            pl.BlockSpec((nl, HEADW), lambda b: (0, 0)),
            pl.BlockSpec(blk, col(COL_HF)),
            pl.BlockSpec(blk, col(COL_HI)),
            pl.BlockSpec(blk, col(COL_HQ)),
            pl.BlockSpec(blk, col(COL_HG)),
            pl.BlockSpec((1, HEADW), lambda b: (0, 0)),
        ],
        out_specs=pl.BlockSpec(blk, lambda b: (b, 0, 0)),
        out_shape=jax.ShapeDtypeStruct((bsz, seq, HEADW), BF16),
        compiler_params=_cp(("parallel",)),
        name="hgrn2",
    )(hg_lb, u3, u3, u3, u3, jnp.tile(norm_w, HG_HEADS).reshape(1, HEADW))


def _head_ones(n, width):
    r = _iota((n, n), 0) // width
    c = _iota((n, n), 1) // width
    return (r == c).astype(BF16)


def _rw_prep(cur, prev, prm, vres, has_prev):
    mu_ref, w0_ref, w2_ref, a0_ref, a2_ref, g2_ref, kk_ref, ka_ref = prm
    ts = cur[0].shape[0]
    row = _iota((ts, 1), 0)

    def shifted(idx, lo, hi):
        u = cur[idx][...].astype(F32)
        if prev is None:
            prev_row = jnp.zeros((1, u.shape[1]), F32)
        else:
            prev_row = prev[idx][SUB - 1:SUB, :].astype(F32) * has_prev.astype(F32)
        u_prev = jnp.where(row == 0, prev_row, pltpu.roll(u, 1, 0))
        return u + (u_prev - u) * mu_ref[:, lo:hi]

    r = shifted(0, 0, 512)
    k = shifted(1, 512, 1024)
    v = shifted(2, 1024, 1536)
    lora = shifted(3, 1536, 1792)
    w_lo = lora[:, 0:64]
    a_lo = lora[:, 64:128]
    g_lo = lora[:, 128:256]

    def lora_dot(act, w_ref):
        return _dot(act.astype(BF16), w_ref[...].astype(BF16))

    wpre = w0_ref[...] + lora_dot(jnp.tanh(w_lo), w2_ref)
    w = -_softplus(-wpre) - 0.5
    lw = -jnp.exp(w)
    a = _sigmoid(a0_ref[...] + lora_dot(a_lo, a2_ref))
    g = lora_dot(_sigmoid(g_lo), g2_ref)
    if vres is not None:
        vf_ref, v0_ref, v1_ref, v2_ref = vres
        mix = _sigmoid(v0_ref[...] + lora_dot(lora_dot(v, v1_ref), v2_ref))
        v = v + (vf_ref[...].astype(F32) - v) * mix

    kk = k * kk_ref[...]
    ssq = _head_sum(kk * kk)
    kkn = kk * lax.rsqrt(jnp.maximum(ssq, 1e-24))
    k = k * (1.0 + (a - 1.0) * ka_ref[...])
    return r, k, v, lw, kkn, kkn * a, g


def _rw_scan(r_ref, k_ref, v_ref, lw_ref, kk_ref, bk_ref, g_ref, rk_ref, gw_ref, gb_ref,
             o_ref, state_ref, oacc_ref, nsub):
    t = CHUNK
    gw = RW_GW
    tc = nsub * t
    units = [(s, grp) for s in range(nsub) for grp in range(RW_GROUPS)]

    rr = _iota((tc, tc), 0)
    cc = _iota((tc, tc), 1)
    tril_bd = jnp.logical_and(cc <= rr, (cc // t) == (rr // t)).astype(BF16)
    rr4 = _iota((t, gw), 0)
    cj = jnp.bitwise_and(_iota((t, gw), 1), t - 1)
    strict = cj < rr4
    incl = cj <= rr4
    eye_c = (cj == rr4).astype(F32)
    bd_mask = (_iota((gw, gw), 0) // t) == (_iota((gw, gw), 1) // RW_DIM)
    zero_b = jnp.zeros((), BF16)

    left_head = _iota((t, LANES), 1) < RW_DIM
    zeros_tile = jnp.zeros((t, LANES), BF16)

    def bd(xc):
        xb = xc.astype(BF16)
        rows = []
        for h in range(RW_PACK):
            tile = xb[:, (h // 2) * LANES:(h // 2 + 1) * LANES]
            kept = jnp.where(left_head, tile, zero_b) if h % 2 == 0 else jnp.where(left_head, zero_b, tile)
            rows.append(jnp.concatenate([kept, zeros_tile] if h < 2 else [zeros_tile, kept], axis=1))
        return jnp.concatenate(rows, axis=0)

    def cut(x, u):
        s, grp = u
        return x[s * t:(s + 1) * t, grp * gw:(grp + 1) * gw]

    r = r_ref[...].astype(F32)
    k = k_ref[...].astype(F32)
    v = v_ref[...].astype(F32)
    lw = lw_ref[...] * LOG2E
    kk = kk_ref[...].astype(F32)
    bk = bk_ref[...].astype(F32)
    c = _dot_exact_lhs(tril_bd, lw)
    e_neg = jnp.exp2(-c)
    a_t = -kk * jnp.exp2(c - lw)
    b_t = bk * e_neg
    k_t = k * e_neg
    r_t = r * jnp.exp2(c)
    c_last = jnp.concatenate(
        [jnp.broadcast_to(c[(s + 1) * t - 1:(s + 1) * t, :], (t, RW_WIDTH)) for s in range(nsub)], axis=0)
    decay_out = jnp.exp2(c_last - c)
    b_g = bk * decay_out
    k_g = k * decay_out
    gamma = jnp.exp2(c_last)

    ar = {u: jnp.concatenate([cut(a_t, u), cut(r_t, u)], axis=0).astype(BF16) for u in units}
    x_bk = {u: _dot_nt(ar[u], jnp.concatenate([bd(cut(b_t, u)), bd(cut(k_t, u))], axis=0)) for u in units}
    l_c = {u: jnp.where(strict, x_bk[u][:t, :gw], 0.0) for u in units}
    m_c = {u: jnp.where(strict, x_bk[u][:t, gw:], 0.0).astype(BF16) for u in units}
    rb = {u: jnp.where(incl, x_bk[u][t:, :gw], 0.0).astype(BF16) for u in units}
    rkm = {u: jnp.where(incl, x_bk[u][t:, gw:], 0.0).astype(BF16) for u in units}
    v_bd = {u: bd(cut(v, u)) for u in units}

    p = {u: _dot(l_c[u].astype(BF16), bd(l_c[u])) for u in units}
    s_inv = {u: eye_c + l_c[u] for u in units}
    mo = {u: _dot(jnp.concatenate([m_c[u], rkm[u]], axis=0), v_bd[u]) for u in units}
    mv = {u: mo[u][:t] for u in units}
    o_kv = {u: mo[u][t:] for u in units}
    for lvl in range(1, 6):
        last = lvl == 5
        nxt_p, nxt_s = {}, {}
        for u in units:
            if last:
                nxt_s[u] = s_inv[u] + _dot(s_inv[u].astype(BF16), bd(p[u]))
            else:
                res = _dot(jnp.concatenate([s_inv[u], p[u]], axis=0).astype(BF16), bd(p[u]))
                nxt_s[u] = s_inv[u] + res[:t]
                nxt_p[u] = res[t:]
        p, s_inv = nxt_p, nxt_s

    rbs = {u: _dot(rb[u], bd(s_inv[u])) for u in units}
    wu = {u: _dot(jnp.concatenate([s_inv[u], rbs[u]], axis=0).astype(BF16),
                  jnp.concatenate([bd(cut(a_t, u)), bd(mv[u])], axis=1)) for u in units}
    w_t = {u: wu[u][:t, :gw] for u in units}
    u_t = {u: wu[u][:t, gw:] for u in units}
    r_hat = {u: (cut(r_t, u) + wu[u][t:, :gw]).astype(BF16) for u in units}
    o_hat = {u: wu[u][t:, gw:] + o_kv[u] for u in units}
    g_mat = {u: jnp.where(bd_mask, _dot(w_t[u].T.astype(BF16), cut(b_g, u).astype(BF16)), 0.0).astype(BF16)
             for u in units}
    c_mat = {}
    for u in units:
        uv_t = jnp.concatenate([u_t[u], cut(v, u)], axis=0).T.astype(BF16)
        bkg = jnp.concatenate([cut(b_g, u), cut(k_g, u)], axis=0).astype(BF16)
        full = jnp.where(bd_mask, _dot(uv_t, bkg), 0.0)
        c_mat[u] = full[0:t] + full[t:2 * t] + full[2 * t:3 * t] + full[3 * t:4 * t]

    for s in range(nsub):
        for grp in range(RW_GROUPS):
            u = (s, grp)
            st = state_ref[grp]
            st_b = st.astype(BF16)
            oacc_ref[s * t:(s + 1) * t, grp * gw:(grp + 1) * gw] = _dot_nt(r_hat[u], bd(st_b)) + o_hat[u]
            g_row = gamma[(s + 1) * t - 1:(s + 1) * t, grp * gw:(grp + 1) * gw]
            state_ref[grp] = st * g_row + _dot(st_b, g_mat[u]) + c_mat[u]

    o = oacc_ref[...]
    mean = _head_sum(o) * (1.0 / RW_DIM)
    dlt = o - mean
    var = _head_sum(dlt * dlt) * (1.0 / RW_DIM)
    on = dlt * lax.rsqrt(var + RW_GN_EPS) * gw_ref[...] + gb_ref[...]
    bonus = _head_sum(r * k * rk_ref[...])
    o_ref[...] = ((on + bonus * v) * g_ref[...]).astype(o_ref.dtype)


RW_SLOT_DTYPES = (BF16, BF16, BF16, F32, BF16, BF16, BF16)


def _rw_mix_body(*refs, has_vres, emit_v, nsub):
    nxt, nxt_prev, first, prm = refs[0:4], refs[4:8], refs[8:12], refs[12:20]
    pos = 20
    vres_nxt = vres_first = None
    if has_vres:
        vf_nxt, vf_first, v0, v1, v2 = refs[pos:pos + 5]
        vres_nxt, vres_first = (vf_nxt, v0, v1, v2), (vf_first, v0, v1, v2)
        pos += 5
    rk_ref, gw_ref, gb_ref = refs[pos:pos + 3]
    pos += 3
    o_ref = refs[pos]
    pos += 1
    v_out = None
    if emit_v:
        v_out = refs[pos]
        pos += 1
    state_ref, oacc_ref = refs[pos:pos + 2]
    slots = refs[pos + 2:pos + 9]

    j = pl.program_id(1)
    last = pl.num_programs(1) - 1

    def put(slot, vals):
        for ref, val in zip(slots, vals):
            ref[slot] = val.astype(ref.dtype)

    @pl.when(j == 0)
    def _():
        state_ref[...] = jnp.zeros_like(state_ref)
        put(0, _rw_prep(first, None, prm, vres_first, None))

    put((j + 1) % 2, _rw_prep(nxt, nxt_prev, prm, vres_nxt, jnp.minimum(j + 1, last) > 0))

    cur = [ref.at[j % 2] for ref in slots]
    _rw_scan(*cur, rk_ref, gw_ref, gb_ref, o_ref, state_ref, oacc_ref, nsub)
    if emit_v:
        v_out[...] = cur[2][...]


def _rw_mix_call(u3, mu, w0, w2, a0, a2, g2, k_k, k_a, vres, r_k, gn_w, gn_b, emit_v):
    bsz, seq, _ = u3.shape
    tc = min(256, seq)
    n = seq // tc
    w = RW_WIDTH
    has_vres = vres is not None

    def full(arr):
        return pl.BlockSpec(arr.shape, lambda b, j: (0,) * arr.ndim)

    def nxt(col, width):
        return pl.BlockSpec((None, tc, width), lambda b, j: (b, jnp.minimum(j + 1, n - 1), col // width))

    def nxt_prev(col, width):
        return pl.BlockSpec(
            (None, SUB, width),
            lambda b, j: (b, jnp.maximum(jnp.minimum(j + 1, n - 1) * (tc // SUB) - 1, 0), col // width))

    def first(col, width):
        return pl.BlockSpec((None, tc, width), lambda b, j: (b, 0, col // width))

    cols = [(COL_RR, w), (COL_RK, w), (COL_RV, w), (COL_RL, RW_LORA_W)]
    params = [mu.reshape(1, -1), w0.reshape(1, w), w2, a0.reshape(1, w), a2, g2,
              k_k.reshape(1, w), k_a.reshape(1, w)]
    in_specs = [f(c, wd) for f in (nxt, nxt_prev, first) for c, wd in cols] + [full(p) for p in params]
    args = [u3] * 12 + params
    if has_vres:
        v_first, v0, v1, v2 = vres
        extra = [v0.reshape(1, w), v1, v2]
        in_specs += [pl.BlockSpec((None, tc, w), lambda b, j: (b, jnp.minimum(j + 1, n - 1), 0)),
                     pl.BlockSpec((None, tc, w), lambda b, j: (b, 0, 0))] + [full(p) for p in extra]
        args += [v_first, v_first] + extra
    tail = [r_k.reshape(1, w), gn_w.reshape(1, w), gn_b.reshape(1, w)]
    in_specs += [full(p) for p in tail]
    args += tail

    out_spec = pl.BlockSpec((None, tc, w), lambda b, j: (b, j, 0))
    out_shape = jax.ShapeDtypeStruct((bsz, seq, w), BF16)
    scratch = [pltpu.VMEM((RW_GROUPS, RW_DIM, RW_GW), F32), pltpu.VMEM((tc, w), F32)]
    scratch += [pltpu.VMEM((2, tc, w), dt) for dt in RW_SLOT_DTYPES]
    return pl.pallas_call(
        functools.partial(_rw_mix_body, has_vres=has_vres, emit_v=emit_v, nsub=tc // CHUNK),
        grid=(bsz, n),
        in_specs=in_specs,
        out_specs=[out_spec, out_spec] if emit_v else out_spec,
        out_shape=[out_shape, out_shape] if emit_v else out_shape,
        scratch_shapes=scratch,
        compiler_params=_cp(("parallel", "arbitrary")),
        name="rwkv7",
    )(*args)


def _merge_body(ga_ref, gb_ref, gc_ref, oa_ref, ob_ref, oc_ref, x_ref, gate_ref,
                wa_ref, wb_ref, wc_ref, wo_ref, o_ref):
    merged = (_sigmoid(ga_ref[...].astype(F32)) * _dot(oa_ref[...], wa_ref[...])
              + _sigmoid(gb_ref[...].astype(F32)) * _dot(ob_ref[...], wb_ref[...])
              + _sigmoid(gc_ref[...].astype(F32)) * _dot(oc_ref[...], wc_ref[...]))
    mix = _dot(merged.astype(BF16), wo_ref[...])
    o_ref[...] = x_ref[...] + gate_ref[...] * mix


def _merge_call(u2, o_a, o_b, o_c, x2, ada3, wa, wb, wc, wo, layer, bsz, seq):
    m, d = x2.shape
    tm = min(512, seq)
    per_b = seq // tm
    base = layer * bsz * 6
    wdt = o_a.shape[1]

    def const(arr):
        return pl.BlockSpec(arr.shape, lambda i: (0, 0))

    return pl.pallas_call(
        _merge_body,
        grid=(m // tm,),
        in_specs=[
            pl.BlockSpec((tm, d), lambda i: (i, 0)),
            pl.BlockSpec((tm, d), lambda i: (i, 1)),
            pl.BlockSpec((tm, d), lambda i: (i, 2)),
            pl.BlockSpec((tm, wdt), lambda i: (i, 0)),
            pl.BlockSpec((tm, wdt), lambda i: (i, 0)),
            pl.BlockSpec((tm, wdt), lambda i: (i, 0)),
            pl.BlockSpec((tm, d), lambda i: (i, 0)),
            pl.BlockSpec((None, 1, d), lambda i: (base + (i // per_b) * 6 + 2, 0, 0)),
            const(wa), const(wb), const(wc), const(wo),
        ],
        out_specs=pl.BlockSpec((tm, d), lambda i: (i, 0)),
        out_shape=jax.ShapeDtypeStruct((m, d), F32),
        compiler_params=_cp(("parallel",)),
        name="merge_out",
    )(u2, u2, u2, o_a, o_b, o_c, x2, ada3, wa, wb, wc, wo)


def _ffn_body(x_ref, nw_ref, shift_ref, scale_ref, gate_ref, wi_ref, wo_ref, fn_ref, o_ref, *, final):
    x = x_ref[...]
    h = _norm_mod(x, nw_ref[...], scale_ref[...], shift_ref[...]).astype(BF16)
    half = FFN_HIDDEN // 2
    acc = jnp.zeros(x.shape, F32)
    for c in range(2):
        gh = _dot(h, wi_ref[:, c * half:(c + 1) * half])
        uh = _dot(h, wi_ref[:, FFN_HIDDEN + c * half:FFN_HIDDEN + (c + 1) * half])
        act = (gh * _sigmoid(gh) * uh).astype(BF16)
        acc = acc + _dot(act, wo_ref[c * half:(c + 1) * half, :])
    y = x + gate_ref[...] * acc
    if final:
        ms = jnp.mean(y * y, axis=-1, keepdims=True)
        y = y * lax.rsqrt(ms + NORM_EPS) * fn_ref[...]
    o_ref[...] = y


def _ffn_call(x2, nw, ada3, wi, wo, final_w, layer, bsz, seq, final):
    m, d = x2.shape
    tm = min(512, seq)
    per_b = seq // tm
    base = layer * bsz * 6

    def ada_spec(k):
        return pl.BlockSpec((None, 1, d), lambda i: (base + (i // per_b) * 6 + k, 0, 0))

    def const(arr):
        return pl.BlockSpec(arr.shape, lambda i: (0, 0), pipeline_mode=pl.Buffered(1))

    return pl.pallas_call(
        functools.partial(_ffn_body, final=final),
        grid=(m // tm,),
        in_specs=[
            pl.BlockSpec((tm, d), lambda i: (i, 0)),
            pl.BlockSpec((1, d), lambda i: (0, 0)),
            ada_spec(3), ada_spec(4), ada_spec(5),
            const(wi), const(wo),
            pl.BlockSpec((1, d), lambda i: (0, 0)),
        ],
        out_specs=pl.BlockSpec((tm, d), lambda i: (i, 0)),
        out_shape=jax.ShapeDtypeStruct((m, d), F32),
        compiler_params=_cp(("parallel",)),
        name="ffn_final" if final else "ffn",
    )(x2, nw, ada3, ada3, ada3, wi, wo, final_w)


def _permute_w_in(w):
    def qk_lanes(t):
        t = t.reshape(t.shape[0], DA_HEADS, 2, 2, DA_QK_DIM // 2)
        return jnp.swapaxes(t, 2, 3).reshape(t.shape[0], DA_HEADS * 2 * DA_QK_DIM)

    q, k, v = qk_lanes(w[:, 0:512]), qk_lanes(w[:, 512:1024]), w[:, 1024:1536]
    hgrn = w[:, 1536:3584]
    rw = w[:, 3584:5376]
    gates = w[:, 5376:8448]
    return jnp.concatenate([gates, hgrn, q, k, v, rw], axis=1)


def kernel(x, c, positions, ada_w, ada_b, norm_mix_w, norm_ffn_w, w_in, da_lambda, da_subln_w, hg_lb, hg_norm_w, rw_mu, rw_w0, rw_w2, rw_a0, rw_a2, rw_g2, rw_k_k, rw_k_a, rw_r_k, rw_gn_w, rw_gn_b, rw_v0, rw_v1, rw_v2, w_branch_a, w_branch_b, w_branch_c, w_out, ffn_w_in, ffn_w_out, final_norm_w):
    bsz, seq, d = x.shape
    depth = ada_w.shape[0]
    m = bsz * seq

    ada = _ada_call(c, ada_w, ada_b)
    ada3 = ada.reshape(depth * bsz * 6, 1, d)
    cos, sin_s = _rope_call(positions)

    x2 = x.reshape(m, d)
    v_first = None
    for l in range(depth):
        w_l = _permute_w_in(w_in[l]).astype(BF16)
        u2 = _proj_in_call(x2, norm_mix_w[l].reshape(1, d), ada3, cos, sin_s, w_l, l, bsz, seq)
        u3 = u2.reshape(bsz, seq, IN_COLS)

        o_a = _attn_call(u3, da_lambda[l], da_subln_w[l], l)
        o_b = _hgrn_call(u3, hg_lb, hg_norm_w[l], l)
        vres = None if l == 0 else (v_first, rw_v0[l - 1], rw_v1[l - 1], rw_v2[l - 1])
        emit_v = l == 0 and depth > 1
        res = _rw_mix_call(u3, rw_mu[l], rw_w0[l], rw_w2[l], rw_a0[l], rw_a2[l], rw_g2[l], rw_k_k[l],
                           rw_k_a[l], vres, rw_r_k[l], rw_gn_w[l], rw_gn_b[l], emit_v)
        if emit_v:
            o_c, v_first = res
        else:
            o_c = res

        x2 = _merge_call(u2, o_a.reshape(m, -1), o_b.reshape(m, -1), o_c.reshape(m, -1), x2, ada3,
                         w_branch_a[l].astype(BF16), w_branch_b[l].astype(BF16),
                         w_branch_c[l].astype(BF16), w_out[l].astype(BF16), l, bsz, seq)
        x2 = _ffn_call(x2, norm_ffn_w[l].reshape(1, d), ada3, ffn_w_in[l].astype(BF16),
                       ffn_w_out[l].astype(BF16), final_norm_w.reshape(1, d), l, bsz, seq,
                       final=(l == depth - 1))
    return x2.reshape(bsz, seq, d)
```

```python
import functools
import math

import jax
import jax.numpy as jnp
from jax import lax
from jax.experimental import pallas as pl
from jax.experimental.pallas import tpu as pltpu

F32 = jnp.float32
BF16 = jnp.bfloat16

D_MODEL = 1024
CHUNK = 64
ROPE_THETA = 10000.0
NORM_EPS = 1e-6

DA_HEADS = 4
DA_QK_DIM = 64
DA_V_DIM = 128
HG_HEADS = 4
HG_DIM = 128
RW_HEADS = 8
RW_DIM = 64
RW_WIDTH = 512
RW_GN_EPS = 64e-5
FFN_HIDDEN = 2816
FFN_CHUNKS = (768, 768, 768, 512)
IN_COLS = 8448

LANES = 128
ATTN_LOCKSTEP = 4
LOG2E = 1.4426950408889634
Q_SCALE = DA_QK_DIM ** -0.5 * LOG2E
VT_ROWS = DA_V_DIM + 16
SUB = 16
RW_PACK = 4
RW_GW = RW_PACK * RW_DIM
RW_GROUPS = RW_HEADS // RW_PACK

COL_GATES = 0
COL_HF = 3072
COL_HI = 3584
COL_HQ = 4096
COL_HG = 4608
COL_Q = 5120
COL_K = 5632
COL_V = 6144
COL_RR = 6656
COL_RK = 7168
COL_RV = 7680
COL_RL = 8192
RW_LORA_W = 256
HEADW = 512
PROJ_TN = 2816
GROUPS_PER_TILE = PROJ_TN // LANES
ROPE_Q0 = COL_Q // LANES
ROPE_K0 = COL_K // LANES

VMEM_LIMIT = 56 * 1024 * 1024


def _cp(sem):
    return pltpu.CompilerParams(dimension_semantics=sem, vmem_limit_bytes=VMEM_LIMIT)


def _sigmoid(x):
    return 1.0 / (1.0 + jnp.exp(-x))


def _softplus(x):
    return jnp.maximum(x, 0.0) + jnp.log(1.0 + jnp.exp(-jnp.abs(x)))


def _dot(a, b):
    return jnp.dot(a, b, preferred_element_type=F32)


def _dot_nt(a, b):
    return lax.dot_general(a, b, (((1,), (1,)), ((), ())), preferred_element_type=F32)


def _dot_exact_lhs(a01, x):
    n = x.shape[1]
    hi = x.astype(BF16)
    lo = (x - hi.astype(F32)).astype(BF16)
    res = _dot(a01, jnp.concatenate([hi, lo], axis=1))
    return res[:, 0:n] + res[:, n:2 * n]


def _head_sum(x):
    ones_g = _head_ones(RW_GW, RW_DIM)
    hi = x.astype(BF16)
    lo = (x - hi.astype(F32)).astype(BF16)
    m = x.shape[0]
    outs = []
    for grp in range(x.shape[1] // RW_GW):
        cols = slice(grp * RW_GW, (grp + 1) * RW_GW)
        res = _dot(jnp.concatenate([hi[:, cols], lo[:, cols]], axis=0), ones_g)
        outs.append(res[:m] + res[m:])
    return jnp.concatenate(outs, axis=1)


def _dot_hp(a, b):
    a_hi = a.astype(BF16)
    a_lo = (a - a_hi.astype(F32)).astype(BF16)
    b_hi = b.astype(BF16)
    b_lo = (b - b_hi.astype(F32)).astype(BF16)
    return _dot(a_hi, b_hi) + _dot(a_hi, b_lo) + _dot(a_lo, b_hi)


def _iota(shape, dim):
    return lax.broadcasted_iota(jnp.int32, shape, dim)


def _ada_body(c_ref, w_ref, b_ref, o_ref):
    c = c_ref[...]
    ca = c * _sigmoid(c)
    o_ref[...] = _dot_hp(ca, w_ref[...]) + b_ref[...]


def _ada_call(c, ada_w, ada_b):
    nl, d, n6 = ada_w.shape
    bsz = c.shape[0]
    tn = 1536
    return pl.pallas_call(
        _ada_body,
        grid=(nl, n6 // tn),
        in_specs=[
            pl.BlockSpec((bsz, d), lambda l, j: (0, 0)),
            pl.BlockSpec((None, d, tn), lambda l, j: (l, 0, j)),
            pl.BlockSpec((None, 1, tn), lambda l, j: (l, 0, j)),
        ],
        out_specs=pl.BlockSpec((None, bsz, tn), lambda l, j: (l, 0, j)),
        out_shape=jax.ShapeDtypeStruct((nl, bsz, n6), F32),
        compiler_params=_cp(("parallel", "parallel")),
        name="ada",
    )(c, ada_w, ada_b.reshape(nl, 1, n6))


def _rope_body(pos_ref, invf_ref, sgn_ref, cos_ref, sin_ref):
    ang = pos_ref[...].astype(F32) * invf_ref[...]
    cos_ref[...] = jnp.cos(ang)
    sin_ref[...] = jnp.sin(ang) * sgn_ref[...]


def _rope_call(positions):
    m = positions.size
    tm = min(2048, m)
    inv_freq = ROPE_THETA ** (-jnp.arange(0, DA_QK_DIM, 2, dtype=F32) / DA_QK_DIM)
    invf = jnp.tile(inv_freq, LANES // (DA_QK_DIM // 2)).reshape(1, LANES)
    sgn = jnp.where(jnp.arange(LANES) < LANES // 2, -1.0, 1.0).astype(F32).reshape(1, LANES)
    return pl.pallas_call(
        _rope_body,
        grid=(m // tm,),
        in_specs=[
            pl.BlockSpec((tm, 1), lambda i: (i, 0)),
            pl.BlockSpec((1, LANES), lambda i: (0, 0)),
            pl.BlockSpec((1, LANES), lambda i: (0, 0)),
        ],
        out_specs=[pl.BlockSpec((tm, LANES), lambda i: (i, 0))] * 2,
        out_shape=[jax.ShapeDtypeStruct((m, LANES), F32)] * 2,
        compiler_params=_cp(("parallel",)),
        name="rope_tables",
    )(positions.reshape(m, 1), invf, sgn)


def _norm_mod(x, nw, scale, shift):
    ms = jnp.mean(x * x, axis=-1, keepdims=True)
    y = x * lax.rsqrt(ms + NORM_EPS) * nw
    return y * (1.0 + scale) + shift


def _rope_kind(group):
    if ROPE_Q0 <= group < ROPE_Q0 + DA_HEADS:
        return "q"
    if ROPE_K0 <= group < ROPE_K0 + DA_HEADS:
        return "k"
    return None


_ROPE_TILES = sorted({g // GROUPS_PER_TILE for g in range(IN_COLS // LANES) if _rope_kind(g)})


def _proj_in_body(x_ref, nw_ref, shift_ref, scale_ref, cos_ref, sin_ref, w_ref, o_ref, h_ref):
    j = pl.program_id(1)

    @pl.when(j == 0)
    def _():
        h_ref[...] = _norm_mod(x_ref[...], nw_ref[...], scale_ref[...], shift_ref[...]).astype(BF16)

    for jt in _ROPE_TILES:
        @pl.when(j == jt)
        def _(jt=jt):
            acc = _dot(h_ref[...], w_ref[...])
            cos = cos_ref[...]
            sin_s = sin_ref[...]
            for g in range(GROUPS_PER_TILE):
                slab = acc[:, g * LANES:(g + 1) * LANES]
                kind = _rope_kind(jt * GROUPS_PER_TILE + g)
                if kind is not None:
                    slab = slab * cos + pltpu.roll(slab, LANES // 2, 1) * sin_s
                    if kind == "q":
                        slab = slab * Q_SCALE
                o_ref[:, g * LANES:(g + 1) * LANES] = slab.astype(o_ref.dtype)

    plain = j != _ROPE_TILES[0]
    for jt in _ROPE_TILES[1:]:
        plain = jnp.logical_and(plain, j != jt)

    @pl.when(plain)
    def _():
        o_ref[...] = _dot(h_ref[...], w_ref[...]).astype(o_ref.dtype)


def _proj_in_call(x2, nw, ada3, cos, sin_s, w_bf16, layer, bsz, seq):
    m, d = x2.shape
    n = w_bf16.shape[1]
    tm = min(1024, seq)
    per_b = seq // tm
    base = layer * bsz * 6

    def ada_spec(k):
        return pl.BlockSpec((None, 1, d), lambda i, j: (base + (i // per_b) * 6 + k, 0, 0))

    return pl.pallas_call(
        _proj_in_body,
        grid=(m // tm, n // PROJ_TN),
        in_specs=[
            pl.BlockSpec((tm, d), lambda i, j: (i, 0)),
            pl.BlockSpec((1, d), lambda i, j: (0, 0)),
            ada_spec(0),
            ada_spec(1),
            pl.BlockSpec((tm, LANES), lambda i, j: (i, 0)),
            pl.BlockSpec((tm, LANES), lambda i, j: (i, 0)),
            pl.BlockSpec((d, PROJ_TN), lambda i, j: (0, j)),
        ],
        out_specs=pl.BlockSpec((tm, PROJ_TN), lambda i, j: (i, j)),
        out_shape=jax.ShapeDtypeStruct((m, n), BF16),
        scratch_shapes=[pltpu.VMEM((tm, d), BF16)],
        compiler_params=_cp(("parallel", "arbitrary")),
        name="proj_in",
    )(x2, nw, ada3, ada3, cos, sin_s, w_bf16)


def _attn_body(lam_ref, q_ref, k_ref, v_ref, sw_ref, o_ref, vt_ref, *, tq, lam_init):
    i = pl.program_id(1)

    @pl.when(i == 0)
    def _():
        for h in range(DA_HEADS):
            vt_ref[h, 0:DA_V_DIM, :] = v_ref[:, h * LANES:(h + 1) * LANES].astype(F32).T.astype(BF16)
            vt_ref[h, DA_V_DIM:VT_ROWS, :] = jnp.ones((VT_ROWS - DA_V_DIM, vt_ref.shape[2]), BF16)

    lv = lam_ref[...]
    lam = (jnp.exp(jnp.sum(lv[0:1] * lv[1:2], axis=1, keepdims=True))
           - jnp.exp(jnp.sum(lv[2:3] * lv[3:4], axis=1, keepdims=True)) + lam_init)

    map0 = jnp.bitwise_and(_iota((1, LANES), 1), DA_QK_DIM - 1) < DA_QK_DIM // 2
    zero = jnp.zeros((), BF16)
    qs = []
    for h in range(DA_HEADS):
        q = q_ref[:, h * LANES:(h + 1) * LANES]
        q0 = jnp.where(map0, q, zero)
        q1 = jnp.where(map0, zero, q)
        qs.append(jnp.concatenate([q0, q1], axis=0))

    def mask_for(nk):
        kk = _iota((nk, 2 * tq), 0)
        qq = _iota((nk, 2 * tq), 1)
        q_chunk = jnp.right_shift(jnp.bitwise_and(qq, tq - 1), 6) + (nk - tq) // CHUNK
        return jnp.right_shift(kk, 6) <= q_chunk

    def step(start, nk, carry, allowed):
        out = [None] * DA_HEADS
        for g0 in range(0, DA_HEADS, ATTN_LOCKSTEP):
            grp = range(g0, g0 + ATTN_LOCKSTEP)
            ss = {h: _dot_nt(k_ref[pl.ds(start, nk), h * LANES:(h + 1) * LANES], qs[h]) for h in grp}
            ps, stats = {}, {}
            for h in grp:
                m_i, _ = carry[h]
                s = ss[h]
                if allowed is not None:
                    s = jnp.where(allowed, s, -jnp.inf)
                m_new = jnp.maximum(m_i, jnp.max(s, axis=0, keepdims=True))
                ps[h] = jnp.exp2(s - m_new).astype(BF16)
                stats[h] = (m_new, jnp.exp2(m_i - m_new))
            for h in grp:
                m_new, alpha = stats[h]
                pv = _dot(vt_ref[h, :, pl.ds(start, nk)], ps[h])
                out[h] = (m_new, alpha * carry[h][1] + pv)
        return tuple(out)

    init = tuple((jnp.full((1, 2 * tq), -jnp.inf, F32), jnp.zeros((VT_ROWS, 2 * tq), F32))
                 for _ in range(DA_HEADS))
    carry = lax.fori_loop(
        0, i // 2, lambda j, c: step(pl.multiple_of(j * (2 * tq), 2 * tq), 2 * tq, c, None), init)
    carry = lax.cond(
        i % 2 == 1,
        lambda c: step(pl.multiple_of((i - 1) * tq, tq), 2 * tq, c, mask_for(2 * tq)),
        lambda c: step(pl.multiple_of(i * tq, tq), tq, c, mask_for(tq)),
        carry)

    for h in range(DA_HEADS):
        acc = carry[h][1]
        o_t = acc[:DA_V_DIM] / acc[DA_V_DIM:DA_V_DIM + 1]
        o_t = o_t[:, :tq] - lam * o_t[:, tq:]
        ms = jnp.mean(o_t * o_t, axis=0, keepdims=True)
        o = (o_t * lax.rsqrt(ms + NORM_EPS)).T * sw_ref[...] * (1.0 - lam_init)
        o_ref[:, h * LANES:(h + 1) * LANES] = o.astype(o_ref.dtype)


def _attn_call(u3, lam_vecs, subln_w, layer):
    bsz, seq, _ = u3.shape
    tq = min(256, seq)
    lam_init = 0.8 - 0.6 * math.exp(-0.3 * layer)
    return pl.pallas_call(
        functools.partial(_attn_body, tq=tq, lam_init=lam_init),
        grid=(bsz, seq // tq),
        in_specs=[
            pl.BlockSpec((4, DA_QK_DIM), lambda b, i: (0, 0)),
            pl.BlockSpec((None, tq, HEADW), lambda b, i: (b, i, COL_Q // HEADW)),
            pl.BlockSpec((None, seq, HEADW), lambda b, i: (b, 0, COL_K // HEADW)),
            pl.BlockSpec((None, seq, HEADW), lambda b, i: (b, 0, COL_V // HEADW)),
            pl.BlockSpec((1, LANES), lambda b, i: (0, 0)),
        ],
        out_specs=pl.BlockSpec((None, tq, HEADW), lambda b, i: (b, i, 0)),
        out_shape=jax.ShapeDtypeStruct((bsz, seq, HEADW), BF16),
        scratch_shapes=[pltpu.VMEM((DA_HEADS, VT_ROWS, seq), BF16)],
        compiler_params=_cp(("parallel", "arbitrary")),
        name="diff_attn",
    )(lam_vecs, u3, u3, u3, subln_w.reshape(1, LANES))


def _hgrn_body(lbp_ref, z_ref, i_ref, q_ref, g_ref, nw_ref, o_ref, *, layer, nchunks):
    lbp = lbp_ref[...]
    e = jnp.exp(lbp - jnp.max(lbp, axis=0, keepdims=True))
    sm = e / jnp.sum(e, axis=0, keepdims=True)
    cs = sm[0:1]
    for t in range(1, layer + 1):
        cs = cs + sm[t:t + 1]
    lb = cs - sm[0:1]
    log_lb = jnp.log(lb)
    log1m = jnp.log1p(-lb)

    rr = _iota((CHUNK, CHUNK), 0)
    cc = _iota((CHUNK, CHUNK), 1)
    sub_start = jnp.bitwise_and(rr, ~(SUB - 1))
    cs_mat = jnp.concatenate([(cc <= rr).astype(BF16), (cc < sub_start).astype(BF16)], axis=0)
    diag_ok = jnp.logical_and(cc <= rr, jnp.right_shift(cc, 4) == jnp.right_shift(rr, 4))
    nsub = CHUNK // SUB
    heads = range(HG_HEADS)

    def hcols(x, h):
        return x[:, h * HG_DIM:(h + 1) * HG_DIM]

    def chunk(c, states):
        sl = pl.ds(pl.multiple_of(c * CHUNK, CHUNK), CHUNK)
        z = z_ref[sl, :].astype(F32)
        y = log1m - _softplus(-z)
        mx = jnp.maximum(log_lb, y)
        lf = mx + jnp.log(1.0 + jnp.exp(-jnp.abs(log_lb - y)))
        key = (1.0 - lb) * _sigmoid(-z)
        qc = q_ref[sl, :].astype(F32)
        vc = i_ref[sl, :]
        vt = vc.astype(F32).T.astype(BF16)

        tot = _dot_exact_lhs(cs_mat, lf * LOG2E)
        b = tot[:CHUNK]
        d = b - tot[CHUNK:]
        b_last = b[CHUNK - 1:CHUNK, :]

        q_d = (qc * jnp.exp2(d)).astype(BF16)
        k_d = (key * jnp.exp2(-d)).astype(BF16)
        q_js, k_js = [], []
        for jb in range(nsub - 1):
            lo, hi = jb * SUB, (jb + 1) * SUB
            e_j = b[hi - 1:hi, :]
            q_j = (qc[hi:] * jnp.exp2(b[hi:] - e_j)).astype(BF16)
            k_j = (key[lo:hi] * jnp.exp2(e_j - b[lo:hi])).astype(BF16)
            q_js.append(jnp.concatenate([jnp.zeros((hi, HEADW), BF16), q_j], axis=0))
            pieces = [k_j, jnp.zeros((CHUNK - hi, HEADW), BF16)]
            if lo:
                pieces.insert(0, jnp.zeros((lo, HEADW), BF16))
            k_js.append(jnp.concatenate(pieces, axis=0))
        q_in = (qc * jnp.exp2(b)).astype(BF16)
        k_out = (key * jnp.exp2(b_last - b)).astype(BF16)
        decay = jnp.exp2(b_last)

        s_diag = [_dot_nt(hcols(q_d, h), hcols(k_d, h)) for h in heads]
        s_off = [_dot_nt(jnp.concatenate([hcols(x, h) for x in q_js], axis=1),
                         jnp.concatenate([hcols(x, h) for x in k_js], axis=1)) for h in heads]
        scores = [(jnp.where(diag_ok, s_diag[h], 0.0) + s_off[h]).astype(BF16) for h in heads]
        o_intra = [_dot(scores[h], hcols(vc, h)) for h in heads]
        o_inter = [_dot_nt(hcols(q_in, h), states[h].astype(BF16)) for h in heads]
        upd = [_dot(vt[h * HG_DIM:(h + 1) * HG_DIM, :], hcols(k_out, h)) for h in heads]
        new_states = tuple(states[h] * hcols(decay, h) + upd[h] for h in heads)

        outs = []
        for h in heads:
            o = o_intra[h] + o_inter[h]
            ms = jnp.mean(o * o, axis=-1, keepdims=True)
            outs.append(o * lax.rsqrt(ms + NORM_EPS))
        g = g_ref[sl, :].astype(F32)
        o = jnp.concatenate(outs, axis=1) * nw_ref[...] * (g * _sigmoid(g))
        o_ref[sl, :] = o.astype(o_ref.dtype)
        return new_states

    init = tuple(jnp.zeros((HG_DIM, HG_DIM), F32) for _ in heads)
    lax.fori_loop(0, nchunks, chunk, init, unroll=8)


def _hgrn_call(u3, hg_lb, norm_w, layer):
    bsz, seq, _ = u3.shape
    nl = hg_lb.shape[0]

    def col(base):
        return lambda b: (b, 0, base // HEADW)

    blk = (None, seq, HEADW)
    return pl.pallas_call(
        functools.partial(_hgrn_body, layer=layer, nchunks=seq // CHUNK),
        grid=(bsz,),
        in_specs=[
            pl.BlockSpec((nl, HEADW), lambda b: (0, 0)),
            pl.BlockSpec(blk, col(COL_HF)),
            pl.BlockSpec(blk, col(COL_HI)),
            pl.BlockSpec(blk, col(COL_HQ)),
            pl.BlockSpec(blk, col(COL_HG)),
            pl.BlockSpec((1, HEADW), lambda b: (0, 0)),
        ],
        out_specs=pl.BlockSpec(blk, lambda b: (b, 0, 0)),
        out_shape=jax.ShapeDtypeStruct((bsz, seq, HEADW), BF16),
        compiler_params=_cp(("parallel",)),
        name="hgrn2",
    )(hg_lb, u3, u3, u3, u3, jnp.tile(norm_w, HG_HEADS).reshape(1, HEADW))


def _head_ones(n, width):
    r = _iota((n, n), 0) // width
    c = _iota((n, n), 1) // width
    return (r == c).astype(BF16)


def _rw_prep(cur, prev, prm, vres, has_prev):
    mu_ref, w0_ref, w2_ref, a0_ref, a2_ref, g2_ref, kk_ref, ka_ref = prm
    ts = cur[0].shape[0]
    row = _iota((ts, 1), 0)

    def shifted(idx, lo, hi):
        u = cur[idx][...].astype(F32)
        if prev is None:
            prev_row = jnp.zeros((1, u.shape[1]), F32)
        else:
            prev_row = prev[idx][SUB - 1:SUB, :].astype(F32) * has_prev.astype(F32)
        u_prev = jnp.where(row == 0, prev_row, pltpu.roll(u, 1, 0))
        return u + (u_prev - u) * mu_ref[:, lo:hi]

    r = shifted(0, 0, 512)
    k = shifted(1, 512, 1024)
    v = shifted(2, 1024, 1536)
    lora = shifted(3, 1536, 1792)
    w_lo = lora[:, 0:64]
    a_lo = lora[:, 64:128]
    g_lo = lora[:, 128:256]

    def lora_dot(act, w_ref):
        return _dot(act.astype(BF16), w_ref[...].astype(BF16))

    wpre = w0_ref[...] + lora_dot(jnp.tanh(w_lo), w2_ref)
    w = -_softplus(-wpre) - 0.5
    lw = -jnp.exp(w)
    a = _sigmoid(a0_ref[...] + lora_dot(a_lo, a2_ref))
    g = lora_dot(_sigmoid(g_lo), g2_ref)
    if vres is not None:
        vf_ref, v0_ref, v1_ref, v2_ref = vres
        mix = _sigmoid(v0_ref[...] + lora_dot(lora_dot(v, v1_ref), v2_ref))
        v = v + (vf_ref[...].astype(F32) - v) * mix

    kk = k * kk_ref[...]
    ssq = _head_sum(kk * kk)
    kkn = kk * lax.rsqrt(jnp.maximum(ssq, 1e-24))
    k = k * (1.0 + (a - 1.0) * ka_ref[...])
    return r, k, v, lw, kkn, kkn * a, g


def _rw_scan(r_ref, k_ref, v_ref, lw_ref, kk_ref, bk_ref, g_ref, rk_ref, gw_ref, gb_ref,
             o_ref, state_ref, oacc_ref, nsub):
    t = CHUNK
    gw = RW_GW
    tc = nsub * t
    units = [(s, grp) for s in range(nsub) for grp in range(RW_GROUPS)]

    rr = _iota((tc, tc), 0)
    cc = _iota((tc, tc), 1)
    tril_bd = jnp.logical_and(cc <= rr, (cc // t) == (rr // t)).astype(BF16)
    rr4 = _iota((t, gw), 0)
    cj = jnp.bitwise_and(_iota((t, gw), 1), t - 1)
    strict = cj < rr4
    incl = cj <= rr4
    eye_c = (cj == rr4).astype(F32)
    bd_mask = (_iota((gw, gw), 0) // t) == (_iota((gw, gw), 1) // RW_DIM)
    zero_b = jnp.zeros((), BF16)

    left_head = _iota((t, LANES), 1) < RW_DIM
    zeros_tile = jnp.zeros((t, LANES), BF16)

    def bd(xc):
        xb = xc.astype(BF16)
        rows = []
        for h in range(RW_PACK):
            tile = xb[:, (h // 2) * LANES:(h // 2 + 1) * LANES]
            kept = jnp.where(left_head, tile, zero_b) if h % 2 == 0 else jnp.where(left_head, zero_b, tile)
            rows.append(jnp.concatenate([kept, zeros_tile] if h < 2 else [zeros_tile, kept], axis=1))
        return jnp.concatenate(rows, axis=0)

    def cut(x, u):
        s, grp = u
        return x[s * t:(s + 1) * t, grp * gw:(grp + 1) * gw]

    r = r_ref[...].astype(F32)
    k = k_ref[...].astype(F32)
    v = v_ref[...].astype(F32)
    lw = lw_ref[...] * LOG2E
    kk = kk_ref[...].astype(F32)
    bk = bk_ref[...].astype(F32)
    c = _dot_exact_lhs(tril_bd, lw)
    e_neg = jnp.exp2(-c)
    a_t = -kk * jnp.exp2(c - lw)
    b_t = bk * e_neg
    k_t = k * e_neg
    r_t = r * jnp.exp2(c)
    c_last = jnp.concatenate(
        [jnp.broadcast_to(c[(s + 1) * t - 1:(s + 1) * t, :], (t, RW_WIDTH)) for s in range(nsub)], axis=0)
    decay_out = jnp.exp2(c_last - c)
    b_g = bk * decay_out
    k_g = k * decay_out
    gamma = jnp.exp2(c_last)

    ar = {u: jnp.concatenate([cut(a_t, u), cut(r_t, u)], axis=0).astype(BF16) for u in units}
    x_bk = {u: _dot_nt(ar[u], jnp.concatenate([bd(cut(b_t, u)), bd(cut(k_t, u))], axis=0)) for u in units}
    l_c = {u: jnp.where(strict, x_bk[u][:t, :gw], 0.0) for u in units}
    m_c = {u: jnp.where(strict, x_bk[u][:t, gw:], 0.0).astype(BF16) for u in units}
    rb = {u: jnp.where(incl, x_bk[u][t:, :gw], 0.0).astype(BF16) for u in units}
    rkm = {u: jnp.where(incl, x_bk[u][t:, gw:], 0.0).astype(BF16) for u in units}
    v_bd = {u: bd(cut(v, u)) for u in units}

    p = {u: _dot(l_c[u].astype(BF16), bd(l_c[u])) for u in units}
    s_inv = {u: eye_c + l_c[u] for u in units}
    mo = {u: _dot(jnp.concatenate([m_c[u], rkm[u]], axis=0), v_bd[u]) for u in units}
    mv = {u: mo[u][:t] for u in units}
    o_kv = {u: mo[u][t:] for u in units}
    for lvl in range(1, 6):
        last = lvl == 5
        nxt_p, nxt_s = {}, {}
        for u in units:
            if last:
                nxt_s[u] = s_inv[u] + _dot(s_inv[u].astype(BF16), bd(p[u]))
            else:
                res = _dot(jnp.concatenate([s_inv[u], p[u]], axis=0).astype(BF16), bd(p[u]))
                nxt_s[u] = s_inv[u] + res[:t]
                nxt_p[u] = res[t:]
        p, s_inv = nxt_p, nxt_s

    rbs = {u: _dot(rb[u], bd(s_inv[u])) for u in units}
    wu = {u: _dot(jnp.concatenate([s_inv[u], rbs[u]], axis=0).astype(BF16),
                  jnp.concatenate([bd(cut(a_t, u)), bd(mv[u])], axis=1)) for u in units}
    w_t = {u: wu[u][:t, :gw] for u in units}
    u_t = {u: wu[u][:t, gw:] for u in units}
    r_hat = {u: (cut(r_t, u) + wu[u][t:, :gw]).astype(BF16) for u in units}
    o_hat = {u: wu[u][t:, gw:] + o_kv[u] for u in units}
    g_mat = {u: jnp.where(bd_mask, _dot(w_t[u].T.astype(BF16), cut(b_g, u).astype(BF16)), 0.0).astype(BF16)
             for u in units}
    c_mat = {}
    for u in units:
        uv_t = jnp.concatenate([u_t[u], cut(v, u)], axis=0).T.astype(BF16)
        bkg = jnp.concatenate([cut(b_g, u), cut(k_g, u)], axis=0).astype(BF16)
        full = jnp.where(bd_mask, _dot(uv_t, bkg), 0.0)
        c_mat[u] = full[0:t] + full[t:2 * t] + full[2 * t:3 * t] + full[3 * t:4 * t]

    for s in range(nsub):
        for grp in range(RW_GROUPS):
            u = (s, grp)
            st = state_ref[grp]
            st_b = st.astype(BF16)
            oacc_ref[s * t:(s + 1) * t, grp * gw:(grp + 1) * gw] = _dot_nt(r_hat[u], bd(st_b)) + o_hat[u]
            g_row = gamma[(s + 1) * t - 1:(s + 1) * t, grp * gw:(grp + 1) * gw]
            state_ref[grp] = st * g_row + _dot(st_b, g_mat[u]) + c_mat[u]

    o = oacc_ref[...]
    mean = _head_sum(o) * (1.0 / RW_DIM)
    dlt = o - mean
    var = _head_sum(dlt * dlt) * (1.0 / RW_DIM)
    on = dlt * lax.rsqrt(var + RW_GN_EPS) * gw_ref[...] + gb_ref[...]
    bonus = _head_sum(r * k * rk_ref[...])
    o_ref[...] = ((on + bonus * v) * g_ref[...]).astype(o_ref.dtype)


RW_SLOT_DTYPES = (BF16, BF16, BF16, F32, BF16, BF16, BF16)


def _rw_mix_body(*refs, has_vres, emit_v, nsub):
    nxt, nxt_prev, first, prm = refs[0:4], refs[4:8], refs[8:12], refs[12:20]
    pos = 20
    vres_nxt = vres_first = None
    if has_vres:
        vf_nxt, vf_first, v0, v1, v2 = refs[pos:pos + 5]
        vres_nxt, vres_first = (vf_nxt, v0, v1, v2), (vf_first, v0, v1, v2)
        pos += 5
    rk_ref, gw_ref, gb_ref = refs[pos:pos + 3]
    pos += 3
    o_ref = refs[pos]
    pos += 1
    v_out = None
    if emit_v:
        v_out = refs[pos]
        pos += 1
    state_ref, oacc_ref = refs[pos:pos + 2]
    slots = refs[pos + 2:pos + 9]

    j = pl.program_id(1)
    last = pl.num_programs(1) - 1

    def put(slot, vals):
        for ref, val in zip(slots, vals):
            ref[slot] = val.astype(ref.dtype)

    @pl.when(j == 0)
    def _():
        state_ref[...] = jnp.zeros_like(state_ref)
        put(0, _rw_prep(first, None, prm, vres_first, None))

    put((j + 1) % 2, _rw_prep(nxt, nxt_prev, prm, vres_nxt, jnp.minimum(j + 1, last) > 0))

    cur = [ref.at[j % 2] for ref in slots]
    _rw_scan(*cur, rk_ref, gw_ref, gb_ref, o_ref, state_ref, oacc_ref, nsub)
    if emit_v:
        v_out[...] = cur[2][...]


def _rw_mix_call(u3, mu, w0, w2, a0, a2, g2, k_k, k_a, vres, r_k, gn_w, gn_b, emit_v):
    bsz, seq, _ = u3.shape
    tc = min(256, seq)
    n = seq // tc
    w = RW_WIDTH
    has_vres = vres is not None

    def full(arr):
        return pl.BlockSpec(arr.shape, lambda b, j: (0,) * arr.ndim)

    def nxt(col, width):
        return pl.BlockSpec((None, tc, width), lambda b, j: (b, jnp.minimum(j + 1, n - 1), col // width))

    def nxt_prev(col, width):
        return pl.BlockSpec(
            (None, SUB, width),
            lambda b, j: (b, jnp.maximum(jnp.minimum(j + 1, n - 1) * (tc // SUB) - 1, 0), col // width))

    def first(col, width):
        return pl.BlockSpec((None, tc, width), lambda b, j: (b, 0, col // width))

    cols = [(COL_RR, w), (COL_RK, w), (COL_RV, w), (COL_RL, RW_LORA_W)]
    params = [mu.reshape(1, -1), w0.reshape(1, w), w2, a0.reshape(1, w), a2, g2,
              k_k.reshape(1, w), k_a.reshape(1, w)]
    in_specs = [f(c, wd) for f in (nxt, nxt_prev, first) for c, wd in cols] + [full(p) for p in params]
    args = [u3] * 12 + params
    if has_vres:
        v_first, v0, v1, v2 = vres
        extra = [v0.reshape(1, w), v1, v2]
        in_specs += [pl.BlockSpec((None, tc, w), lambda b, j: (b, jnp.minimum(j + 1, n - 1), 0)),
                     pl.BlockSpec((None, tc, w), lambda b, j: (b, 0, 0))] + [full(p) for p in extra]
        args += [v_first, v_first] + extra
    tail = [r_k.reshape(1, w), gn_w.reshape(1, w), gn_b.reshape(1, w)]
    in_specs += [full(p) for p in tail]
    args += tail

    out_spec = pl.BlockSpec((None, tc, w), lambda b, j: (b, j, 0))
    out_shape = jax.ShapeDtypeStruct((bsz, seq, w), BF16)
    scratch = [pltpu.VMEM((RW_GROUPS, RW_DIM, RW_GW), F32), pltpu.VMEM((tc, w), F32)]
    scratch += [pltpu.VMEM((2, tc, w), dt) for dt in RW_SLOT_DTYPES]
    return pl.pallas_call(
        functools.partial(_rw_mix_body, has_vres=has_vres, emit_v=emit_v, nsub=tc // CHUNK),
        grid=(bsz, n),
        in_specs=in_specs,
        out_specs=[out_spec, out_spec] if emit_v else out_spec,
        out_shape=[out_shape, out_shape] if emit_v else out_shape,
        scratch_shapes=scratch,
        compiler_params=_cp(("parallel", "arbitrary")),
        name="rwkv7",
    )(*args)


def _merge_body(ga_ref, gb_ref, gc_ref, oa_ref, ob_ref, oc_ref, x_ref, gate_ref,
                wa_ref, wb_ref, wc_ref, wo_ref, o_ref):
    merged = (_sigmoid(ga_ref[...].astype(F32)) * _dot(oa_ref[...], wa_ref[...])
              + _sigmoid(gb_ref[...].astype(F32)) * _dot(ob_ref[...], wb_ref[...])
              + _sigmoid(gc_ref[...].astype(F32)) * _dot(oc_ref[...], wc_ref[...]))
    mix = _dot(merged.astype(BF16), wo_ref[...])
    o_ref[...] = x_ref[...] + gate_ref[...] * mix


def _merge_call(u2, o_a, o_b, o_c, x2, ada3, wa, wb, wc, wo, layer, bsz, seq):
    m, d = x2.shape
    tm = min(512, seq)
    per_b = seq // tm
    base = layer * bsz * 6
    wdt = o_a.shape[1]

    def const(arr):
        return pl.BlockSpec(arr.shape, lambda i: (0, 0))

    return pl.pallas_call(
        _merge_body,
        grid=(m // tm,),
        in_specs=[
            pl.BlockSpec((tm, d), lambda i: (i, 0)),
            pl.BlockSpec((tm, d), lambda i: (i, 1)),
            pl.BlockSpec((tm, d), lambda i: (i, 2)),
            pl.BlockSpec((tm, wdt), lambda i: (i, 0)),
            pl.BlockSpec((tm, wdt), lambda i: (i, 0)),
            pl.BlockSpec((tm, wdt), lambda i: (i, 0)),
            pl.BlockSpec((tm, d), lambda i: (i, 0)),
            pl.BlockSpec((None, 1, d), lambda i: (base + (i // per_b) * 6 + 2, 0, 0)),
            const(wa), const(wb), const(wc), const(wo),
        ],
        out_specs=pl.BlockSpec((tm, d), lambda i: (i, 0)),
        out_shape=jax.ShapeDtypeStruct((m, d), F32),
        compiler_params=_cp(("parallel",)),
        name="merge_out",
    )(u2, u2, u2, o_a, o_b, o_c, x2, ada3, wa, wb, wc, wo)


def _ffn_body(x_ref, nw_ref, shift_ref, scale_ref, gate_ref, wi_ref, wo_ref, fn_ref, o_ref, *, final):
    x = x_ref[...]
    h = _norm_mod(x, nw_ref[...], scale_ref[...], shift_ref[...]).astype(BF16)
    acc = jnp.zeros(x.shape, F32)
    lo = 0
    for width in FFN_CHUNKS:
        gh = _dot(h, wi_ref[:, lo:lo + width])
        uh = _dot(h, wi_ref[:, FFN_HIDDEN + lo:FFN_HIDDEN + lo + width])
        act = (gh * _sigmoid(gh) * uh).astype(BF16)
        acc = acc + _dot(act, wo_ref[lo:lo + width, :])
        lo += width
    y = x + gate_ref[...] * acc
    if final:
        ms = jnp.mean(y * y, axis=-1, keepdims=True)
        y = y * lax.rsqrt(ms + NORM_EPS) * fn_ref[...]
    o_ref[...] = y


def _ffn_call(x2, nw, ada3, wi, wo, final_w, layer, bsz, seq, final):
    m, d = x2.shape
    tm = min(512, seq)
    per_b = seq // tm
    base = layer * bsz * 6

    def ada_spec(k):
        return pl.BlockSpec((None, 1, d), lambda i: (base + (i // per_b) * 6 + k, 0, 0))

    def const(arr):
        return pl.BlockSpec(arr.shape, lambda i: (0, 0), pipeline_mode=pl.Buffered(1))

    return pl.pallas_call(
        functools.partial(_ffn_body, final=final),
        grid=(m // tm,),
        in_specs=[
            pl.BlockSpec((tm, d), lambda i: (i, 0)),
            pl.BlockSpec((1, d), lambda i: (0, 0)),
            ada_spec(3), ada_spec(4), ada_spec(5),
            const(wi), const(wo),
            pl.BlockSpec((1, d), lambda i: (0, 0)),
        ],
        out_specs=pl.BlockSpec((tm, d), lambda i: (i, 0)),
        out_shape=jax.ShapeDtypeStruct((m, d), F32),
        compiler_params=_cp(("parallel",)),
        name="ffn_final" if final else "ffn",
    )(x2, nw, ada3, ada3, ada3, wi, wo, final_w)


def _permute_w_in(w):
    def qk_lanes(t):
        t = t.reshape(t.shape[0], DA_HEADS, 2, 2, DA_QK_DIM // 2)
        return jnp.swapaxes(t, 2, 3).reshape(t.shape[0], DA_HEADS * 2 * DA_QK_DIM)

    q, k, v = qk_lanes(w[:, 0:512]), qk_lanes(w[:, 512:1024]), w[:, 1024:1536]
    hgrn = w[:, 1536:3584]
    rw = w[:, 3584:5376]
    gates = w[:, 5376:8448]
    return jnp.concatenate([gates, hgrn, q, k, v, rw], axis=1)


def kernel(x, c, positions, ada_w, ada_b, norm_mix_w, norm_ffn_w, w_in, da_lambda, da_subln_w, hg_lb, hg_norm_w, rw_mu, rw_w0, rw_w2, rw_a0, rw_a2, rw_g2, rw_k_k, rw_k_a, rw_r_k, rw_gn_w, rw_gn_b, rw_v0, rw_v1, rw_v2, w_branch_a, w_branch_b, w_branch_c, w_out, ffn_w_in, ffn_w_out, final_norm_w):
    bsz, seq, d = x.shape
    depth = ada_w.shape[0]
    m = bsz * seq

    ada = _ada_call(c, ada_w, ada_b)
    ada3 = ada.reshape(depth * bsz * 6, 1, d)
    cos, sin_s = _rope_call(positions)

    x2 = x.reshape(m, d)
    v_first = None
    for l in range(depth):
        w_l = _permute_w_in(w_in[l]).astype(BF16)
        u2 = _proj_in_call(x2, norm_mix_w[l].reshape(1, d), ada3, cos, sin_s, w_l, l, bsz, seq)
        u3 = u2.reshape(bsz, seq, IN_COLS)

        o_a = _attn_call(u3, da_lambda[l], da_subln_w[l], l)
        o_b = _hgrn_call(u3, hg_lb, hg_norm_w[l], l)
        vres = None if l == 0 else (v_first, rw_v0[l - 1], rw_v1[l - 1], rw_v2[l - 1])
        emit_v = l == 0 and depth > 1
        res = _rw_mix_call(u3, rw_mu[l], rw_w0[l], rw_w2[l], rw_a0[l], rw_a2[l], rw_g2[l], rw_k_k[l],
                           rw_k_a[l], vres, rw_r_k[l], rw_gn_w[l], rw_gn_b[l], emit_v)
        if emit_v:
            o_c, v_first = res
        else:
            o_c = res

        x2 = _merge_call(u2, o_a.reshape(m, -1), o_b.reshape(m, -1), o_c.reshape(m, -1), x2, ada3,
                         w_branch_a[l].astype(BF16), w_branch_b[l].astype(BF16),
                         w_branch_c[l].astype(BF16), w_out[l].astype(BF16), l, bsz, seq)
        x2 = _ffn_call(x2, norm_ffn_w[l].reshape(1, d), ada3, ffn_w_in[l].astype(BF16),
                       ffn_w_out[l].astype(BF16), final_norm_w.reshape(1, d), l, bsz, seq,
                       final=(l == depth - 1))
    return x2.reshape(bsz, seq, d)
```

```python
import functools
import math

import jax
import jax.numpy as jnp
from jax import lax
from jax.experimental import pallas as pl
from jax.experimental.pallas import tpu as pltpu

F32 = jnp.float32
BF16 = jnp.bfloat16

D_MODEL = 1024
CHUNK = 64
ROPE_THETA = 10000.0
NORM_EPS = 1e-6

DA_HEADS = 4
DA_QK_DIM = 64
DA_V_DIM = 128
HG_HEADS = 4
HG_DIM = 128
RW_HEADS = 8
RW_DIM = 64
RW_WIDTH = 512
RW_GN_EPS = 64e-5
FFN_HIDDEN = 2816
FFN_CHUNKS = (768, 768, 768, 512)
IN_COLS = 8448

LANES = 128
ATTN_LOCKSTEP = 4
LOG2E = 1.4426950408889634
Q_SCALE = DA_QK_DIM ** -0.5 * LOG2E
VT_ROWS = DA_V_DIM + 16
SUB = 16
RW_PACK = 4
RW_GW = RW_PACK * RW_DIM
RW_GROUPS = RW_HEADS // RW_PACK

COL_GATES = 0
COL_HF = 3072
COL_HI = 3584
COL_HQ = 4096
COL_HG = 4608
COL_Q = 5120
COL_K = 5632
COL_V = 6144
COL_RR = 6656
COL_RK = 7168
COL_RV = 7680
COL_RL = 8192
RW_LORA_W = 256
HEADW = 512
PROJ_TN = 2816
GROUPS_PER_TILE = PROJ_TN // LANES
ROPE_Q0 = COL_Q // LANES
ROPE_K0 = COL_K // LANES

VMEM_LIMIT = 56 * 1024 * 1024


def _cp(sem):
    return pltpu.CompilerParams(dimension_semantics=sem, vmem_limit_bytes=VMEM_LIMIT)


def _sigmoid(x):
    return 1.0 / (1.0 + jnp.exp(-x))


def _softplus(x):
    return jnp.maximum(x, 0.0) + jnp.log(1.0 + jnp.exp(-jnp.abs(x)))


def _dot(a, b):
    return jnp.dot(a, b, preferred_element_type=F32)


def _dot_nt(a, b):
    return lax.dot_general(a, b, (((1,), (1,)), ((), ())), preferred_element_type=F32)


def _dot_exact_lhs(a01, x):
    n = x.shape[1]
    hi = x.astype(BF16)
    lo = (x - hi.astype(F32)).astype(BF16)
    res = _dot(a01, jnp.concatenate([hi, lo], axis=1))
    return res[:, 0:n] + res[:, n:2 * n]


def _head_sum(x):
    ones_g = _head_ones(RW_GW, RW_DIM)
    xb = x.astype(BF16)
    return jnp.concatenate(
        [_dot(xb[:, grp * RW_GW:(grp + 1) * RW_GW], ones_g) for grp in range(x.shape[1] // RW_GW)], axis=1)


def _dot_hp(a, b):
    a_hi = a.astype(BF16)
    a_lo = (a - a_hi.astype(F32)).astype(BF16)
    b_hi = b.astype(BF16)
    b_lo = (b - b_hi.astype(F32)).astype(BF16)
    return _dot(a_hi, b_hi) + _dot(a_hi, b_lo) + _dot(a_lo, b_hi)


def _iota(shape, dim):
    return lax.broadcasted_iota(jnp.int32, shape, dim)


def _ada_body(c_ref, w_ref, b_ref, o_ref):
    c = c_ref[...]
    ca = c * _sigmoid(c)
    o_ref[...] = _dot_hp(ca, w_ref[...]) + b_ref[...]


def _ada_call(c, ada_w, ada_b):
    nl, d, n6 = ada_w.shape
    bsz = c.shape[0]
    tn = 1536
    return pl.pallas_call(
        _ada_body,
        grid=(nl, n6 // tn),
        in_specs=[
            pl.BlockSpec((bsz, d), lambda l, j: (0, 0)),
            pl.BlockSpec((None, d, tn), lambda l, j: (l, 0, j)),
            pl.BlockSpec((None, 1, tn), lambda l, j: (l, 0, j)),
        ],
        out_specs=pl.BlockSpec((None, bsz, tn), lambda l, j: (l, 0, j)),
        out_shape=jax.ShapeDtypeStruct((nl, bsz, n6), F32),
        compiler_params=_cp(("parallel", "parallel")),
        name="ada",
    )(c, ada_w, ada_b.reshape(nl, 1, n6))


def _rope_body(pos_ref, invf_ref, sgn_ref, cos_ref, sin_ref):
    ang = pos_ref[...].astype(F32) * invf_ref[...]
    cos_ref[...] = jnp.cos(ang)
    sin_ref[...] = jnp.sin(ang) * sgn_ref[...]


def _rope_call(positions):
    m = positions.size
    tm = min(2048, m)
    inv_freq = ROPE_THETA ** (-jnp.arange(0, DA_QK_DIM, 2, dtype=F32) / DA_QK_DIM)
    invf = jnp.tile(inv_freq, LANES // (DA_QK_DIM // 2)).reshape(1, LANES)
    sgn = jnp.where(jnp.arange(LANES) < LANES // 2, -1.0, 1.0).astype(F32).reshape(1, LANES)
    return pl.pallas_call(
        _rope_body,
        grid=(m // tm,),
        in_specs=[
            pl.BlockSpec((tm, 1), lambda i: (i, 0)),
            pl.BlockSpec((1, LANES), lambda i: (0, 0)),
            pl.BlockSpec((1, LANES), lambda i: (0, 0)),
        ],
        out_specs=[pl.BlockSpec((tm, LANES), lambda i: (i, 0))] * 2,
        out_shape=[jax.ShapeDtypeStruct((m, LANES), F32)] * 2,
        compiler_params=_cp(("parallel",)),
        name="rope_tables",
    )(positions.reshape(m, 1), invf, sgn)


def _norm_mod(x, nw, scale, shift):
    ms = jnp.mean(x * x, axis=-1, keepdims=True)
    y = x * lax.rsqrt(ms + NORM_EPS) * nw
    return y * (1.0 + scale) + shift


def _rope_kind(group):
    if ROPE_Q0 <= group < ROPE_Q0 + DA_HEADS:
        return "q"
    if ROPE_K0 <= group < ROPE_K0 + DA_HEADS:
        return "k"
    return None


_ROPE_TILES = sorted({g // GROUPS_PER_TILE for g in range(IN_COLS // LANES) if _rope_kind(g)})


def _proj_in_body(x_ref, nw_ref, shift_ref, scale_ref, cos_ref, sin_ref, w_ref, o_ref, h_ref):
    j = pl.program_id(1)

    @pl.when(j == 0)
    def _():
        h_ref[...] = _norm_mod(x_ref[...], nw_ref[...], scale_ref[...], shift_ref[...]).astype(BF16)

    for jt in _ROPE_TILES:
        @pl.when(j == jt)
        def _(jt=jt):
            acc = _dot(h_ref[...], w_ref[...])
            cos = cos_ref[...]
            sin_s = sin_ref[...]
            for g in range(GROUPS_PER_TILE):
                slab = acc[:, g * LANES:(g + 1) * LANES]
                kind = _rope_kind(jt * GROUPS_PER_TILE + g)
                if kind is not None:
                    slab = slab * cos + pltpu.roll(slab, LANES // 2, 1) * sin_s
                    if kind == "q":
                        slab = slab * Q_SCALE
                o_ref[:, g * LANES:(g + 1) * LANES] = slab.astype(o_ref.dtype)

    plain = j != _ROPE_TILES[0]
    for jt in _ROPE_TILES[1:]:
        plain = jnp.logical_and(plain, j != jt)

    @pl.when(plain)
    def _():
        o_ref[...] = _dot(h_ref[...], w_ref[...]).astype(o_ref.dtype)


def _proj_in_call(x2, nw, ada3, cos, sin_s, w_bf16, layer, bsz, seq):
    m, d = x2.shape
    n = w_bf16.shape[1]
    tm = min(1024, seq)
    per_b = seq // tm
    base = layer * bsz * 6

    def ada_spec(k):
        return pl.BlockSpec((None, 1, d), lambda i, j: (base + (i // per_b) * 6 + k, 0, 0))

    return pl.pallas_call(
        _proj_in_body,
        grid=(m // tm, n // PROJ_TN),
        in_specs=[
            pl.BlockSpec((tm, d), lambda i, j: (i, 0)),
            pl.BlockSpec((1, d), lambda i, j: (0, 0)),
            ada_spec(0),
            ada_spec(1),
            pl.BlockSpec((tm, LANES), lambda i, j: (i, 0)),
            pl.BlockSpec((tm, LANES), lambda i, j: (i, 0)),
            pl.BlockSpec((d, PROJ_TN), lambda i, j: (0, j)),
        ],
        out_specs=pl.BlockSpec((tm, PROJ_TN), lambda i, j: (i, j)),
        out_shape=jax.ShapeDtypeStruct((m, n), BF16),
        scratch_shapes=[pltpu.VMEM((tm, d), BF16)],
        compiler_params=_cp(("parallel", "arbitrary")),
        name="proj_in",
    )(x2, nw, ada3, ada3, cos, sin_s, w_bf16)


def _attn_body(lam_ref, q_ref, k_ref, v_ref, sw_ref, o_ref, vt_ref, *, tq, lam_init):
    i = pl.program_id(1)

    @pl.when(i == 0)
    def _():
        for h in range(DA_HEADS):
            vt_ref[h, 0:DA_V_DIM, :] = v_ref[:, h * LANES:(h + 1) * LANES].astype(F32).T.astype(BF16)
            vt_ref[h, DA_V_DIM:VT_ROWS, :] = jnp.ones((VT_ROWS - DA_V_DIM, vt_ref.shape[2]), BF16)

    lv = lam_ref[...]
    lam = (jnp.exp(jnp.sum(lv[0:1] * lv[1:2], axis=1, keepdims=True))
           - jnp.exp(jnp.sum(lv[2:3] * lv[3:4], axis=1, keepdims=True)) + lam_init)

    map0 = jnp.bitwise_and(_iota((1, LANES), 1), DA_QK_DIM - 1) < DA_QK_DIM // 2
    zero = jnp.zeros((), BF16)
    qs = []
    for h in range(DA_HEADS):
        q = q_ref[:, h * LANES:(h + 1) * LANES]
        q0 = jnp.where(map0, q, zero)
        q1 = jnp.where(map0, zero, q)
        qs.append(jnp.concatenate([q0, q1], axis=0))

    def mask_for(nk):
        k_chunk = jnp.right_shift(_iota((nk, 1), 0), 6)
        q_chunk = jnp.right_shift(jnp.bitwise_and(_iota((1, 2 * tq), 1), tq - 1), 6) + (nk - tq) // CHUNK
        return k_chunk <= q_chunk

    def step(start, nk, carry, allowed):
        out = [None] * DA_HEADS
        for g0 in range(0, DA_HEADS, ATTN_LOCKSTEP):
            grp = range(g0, g0 + ATTN_LOCKSTEP)
            ss = {h: _dot_nt(k_ref[pl.ds(start, nk), h * LANES:(h + 1) * LANES], qs[h]) for h in grp}
            ps, stats = {}, {}
            for h in grp:
                m_i, _ = carry[h]
                s = ss[h]
                if allowed is not None:
                    s = jnp.where(allowed, s, -jnp.inf)
                m_new = jnp.maximum(m_i, jnp.max(s, axis=0, keepdims=True))
                ps[h] = jnp.exp2(s - m_new).astype(BF16)
                stats[h] = (m_new, jnp.exp2(m_i - m_new))
            for h in grp:
                m_new, alpha = stats[h]
                pv = _dot(vt_ref[h, :, pl.ds(start, nk)], ps[h])
                out[h] = (m_new, alpha * carry[h][1] + pv)
        return tuple(out)

    init = tuple((jnp.full((1, 2 * tq), -jnp.inf, F32), jnp.zeros((VT_ROWS, 2 * tq), F32))
                 for _ in range(DA_HEADS))
    carry = lax.fori_loop(
        0, i // 2, lambda j, c: step(pl.multiple_of(j * (2 * tq), 2 * tq), 2 * tq, c, None), init)
    carry = lax.cond(
        i % 2 == 1,
        lambda c: step(pl.multiple_of((i - 1) * tq, tq), 2 * tq, c, mask_for(2 * tq)),
        lambda c: step(pl.multiple_of(i * tq, tq), tq, c, mask_for(tq)),
        carry)

    for h in range(DA_HEADS):
        acc = carry[h][1]
        o_t = acc[:DA_V_DIM] / acc[DA_V_DIM:DA_V_DIM + 1]
        o_t = o_t[:, :tq] - lam * o_t[:, tq:]
        ms = jnp.mean(o_t * o_t, axis=0, keepdims=True)
        o = (o_t * lax.rsqrt(ms + NORM_EPS)).T * sw_ref[...] * (1.0 - lam_init)
        o_ref[:, h * LANES:(h + 1) * LANES] = o.astype(o_ref.dtype)


def _attn_call(u3, lam_vecs, subln_w, layer):
    bsz, seq, _ = u3.shape
    tq = min(256, seq)
    lam_init = 0.8 - 0.6 * math.exp(-0.3 * layer)
    return pl.pallas_call(
        functools.partial(_attn_body, tq=tq, lam_init=lam_init),
        grid=(bsz, seq // tq),
        in_specs=[
            pl.BlockSpec((4, DA_QK_DIM), lambda b, i: (0, 0)),
            pl.BlockSpec((None, tq, HEADW), lambda b, i: (b, i, COL_Q // HEADW)),
            pl.BlockSpec((None, seq, HEADW), lambda b, i: (b, 0, COL_K // HEADW)),
            pl.BlockSpec((None, seq, HEADW), lambda b, i: (b, 0, COL_V // HEADW)),
            pl.BlockSpec((1, LANES), lambda b, i: (0, 0)),
        ],
        out_specs=pl.BlockSpec((None, tq, HEADW), lambda b, i: (b, i, 0)),
        out_shape=jax.ShapeDtypeStruct((bsz, seq, HEADW), BF16),
        scratch_shapes=[pltpu.VMEM((DA_HEADS, VT_ROWS, seq), BF16)],
        compiler_params=_cp(("parallel", "arbitrary")),
        name="diff_attn",
    )(lam_vecs, u3, u3, u3, subln_w.reshape(1, LANES))


def _hgrn_body(lbp_ref, z_ref, i_ref, q_ref, g_ref, nw_ref, o_ref, *, layer, nchunks):
    lbp = lbp_ref[...]
    e = jnp.exp(lbp - jnp.max(lbp, axis=0, keepdims=True))
    sm = e / jnp.sum(e, axis=0, keepdims=True)
    cs = sm[0:1]
    for t in range(1, layer + 1):
        cs = cs + sm[t:t + 1]
    lb = cs - sm[0:1]
    log_lb = jnp.log(lb)
    log1m = jnp.log1p(-lb)

    rr = _iota((CHUNK, CHUNK), 0)
    cc = _iota((CHUNK, CHUNK), 1)
    sub_start = jnp.bitwise_and(rr, ~(SUB - 1))
    cs_mat = jnp.concatenate([(cc <= rr).astype(BF16), (cc < sub_start).astype(BF16)], axis=0)
    diag_ok = jnp.logical_and(cc <= rr, jnp.right_shift(cc, 4) == jnp.right_shift(rr, 4))
    nsub = CHUNK // SUB
    heads = range(HG_HEADS)

    def hcols(x, h):
        return x[:, h * HG_DIM:(h + 1) * HG_DIM]

    def chunk(c, states):
        sl = pl.ds(pl.multiple_of(c * CHUNK, CHUNK), CHUNK)
        z = z_ref[sl, :].astype(F32)
        y = log1m - _softplus(-z)
        mx = jnp.maximum(log_lb, y)
        lf = mx + jnp.log(1.0 + jnp.exp(-jnp.abs(log_lb - y)))
        key = (1.0 - lb) * _sigmoid(-z)
        qc = q_ref[sl, :].astype(F32)
        vc = i_ref[sl, :]
        vt = vc.astype(F32).T.astype(BF16)

        tot = _dot_exact_lhs(cs_mat, lf * LOG2E)
        b = tot[:CHUNK]
        d = b - tot[CHUNK:]
        b_last = b[CHUNK - 1:CHUNK, :]

        q_d = (qc * jnp.exp2(d)).astype(BF16)
        k_d = (key * jnp.exp2(-d)).astype(BF16)
        q_js, k_js = [], []
        for jb in range(nsub - 1):
            lo, hi = jb * SUB, (jb + 1) * SUB
            e_j = b[hi - 1:hi, :]
            q_j = (qc[hi:] * jnp.exp2(b[hi:] - e_j)).astype(BF16)
            k_j = (key[lo:hi] * jnp.exp2(e_j - b[lo:hi])).astype(BF16)
            q_js.append(jnp.concatenate([jnp.zeros((hi, HEADW), BF16), q_j], axis=0))
            pieces = [k_j, jnp.zeros((CHUNK - hi, HEADW), BF16)]
            if lo:
                pieces.insert(0, jnp.zeros((lo, HEADW), BF16))
            k_js.append(jnp.concatenate(pieces, axis=0))
        q_in = (qc * jnp.exp2(b)).astype(BF16)
        k_out = (key * jnp.exp2(b_last - b)).astype(BF16)
        decay = jnp.exp2(b_last)

        s_diag = [_dot_nt(hcols(q_d, h), hcols(k_d, h)) for h in heads]
        s_off = [_dot_nt(jnp.concatenate([hcols(x, h) for x in q_js], axis=1),
                         jnp.concatenate([hcols(x, h) for x in k_js], axis=1)) for h in heads]
        scores = [(jnp.where(diag_ok, s_diag[h], 0.0) + s_off[h]).astype(BF16) for h in heads]
        o_intra = [_dot(scores[h], hcols(vc, h)) for h in heads]
        o_inter = [_dot_nt(hcols(q_in, h), states[h].astype(BF16)) for h in heads]
        upd = [_dot(vt[h * HG_DIM:(h + 1) * HG_DIM, :], hcols(k_out, h)) for h in heads]
        new_states = tuple(states[h] * hcols(decay, h) + upd[h] for h in heads)

        outs = []
        for h in heads:
            o = o_intra[h] + o_inter[h]
            ms = jnp.mean(o * o, axis=-1, keepdims=True)
            outs.append(o * lax.rsqrt(ms + NORM_EPS))
        g = g_ref[sl, :].astype(F32)
        o = jnp.concatenate(outs, axis=1) * nw_ref[...] * (g * _sigmoid(g))
        o_ref[sl, :] = o.astype(o_ref.dtype)
        return new_states

    init = tuple(jnp.zeros((HG_DIM, HG_DIM), F32) for _ in heads)
    lax.fori_loop(0, nchunks, chunk, init, unroll=8)


def _hgrn_call(u3, hg_lb, norm_w, layer):
    bsz, seq, _ = u3.shape
    nl = hg_lb.shape[0]

    def col(base):
        return lambda b: (b, 0, base // HEADW)

    blk = (None, seq, HEADW)
    return pl.pallas_call(
        functools.partial(_hgrn_body, layer=layer, nchunks=seq // CHUNK),
        grid=(bsz,),
        in_specs=[
            pl.BlockSpec((nl, HEADW), lambda b: (0, 0)),
            pl.BlockSpec(blk, col(COL_HF)),
            pl.BlockSpec(blk, col(COL_HI)),
            pl.BlockSpec(blk, col(COL_HQ)),
            pl.BlockSpec(blk, col(COL_HG)),
            pl.BlockSpec((1, HEADW), lambda b: (0, 0)),
        ],
        out_specs=pl.BlockSpec(blk, lambda b: (b, 0, 0)),
        out_shape=jax.ShapeDtypeStruct((bsz, seq, HEADW), BF16),
        compiler_params=_cp(("parallel",)),
        name="hgrn2",
    )(hg_lb, u3, u3, u3, u3, jnp.tile(norm_w, HG_HEADS).reshape(1, HEADW))


def _head_ones(n, width):
    r = _iota((n, n), 0) // width
    c = _iota((n, n), 1) // width
    return (r == c).astype(BF16)


def _rw_prep(cur, prev, prm, vres, has_prev):
    mu_ref, w0_ref, w2_ref, a0_ref, a2_ref, g2_ref, kk_ref, ka_ref = prm
    ts = cur[0].shape[0]
    row = _iota((ts, 1), 0)

    def shifted(idx, lo, hi):
        u = cur[idx][...].astype(F32)
        if prev is None:
            prev_row = jnp.zeros((1, u.shape[1]), F32)
        else:
            prev_row = prev[idx][SUB - 1:SUB, :].astype(F32) * has_prev.astype(F32)
        u_prev = jnp.where(row == 0, prev_row, pltpu.roll(u, 1, 0))
        return u + (u_prev - u) * mu_ref[:, lo:hi]

    r = shifted(0, 0, 512)
    k = shifted(1, 512, 1024)
    v = shifted(2, 1024, 1536)
    lora = shifted(3, 1536, 1792)
    w_lo = lora[:, 0:64]
    a_lo = lora[:, 64:128]
    g_lo = lora[:, 128:256]

    def lora_dot(act, w_ref):
        return _dot(act.astype(BF16), w_ref[...].astype(BF16))

    wpre = w0_ref[...] + lora_dot(jnp.tanh(w_lo), w2_ref)
    w = -_softplus(-wpre) - 0.5
    lw = -jnp.exp(w)
    a = _sigmoid(a0_ref[...] + lora_dot(a_lo, a2_ref))
    g = lora_dot(_sigmoid(g_lo), g2_ref)
    if vres is not None:
        vf_ref, v0_ref, v1_ref, v2_ref = vres
        mix = _sigmoid(v0_ref[...] + lora_dot(lora_dot(v, v1_ref), v2_ref))
        v = v + (vf_ref[...].astype(F32) - v) * mix

    kk = k * kk_ref[...]
    ssq = _head_sum(kk * kk)
    kkn = kk * lax.rsqrt(jnp.maximum(ssq, 1e-24))
    k = k * (1.0 + (a - 1.0) * ka_ref[...])
    return r, k, v, lw, kkn, kkn * a, g


def _rw_scan(r_ref, k_ref, v_ref, lw_ref, kk_ref, bk_ref, g_ref, rk_ref, gw_ref, gb_ref,
             o_ref, state_ref, oacc_ref, nsub):
    t = CHUNK
    gw = RW_GW
    tc = nsub * t
    units = [(s, grp) for s in range(nsub) for grp in range(RW_GROUPS)]

    rr = _iota((tc, tc), 0)
    cc = _iota((tc, tc), 1)
    tril_bd = jnp.logical_and(cc <= rr, (cc // t) == (rr // t)).astype(BF16)
    rr4 = _iota((t, gw), 0)
    cj = jnp.bitwise_and(_iota((t, gw), 1), t - 1)
    strict = cj < rr4
    incl = cj <= rr4
    eye_c = (cj == rr4).astype(F32)
    bd_mask = (_iota((gw, gw), 0) // t) == (_iota((gw, gw), 1) // RW_DIM)
    zero_b = jnp.zeros((), BF16)

    left_head = _iota((t, LANES), 1) < RW_DIM
    zeros_tile = jnp.zeros((t, LANES), BF16)

    def bd(xc):
        xb = xc.astype(BF16)
        rows = []
        for h in range(RW_PACK):
            tile = xb[:, (h // 2) * LANES:(h // 2 + 1) * LANES]
            kept = jnp.where(left_head, tile, zero_b) if h % 2 == 0 else jnp.where(left_head, zero_b, tile)
            rows.append(jnp.concatenate([kept, zeros_tile] if h < 2 else [zeros_tile, kept], axis=1))
        return jnp.concatenate(rows, axis=0)

    def cut(x, u):
        s, grp = u
        return x[s * t:(s + 1) * t, grp * gw:(grp + 1) * gw]

    r = r_ref[...].astype(F32)
    k = k_ref[...].astype(F32)
    v = v_ref[...].astype(F32)
    lw = lw_ref[...] * LOG2E
    kk = kk_ref[...].astype(F32)
    bk = bk_ref[...].astype(F32)
    c = _dot_exact_lhs(tril_bd, lw)
    e_neg = jnp.exp2(-c)
    a_t = -kk * jnp.exp2(c - lw)
    b_t = bk * e_neg
    k_t = k * e_neg
    r_t = r * jnp.exp2(c)
    c_last = jnp.concatenate(
        [jnp.broadcast_to(c[(s + 1) * t - 1:(s + 1) * t, :], (t, RW_WIDTH)) for s in range(nsub)], axis=0)
    decay_out = jnp.exp2(c_last - c)
    b_g = bk * decay_out
    k_g = k * decay_out
    gamma = jnp.exp2(c_last)

    ar = {u: jnp.concatenate([cut(a_t, u), cut(r_t, u)], axis=0).astype(BF16) for u in units}
    x_bk = {u: _dot_nt(ar[u], jnp.concatenate([bd(cut(b_t, u)), bd(cut(k_t, u))], axis=0)) for u in units}
    l_c = {u: jnp.where(strict, x_bk[u][:t, :gw], 0.0) for u in units}
    m_c = {u: jnp.where(strict, x_bk[u][:t, gw:], 0.0).astype(BF16) for u in units}
    rb = {u: jnp.where(incl, x_bk[u][t:, :gw], 0.0).astype(BF16) for u in units}
    rkm = {u: jnp.where(incl, x_bk[u][t:, gw:], 0.0).astype(BF16) for u in units}
    v_bd = {u: bd(cut(v, u)) for u in units}

    p = {u: _dot(l_c[u].astype(BF16), bd(l_c[u])) for u in units}
    s_inv = {u: eye_c + l_c[u] for u in units}
    mo = {u: _dot(jnp.concatenate([m_c[u], rkm[u]], axis=0), v_bd[u]) for u in units}
    mv = {u: mo[u][:t] for u in units}
    o_kv = {u: mo[u][t:] for u in units}
    for lvl in range(1, 6):
        last = lvl == 5
        nxt_p, nxt_s = {}, {}
        for u in units:
            if last:
                nxt_s[u] = s_inv[u] + _dot(s_inv[u].astype(BF16), bd(p[u]))
            else:
                res = _dot(jnp.concatenate([s_inv[u], p[u]], axis=0).astype(BF16), bd(p[u]))
                nxt_s[u] = s_inv[u] + res[:t]
                nxt_p[u] = res[t:]
        p, s_inv = nxt_p, nxt_s

    rbs = {u: _dot(rb[u], bd(s_inv[u])) for u in units}
    wu = {u: _dot(jnp.concatenate([s_inv[u], rbs[u]], axis=0).astype(BF16),
                  jnp.concatenate([bd(cut(a_t, u)), bd(mv[u])], axis=1)) for u in units}
    w_t = {u: wu[u][:t, :gw] for u in units}
    u_t = {u: wu[u][:t, gw:] for u in units}
    r_hat = {u: (cut(r_t, u) + wu[u][t:, :gw]).astype(BF16) for u in units}
    o_hat = {u: wu[u][t:, gw:] + o_kv[u] for u in units}
    g_mat = {u: jnp.where(bd_mask, _dot(w_t[u].T.astype(BF16), cut(b_g, u).astype(BF16)), 0.0).astype(BF16)
             for u in units}
    c_mat = {}
    for u in units:
        uv_t = jnp.concatenate([u_t[u], cut(v, u)], axis=0).T.astype(BF16)
        bkg = jnp.concatenate([cut(b_g, u), cut(k_g, u)], axis=0).astype(BF16)
        full = jnp.where(bd_mask, _dot(uv_t, bkg), 0.0)
        c_mat[u] = full[0:t] + full[t:2 * t] + full[2 * t:3 * t] + full[3 * t:4 * t]

    for s in range(nsub):
        for grp in range(RW_GROUPS):
            u = (s, grp)
            st = state_ref[grp]
            st_b = st.astype(BF16)
            oacc_ref[s * t:(s + 1) * t, grp * gw:(grp + 1) * gw] = _dot_nt(r_hat[u], bd(st_b)) + o_hat[u]
            g_row = gamma[(s + 1) * t - 1:(s + 1) * t, grp * gw:(grp + 1) * gw]
            state_ref[grp] = st * g_row + _dot(st_b, g_mat[u]) + c_mat[u]

    o = oacc_ref[...]
    mean = _head_sum(o) * (1.0 / RW_DIM)
    dlt = o - mean
    var = _head_sum(dlt * dlt) * (1.0 / RW_DIM)
    on = dlt * lax.rsqrt(var + RW_GN_EPS) * gw_ref[...] + gb_ref[...]
    bonus = _head_sum(r * k * rk_ref[...])
    o_ref[...] = ((on + bonus * v) * g_ref[...]).astype(o_ref.dtype)


RW_SLOT_DTYPES = (BF16, BF16, BF16, F32, BF16, BF16, BF16)


def _rw_mix_body(*refs, has_vres, emit_v, nsub):
    nxt, nxt_prev, first, prm = refs[0:4], refs[4:8], refs[8:12], refs[12:20]
    pos = 20
    vres_nxt = vres_first = None
    if has_vres:
        vf_nxt, vf_first, v0, v1, v2 = refs[pos:pos + 5]
        vres_nxt, vres_first = (vf_nxt, v0, v1, v2), (vf_first, v0, v1, v2)
        pos += 5
    rk_ref, gw_ref, gb_ref = refs[pos:pos + 3]
    pos += 3
    o_ref = refs[pos]
    pos += 1
    v_out = None
    if emit_v:
        v_out = refs[pos]
        pos += 1
    state_ref, oacc_ref = refs[pos:pos + 2]
    slots = refs[pos + 2:pos + 9]

    j = pl.program_id(1)
    last = pl.num_programs(1) - 1

    def put(slot, vals):
        for ref, val in zip(slots, vals):
            ref[slot] = val.astype(ref.dtype)

    @pl.when(j == 0)
    def _():
        state_ref[...] = jnp.zeros_like(state_ref)
        put(0, _rw_prep(first, None, prm, vres_first, None))

    put((j + 1) % 2, _rw_prep(nxt, nxt_prev, prm, vres_nxt, jnp.minimum(j + 1, last) > 0))

    cur = [ref.at[j % 2] for ref in slots]
    _rw_scan(*cur, rk_ref, gw_ref, gb_ref, o_ref, state_ref, oacc_ref, nsub)
    if emit_v:
        v_out[...] = cur[2][...]


def _rw_mix_call(u3, mu, w0, w2, a0, a2, g2, k_k, k_a, vres, r_k, gn_w, gn_b, emit_v):
    bsz, seq, _ = u3.shape
    tc = min(256, seq)
    n = seq // tc
    w = RW_WIDTH
    has_vres = vres is not None

    def full(arr):
        return pl.BlockSpec(arr.shape, lambda b, j: (0,) * arr.ndim)

    def nxt(col, width):
        return pl.BlockSpec((None, tc, width), lambda b, j: (b, jnp.minimum(j + 1, n - 1), col // width))

    def nxt_prev(col, width):
        return pl.BlockSpec(
            (None, SUB, width),
            lambda b, j: (b, jnp.maximum(jnp.minimum(j + 1, n - 1) * (tc // SUB) - 1, 0), col // width))

    def first(col, width):
        return pl.BlockSpec((None, tc, width), lambda b, j: (b, 0, col // width))

    cols = [(COL_RR, w), (COL_RK, w), (COL_RV, w), (COL_RL, RW_LORA_W)]
    params = [mu.reshape(1, -1), w0.reshape(1, w), w2, a0.reshape(1, w), a2, g2,
              k_k.reshape(1, w), k_a.reshape(1, w)]
    in_specs = [f(c, wd) for f in (nxt, nxt_prev, first) for c, wd in cols] + [full(p) for p in params]
    args = [u3] * 12 + params
    if has_vres:
        v_first, v0, v1, v2 = vres
        extra = [v0.reshape(1, w), v1, v2]
        in_specs += [pl.BlockSpec((None, tc, w), lambda b, j: (b, jnp.minimum(j + 1, n - 1), 0)),
                     pl.BlockSpec((None, tc, w), lambda b, j: (b, 0, 0))] + [full(p) for p in extra]
        args += [v_first, v_first] + extra
    tail = [r_k.reshape(1, w), gn_w.reshape(1, w), gn_b.reshape(1, w)]
    in_specs += [full(p) for p in tail]
    args += tail

    out_spec = pl.BlockSpec((None, tc, w), lambda b, j: (b, j, 0))
    out_shape = jax.ShapeDtypeStruct((bsz, seq, w), BF16)
    scratch = [pltpu.VMEM((RW_GROUPS, RW_DIM, RW_GW), F32), pltpu.VMEM((tc, w), F32)]
    scratch += [pltpu.VMEM((2, tc, w), dt) for dt in RW_SLOT_DTYPES]
    return pl.pallas_call(
        functools.partial(_rw_mix_body, has_vres=has_vres, emit_v=emit_v, nsub=tc // CHUNK),
        grid=(bsz, n),
        in_specs=in_specs,
        out_specs=[out_spec, out_spec] if emit_v else out_spec,
        out_shape=[out_shape, out_shape] if emit_v else out_shape,
        scratch_shapes=scratch,
        compiler_params=_cp(("parallel", "arbitrary")),
        name="rwkv7",
    )(*args)


def _merge_body(ga_ref, gb_ref, gc_ref, oa_ref, ob_ref, oc_ref, x_ref, gate_ref,
                wa_ref, wb_ref, wc_ref, wo_ref, o_ref):
    merged = (_sigmoid(ga_ref[...].astype(F32)) * _dot(oa_ref[...], wa_ref[...])
              + _sigmoid(gb_ref[...].astype(F32)) * _dot(ob_ref[...], wb_ref[...])
              + _sigmoid(gc_ref[...].astype(F32)) * _dot(oc_ref[...], wc_ref[...]))
    mix = _dot(merged.astype(BF16), wo_ref[...])
    o_ref[...] = x_ref[...] + gate_ref[...] * mix


def _merge_call(u2, o_a, o_b, o_c, x2, ada3, wa, wb, wc, wo, layer, bsz, seq):
    m, d = x2.shape
    tm = min(512, seq)
    per_b = seq // tm
    base = layer * bsz * 6
    wdt = o_a.shape[1]

    def const(arr):
        return pl.BlockSpec(arr.shape, lambda i: (0, 0))

    return pl.pallas_call(
        _merge_body,
        grid=(m // tm,),
        in_specs=[
            pl.BlockSpec((tm, d), lambda i: (i, 0)),
            pl.BlockSpec((tm, d), lambda i: (i, 1)),
            pl.BlockSpec((tm, d), lambda i: (i, 2)),
            pl.BlockSpec((tm, wdt), lambda i: (i, 0)),
            pl.BlockSpec((tm, wdt), lambda i: (i, 0)),
            pl.BlockSpec((tm, wdt), lambda i: (i, 0)),
            pl.BlockSpec((tm, d), lambda i: (i, 0)),
            pl.BlockSpec((None, 1, d), lambda i: (base + (i // per_b) * 6 + 2, 0, 0)),
            const(wa), const(wb), const(wc), const(wo),
        ],
        out_specs=pl.BlockSpec((tm, d), lambda i: (i, 0)),
        out_shape=jax.ShapeDtypeStruct((m, d), F32),
        compiler_params=_cp(("parallel",)),
        name="merge_out",
    )(u2, u2, u2, o_a, o_b, o_c, x2, ada3, wa, wb, wc, wo)


def _ffn_body(x_ref, nw_ref, shift_ref, scale_ref, gate_ref, wi_ref, wo_ref, fn_ref, o_ref, *, final):
    x = x_ref[...]
    h = _norm_mod(x, nw_ref[...], scale_ref[...], shift_ref[...]).astype(BF16)
    acc = jnp.zeros(x.shape, F32)
    lo = 0
    for width in FFN_CHUNKS:
        gh = _dot(h, wi_ref[:, lo:lo + width])
        uh = _dot(h, wi_ref[:, FFN_HIDDEN + lo:FFN_HIDDEN + lo + width])
        act = (gh * _sigmoid(gh) * uh).astype(BF16)
        acc = acc + _dot(act, wo_ref[lo:lo + width, :])
        lo += width
    y = x + gate_ref[...] * acc
    if final:
        ms = jnp.mean(y * y, axis=-1, keepdims=True)
        y = y * lax.rsqrt(ms + NORM_EPS) * fn_ref[...]
    o_ref[...] = y


def _ffn_call(x2, nw, ada3, wi, wo, final_w, layer, bsz, seq, final):
    m, d = x2.shape
    tm = min(512, seq)
    per_b = seq // tm
    base = layer * bsz * 6

    def ada_spec(k):
        return pl.BlockSpec((None, 1, d), lambda i: (base + (i // per_b) * 6 + k, 0, 0))

    def const(arr):
        return pl.BlockSpec(arr.shape, lambda i: (0, 0), pipeline_mode=pl.Buffered(1))

    return pl.pallas_call(
        functools.partial(_ffn_body, final=final),
        grid=(m // tm,),
        in_specs=[
            pl.BlockSpec((tm, d), lambda i: (i, 0)),
            pl.BlockSpec((1, d), lambda i: (0, 0)),
            ada_spec(3), ada_spec(4), ada_spec(5),
            const(wi), const(wo),
            pl.BlockSpec((1, d), lambda i: (0, 0)),
        ],
        out_specs=pl.BlockSpec((tm, d), lambda i: (i, 0)),
        out_shape=jax.ShapeDtypeStruct((m, d), F32),
        compiler_params=_cp(("parallel",)),
        name="ffn_final" if final else "ffn",
    )(x2, nw, ada3, ada3, ada3, wi, wo, final_w)


def _permute_w_in(w):
    def qk_lanes(t):
        t = t.reshape(t.shape[0], DA_HEADS, 2, 2, DA_QK_DIM // 2)
        return jnp.swapaxes(t, 2, 3).reshape(t.shape[0], DA_HEADS * 2 * DA_QK_DIM)

    q, k, v = qk_lanes(w[:, 0:512]), qk_lanes(w[:, 512:1024]), w[:, 1024:1536]
    hgrn = w[:, 1536:3584]
    rw = w[:, 3584:5376]
    gates = w[:, 5376:8448]
    return jnp.concatenate([gates, hgrn, q, k, v, rw], axis=1)


def kernel(x, c, positions, ada_w, ada_b, norm_mix_w, norm_ffn_w, w_in, da_lambda, da_subln_w, hg_lb, hg_norm_w, rw_mu, rw_w0, rw_w2, rw_a0, rw_a2, rw_g2, rw_k_k, rw_k_a, rw_r_k, rw_gn_w, rw_gn_b, rw_v0, rw_v1, rw_v2, w_branch_a, w_branch_b, w_branch_c, w_out, ffn_w_in, ffn_w_out, final_norm_w):
    bsz, seq, d = x.shape
    depth = ada_w.shape[0]
    m = bsz * seq

    ada = _ada_call(c, ada_w, ada_b)
    ada3 = ada.reshape(depth * bsz * 6, 1, d)
    cos, sin_s = _rope_call(positions)

    x2 = x.reshape(m, d)
    v_first = None
    for l in range(depth):
        w_l = _permute_w_in(w_in[l]).astype(BF16)
        u2 = _proj_in_call(x2, norm_mix_w[l].reshape(1, d), ada3, cos, sin_s, w_l, l, bsz, seq)
        u3 = u2.reshape(bsz, seq, IN_COLS)

        o_a = _attn_call(u3, da_lambda[l], da_subln_w[l], l)
        o_b = _hgrn_call(u3, hg_lb, hg_norm_w[l], l)
        vres = None if l == 0 else (v_first, rw_v0[l - 1], rw_v1[l - 1], rw_v2[l - 1])
        emit_v = l == 0 and depth > 1
        res = _rw_mix_call(u3, rw_mu[l], rw_w0[l], rw_w2[l], rw_a0[l], rw_a2[l], rw_g2[l], rw_k_k[l],
                           rw_k_a[l], vres, rw_r_k[l], rw_gn_w[l], rw_gn_b[l], emit_v)
        if emit_v:
            o_c, v_first = res
        else:
            o_c = res

        x2 = _merge_call(u2, o_a.reshape(m, -1), o_b.reshape(m, -1), o_c.reshape(m, -1), x2, ada3,
                         w_branch_a[l].astype(BF16), w_branch_b[l].astype(BF16),
                         w_branch_c[l].astype(BF16), w_out[l].astype(BF16), l, bsz, seq)
        x2 = _ffn_call(x2, norm_ffn_w[l].reshape(1, d), ada3, ffn_w_in[l].astype(BF16),
                       ffn_w_out[l].astype(BF16), final_norm_w.reshape(1, d), l, bsz, seq,
                       final=(l == depth - 1))
    return x2.reshape(bsz, seq, d)
```

```python
import functools
import math

import jax
import jax.numpy as jnp
from jax import lax
from jax.experimental import pallas as pl
from jax.experimental.pallas import tpu as pltpu

F32 = jnp.float32
BF16 = jnp.bfloat16

D_MODEL = 1024
CHUNK = 64
ROPE_THETA = 10000.0
NORM_EPS = 1e-6

DA_HEADS = 4
DA_QK_DIM = 64
DA_V_DIM = 128
HG_HEADS = 4
HG_DIM = 128
RW_HEADS = 8
RW_DIM = 64
RW_WIDTH = 512
RW_GN_EPS = 64e-5
FFN_HIDDEN = 2816
FFN_CHUNKS = (768, 768, 768, 512)
IN_COLS = 8448

LANES = 128
ATTN_LOCKSTEP = 4
LOG2E = 1.4426950408889634
Q_SCALE = DA_QK_DIM ** -0.5 * LOG2E
VT_ROWS = DA_V_DIM + 16
SUB = 16
RW_PACK = 4
RW_GW = RW_PACK * RW_DIM
RW_GROUPS = RW_HEADS // RW_PACK

COL_GATES = 0
COL_HF = 3072
COL_HI = 3584
COL_HQ = 4096
COL_HG = 4608
COL_Q = 5120
COL_K = 5632
COL_V = 6144
COL_RR = 6656
COL_RK = 7168
COL_RV = 7680
COL_RL = 8192
RW_LORA_W = 256
HEADW = 512
PROJ_TN = 2816
GROUPS_PER_TILE = PROJ_TN // LANES
ROPE_Q0 = COL_Q // LANES
ROPE_K0 = COL_K // LANES

VMEM_LIMIT = 56 * 1024 * 1024


def _cp(sem):
    return pltpu.CompilerParams(dimension_semantics=sem, vmem_limit_bytes=VMEM_LIMIT)


def _sigmoid(x):
    return 1.0 / (1.0 + jnp.exp(-x))


def _softplus(x):
    return jnp.maximum(x, 0.0) + jnp.log(1.0 + jnp.exp(-jnp.abs(x)))


def _dot(a, b):
    return jnp.dot(a, b, preferred_element_type=F32)


def _dot_nt(a, b):
    return lax.dot_general(a, b, (((1,), (1,)), ((), ())), preferred_element_type=F32)


def _dot_exact_lhs(a01, x):
    n = x.shape[1]
    hi = x.astype(BF16)
    lo = (x - hi.astype(F32)).astype(BF16)
    res = _dot(a01, jnp.concatenate([hi, lo], axis=1))
    return res[:, 0:n] + res[:, n:2 * n]


def _head_sum(x):
    ones_g = _head_ones(RW_GW, RW_DIM)
    xb = x.astype(BF16)
    return jnp.concatenate(
        [_dot(xb[:, grp * RW_GW:(grp + 1) * RW_GW], ones_g) for grp in range(x.shape[1] // RW_GW)], axis=1)


def _dot_hp(a, b):
    a_hi = a.astype(BF16)
    a_lo = (a - a_hi.astype(F32)).astype(BF16)
    b_hi = b.astype(BF16)
    b_lo = (b - b_hi.astype(F32)).astype(BF16)
    return _dot(a_hi, b_hi) + _dot(a_hi, b_lo) + _dot(a_lo, b_hi)


def _iota(shape, dim):
    return lax.broadcasted_iota(jnp.int32, shape, dim)


def _ada_body(c_ref, w_ref, b_ref, o_ref):
    c = c_ref[...]
    ca = c * _sigmoid(c)
    o_ref[...] = _dot_hp(ca, w_ref[...]) + b_ref[...]


def _ada_call(c, ada_w, ada_b):
    nl, d, n6 = ada_w.shape
    bsz = c.shape[0]
    tn = 1536
    return pl.pallas_call(
        _ada_body,
        grid=(nl, n6 // tn),
        in_specs=[
            pl.BlockSpec((bsz, d), lambda l, j: (0, 0)),
            pl.BlockSpec((None, d, tn), lambda l, j: (l, 0, j)),
            pl.BlockSpec((None, 1, tn), lambda l, j: (l, 0, j)),
        ],
        out_specs=pl.BlockSpec((None, bsz, tn), lambda l, j: (l, 0, j)),
        out_shape=jax.ShapeDtypeStruct((nl, bsz, n6), F32),
        compiler_params=_cp(("parallel", "parallel")),
        name="ada",
    )(c, ada_w, ada_b.reshape(nl, 1, n6))


def _rope_body(pos_ref, invf_ref, sgn_ref, cos_ref, sin_ref):
    ang = pos_ref[...].astype(F32) * invf_ref[...]
    cos_ref[...] = jnp.cos(ang)
    sin_ref[...] = jnp.sin(ang) * sgn_ref[...]


def _rope_call(positions):
    m = positions.size
    tm = min(2048, m)
    inv_freq = ROPE_THETA ** (-jnp.arange(0, DA_QK_DIM, 2, dtype=F32) / DA_QK_DIM)
    invf = jnp.tile(inv_freq, LANES // (DA_QK_DIM // 2)).reshape(1, LANES)
    sgn = jnp.where(jnp.arange(LANES) < LANES // 2, -1.0, 1.0).astype(F32).reshape(1, LANES)
    return pl.pallas_call(
        _rope_body,
        grid=(m // tm,),
        in_specs=[
            pl.BlockSpec((tm, 1), lambda i: (i, 0)),
            pl.BlockSpec((1, LANES), lambda i: (0, 0)),
            pl.BlockSpec((1, LANES), lambda i: (0, 0)),
        ],
        out_specs=[pl.BlockSpec((tm, LANES), lambda i: (i, 0))] * 2,
        out_shape=[jax.ShapeDtypeStruct((m, LANES), F32)] * 2,
        compiler_params=_cp(("parallel",)),
        name="rope_tables",
    )(positions.reshape(m, 1), invf, sgn)


def _norm_mod(x, nw, scale, shift):
    ms = jnp.mean(x * x, axis=-1, keepdims=True)
    y = x * lax.rsqrt(ms + NORM_EPS) * nw
    return y * (1.0 + scale) + shift


def _rope_kind(group):
    if ROPE_Q0 <= group < ROPE_Q0 + DA_HEADS:
        return "q"
    if ROPE_K0 <= group < ROPE_K0 + DA_HEADS:
        return "k"
    return None


_ROPE_TILES = sorted({g // GROUPS_PER_TILE for g in range(IN_COLS // LANES) if _rope_kind(g)})


def _proj_in_body(x_ref, nw_ref, shift_ref, scale_ref, cos_ref, sin_ref, w_ref, o_ref, h_ref):
    j = pl.program_id(1)

    @pl.when(j == 0)
    def _():
        h_ref[...] = _norm_mod(x_ref[...], nw_ref[...], scale_ref[...], shift_ref[...]).astype(BF16)

    for jt in _ROPE_TILES:
        @pl.when(j == jt)
        def _(jt=jt):
            acc = _dot(h_ref[...], w_ref[...])
            cos = cos_ref[...]
            sin_s = sin_ref[...]
            for g in range(GROUPS_PER_TILE):
                slab = acc[:, g * LANES:(g + 1) * LANES]
                kind = _rope_kind(jt * GROUPS_PER_TILE + g)
                if kind is not None:
                    slab = slab * cos + pltpu.roll(slab, LANES // 2, 1) * sin_s
                    if kind == "q":
                        slab = slab * Q_SCALE
                o_ref[:, g * LANES:(g + 1) * LANES] = slab.astype(o_ref.dtype)

    plain = j != _ROPE_TILES[0]
    for jt in _ROPE_TILES[1:]:
        plain = jnp.logical_and(plain, j != jt)

    @pl.when(plain)
    def _():
        o_ref[...] = _dot(h_ref[...], w_ref[...]).astype(o_ref.dtype)


def _proj_in_call(x2, nw, ada3, cos, sin_s, w_bf16, layer, bsz, seq):
    m, d = x2.shape
    n = w_bf16.shape[1]
    tm = min(1024, seq)
    per_b = seq // tm
    base = layer * bsz * 6

    def ada_spec(k):
        return pl.BlockSpec((None, 1, d), lambda i, j: (base + (i // per_b) * 6 + k, 0, 0))

    return pl.pallas_call(
        _proj_in_body,
        grid=(m // tm, n // PROJ_TN),
        in_specs=[
            pl.BlockSpec((tm, d), lambda i, j: (i, 0)),
            pl.BlockSpec((1, d), lambda i, j: (0, 0)),
            ada_spec(0),
            ada_spec(1),
            pl.BlockSpec((tm, LANES), lambda i, j: (i, 0)),
            pl.BlockSpec((tm, LANES), lambda i, j: (i, 0)),
            pl.BlockSpec((d, PROJ_TN), lambda i, j: (0, j)),
        ],
        out_specs=pl.BlockSpec((tm, PROJ_TN), lambda i, j: (i, j)),
        out_shape=jax.ShapeDtypeStruct((m, n), BF16),
        scratch_shapes=[pltpu.VMEM((tm, d), BF16)],
        compiler_params=_cp(("parallel", "arbitrary")),
        name="proj_in",
    )(x2, nw, ada3, ada3, cos, sin_s, w_bf16)


def _attn_body(lam_ref, q_ref, k_ref, v_ref, sw_ref, o_ref, vt_ref, *, tq, lam_init):
    i = pl.program_id(1)

    @pl.when(i == 0)
    def _():
        for h in range(DA_HEADS):
            vt_ref[h, 0:DA_V_DIM, :] = v_ref[:, h * LANES:(h + 1) * LANES].astype(F32).T.astype(BF16)
            vt_ref[h, DA_V_DIM:VT_ROWS, :] = jnp.ones((VT_ROWS - DA_V_DIM, vt_ref.shape[2]), BF16)

    lv = lam_ref[...]
    lam = (jnp.exp(jnp.sum(lv[0:1] * lv[1:2], axis=1, keepdims=True))
           - jnp.exp(jnp.sum(lv[2:3] * lv[3:4], axis=1, keepdims=True)) + lam_init)

    map0 = jnp.bitwise_and(_iota((1, LANES), 1), DA_QK_DIM - 1) < DA_QK_DIM // 2
    zero = jnp.zeros((), BF16)
    qs = []
    for h in range(DA_HEADS):
        q = q_ref[:, h * LANES:(h + 1) * LANES]
        q0 = jnp.where(map0, q, zero)
        q1 = jnp.where(map0, zero, q)
        qs.append(jnp.concatenate([q0, q1], axis=0))

    def mask_for(nk):
        k_chunk = jnp.right_shift(_iota((nk, 1), 0), 6)
        q_chunk = jnp.right_shift(jnp.bitwise_and(_iota((1, 2 * tq), 1), tq - 1), 6) + (nk - tq) // CHUNK
        return k_chunk <= q_chunk

    def step(start, nk, carry, allowed):
        out = [None] * DA_HEADS
        for g0 in range(0, DA_HEADS, ATTN_LOCKSTEP):
            grp = range(g0, g0 + ATTN_LOCKSTEP)
            ss = {h: _dot_nt(k_ref[pl.ds(start, nk), h * LANES:(h + 1) * LANES], qs[h]) for h in grp}
            ps, stats = {}, {}
            for h in grp:
                m_i, _ = carry[h]
                s = ss[h]
                if allowed is not None:
                    s = jnp.where(allowed, s, -jnp.inf)
                m_new = jnp.maximum(m_i, jnp.max(s, axis=0, keepdims=True))
                ps[h] = jnp.exp2(s - m_new).astype(BF16)
                stats[h] = (m_new, jnp.exp2(m_i - m_new))
            for h in grp:
                m_new, alpha = stats[h]
                pv = _dot(vt_ref[h, :, pl.ds(start, nk)], ps[h])
                out[h] = (m_new, alpha * carry[h][1] + pv)
        return tuple(out)

    init = tuple((jnp.full((1, 2 * tq), -jnp.inf, F32), jnp.zeros((VT_ROWS, 2 * tq), F32))
                 for _ in range(DA_HEADS))
    carry = lax.fori_loop(
        0, i // 2, lambda j, c: step(pl.multiple_of(j * (2 * tq), 2 * tq), 2 * tq, c, None), init)
    carry = lax.cond(
        i % 2 == 1,
        lambda c: step(pl.multiple_of((i - 1) * tq, tq), 2 * tq, c, mask_for(2 * tq)),
        lambda c: step(pl.multiple_of(i * tq, tq), tq, c, mask_for(tq)),
        carry)

    for h in range(DA_HEADS):
        acc = carry[h][1]
        o_t = acc[:DA_V_DIM] / acc[DA_V_DIM:DA_V_DIM + 1]
        o_t = o_t[:, :tq] - lam * o_t[:, tq:]
        ms = jnp.mean(o_t * o_t, axis=0, keepdims=True)
        o = (o_t * lax.rsqrt(ms + NORM_EPS)).T * sw_ref[...] * (1.0 - lam_init)
        o_ref[:, h * LANES:(h + 1) * LANES] = o.astype(o_ref.dtype)


def _attn_call(u3, lam_vecs, subln_w, layer):
    bsz, seq, _ = u3.shape
    tq = min(256, seq)
    lam_init = 0.8 - 0.6 * math.exp(-0.3 * layer)
    return pl.pallas_call(
        functools.partial(_attn_body, tq=tq, lam_init=lam_init),
        grid=(bsz, seq // tq),
        in_specs=[
            pl.BlockSpec((4, DA_QK_DIM), lambda b, i: (0, 0)),
            pl.BlockSpec((None, tq, HEADW), lambda b, i: (b, i, COL_Q // HEADW)),
            pl.BlockSpec((None, seq, HEADW), lambda b, i: (b, 0, COL_K // HEADW)),
            pl.BlockSpec((None, seq, HEADW), lambda b, i: (b, 0, COL_V // HEADW)),
            pl.BlockSpec((1, LANES), lambda b, i: (0, 0)),
        ],
        out_specs=pl.BlockSpec((None, tq, HEADW), lambda b, i: (b, i, 0)),
        out_shape=jax.ShapeDtypeStruct((bsz, seq, HEADW), BF16),
        scratch_shapes=[pltpu.VMEM((DA_HEADS, VT_ROWS, seq), BF16)],
        compiler_params=_cp(("parallel", "arbitrary")),
        name="diff_attn",
    )(lam_vecs, u3, u3, u3, subln_w.reshape(1, LANES))


def _hgrn_body(lbp_ref, z_ref, i_ref, q_ref, g_ref, nw_ref, o_ref, *, layer, nchunks):
    lbp = lbp_ref[...]
    e = jnp.exp(lbp - jnp.max(lbp, axis=0, keepdims=True))
    sm = e / jnp.sum(e, axis=0, keepdims=True)
    cs = sm[0:1]
    for t in range(1, layer + 1):
        cs = cs + sm[t:t + 1]
    lb = cs - sm[0:1]
    log_lb = jnp.log(lb)
    log1m = jnp.log1p(-lb)

    rr = _iota((CHUNK, CHUNK), 0)
    cc = _iota((CHUNK, CHUNK), 1)
    sub_mid = jnp.bitwise_and(rr, ~(SUB - 1)) + (SUB // 2 - 1)
    cs_mat = jnp.concatenate([(cc <= rr).astype(BF16), (cc <= sub_mid).astype(BF16)], axis=0)
    diag_ok = jnp.logical_and(cc <= rr, jnp.right_shift(cc, 4) == jnp.right_shift(rr, 4))
    nsub = CHUNK // SUB
    heads = range(HG_HEADS)

    def hcols(x, h):
        return x[:, h * HG_DIM:(h + 1) * HG_DIM]

    def chunk(c, states):
        sl = pl.ds(pl.multiple_of(c * CHUNK, CHUNK), CHUNK)
        z = z_ref[sl, :].astype(F32)
        y = log1m - _softplus(-z)
        mx = jnp.maximum(log_lb, y)
        lf = mx + jnp.log(1.0 + jnp.exp(-jnp.abs(log_lb - y)))
        key = (1.0 - lb) * _sigmoid(-z)
        qc = q_ref[sl, :].astype(F32)
        vc = i_ref[sl, :]
        vt = vc.astype(F32).T.astype(BF16)

        tot = _dot_exact_lhs(cs_mat, lf * LOG2E)
        b = tot[:CHUNK]
        d = b - tot[CHUNK:]
        b_last = b[CHUNK - 1:CHUNK, :]

        q_d = (qc * jnp.exp2(d)).astype(BF16)
        k_d = (key * jnp.exp2(-d)).astype(BF16)
        q_js, k_js = [], []
        for jb in range(nsub - 1):
            lo, hi = jb * SUB, (jb + 1) * SUB
            e_j = b[hi - 1:hi, :]
            q_j = (qc[hi:] * jnp.exp2(b[hi:] - e_j)).astype(BF16)
            k_j = (key[lo:hi] * jnp.exp2(e_j - b[lo:hi])).astype(BF16)
            q_js.append(jnp.concatenate([jnp.zeros((hi, HEADW), BF16), q_j], axis=0))
            pieces = [k_j, jnp.zeros((CHUNK - hi, HEADW), BF16)]
            if lo:
                pieces.insert(0, jnp.zeros((lo, HEADW), BF16))
            k_js.append(jnp.concatenate(pieces, axis=0))
        q_in = (qc * jnp.exp2(b)).astype(BF16)
        k_out = (key * jnp.exp2(b_last - b)).astype(BF16)
        decay = jnp.exp2(b_last)

        s_diag = [_dot_nt(hcols(q_d, h), hcols(k_d, h)) for h in heads]
        s_off = [_dot_nt(jnp.concatenate([hcols(x, h) for x in q_js], axis=1),
                         jnp.concatenate([hcols(x, h) for x in k_js], axis=1)) for h in heads]
        scores = [(jnp.where(diag_ok, s_diag[h], 0.0) + s_off[h]).astype(BF16) for h in heads]
        o_intra = [_dot(scores[h], hcols(vc, h)) for h in heads]
        o_inter = [_dot_nt(hcols(q_in, h), states[h].astype(BF16)) for h in heads]
        upd = [_dot(vt[h * HG_DIM:(h + 1) * HG_DIM, :], hcols(k_out, h)) for h in heads]
        new_states = tuple(states[h] * hcols(decay, h) + upd[h] for h in heads)

        outs = []
        for h in heads:
            o = o_intra[h] + o_inter[h]
            ms = jnp.mean(o * o, axis=-1, keepdims=True)
            outs.append(o * lax.rsqrt(ms + NORM_EPS))
        g = g_ref[sl, :].astype(F32)
        o = jnp.concatenate(outs, axis=1) * nw_ref[...] * (g * _sigmoid(g))
        o_ref[sl, :] = o.astype(o_ref.dtype)
        return new_states

    init = tuple(jnp.zeros((HG_DIM, HG_DIM), F32) for _ in heads)
    lax.fori_loop(0, nchunks, chunk, init, unroll=16)


def _hgrn_call(u3, hg_lb, norm_w, layer):
    bsz, seq, _ = u3.shape
    nl = hg_lb.shape[0]

    def col(base):
        return lambda b: (b, 0, base // HEADW)

    blk = (None, seq, HEADW)
    return pl.pallas_call(
        functools.partial(_hgrn_body, layer=layer, nchunks=seq // CHUNK),
        grid=(bsz,),
        in_specs=[
            pl.BlockSpec((nl, HEADW), lambda b: (0, 0)),
            pl.BlockSpec(blk, col(COL_HF)),
            pl.BlockSpec(blk, col(COL_HI)),
            pl.BlockSpec(blk, col(COL_HQ)),
            pl.BlockSpec(blk, col(COL_HG)),
            pl.BlockSpec((1, HEADW), lambda b: (0, 0)),
        ],
        out_specs=pl.BlockSpec(blk, lambda b: (b, 0, 0)),
        out_shape=jax.ShapeDtypeStruct((bsz, seq, HEADW), BF16),
        compiler_params=_cp(("parallel",)),
        name="hgrn2",
    )(hg_lb, u3, u3, u3, u3, jnp.tile(norm_w, HG_HEADS).reshape(1, HEADW))


def _head_ones(n, width):
    r = _iota((n, n), 0) // width
    c = _iota((n, n), 1) // width
    return (r == c).astype(BF16)


def _rw_prep(cur, prev, prm, vres, has_prev):
    mu_ref, w0_ref, w2_ref, a0_ref, a2_ref, g2_ref, kk_ref, ka_ref = prm
    ts = cur[0].shape[0]
    row = _iota((ts, 1), 0)

    def shifted(idx, lo, hi):
        u = cur[idx][...].astype(F32)
        if prev is None:
            prev_row = jnp.zeros((1, u.shape[1]), F32)
        else:
            prev_row = prev[idx][SUB - 1:SUB, :].astype(F32) * has_prev.astype(F32)
        u_prev = jnp.where(row == 0, prev_row, pltpu.roll(u, 1, 0))
        return u + (u_prev - u) * mu_ref[:, lo:hi]

    r = shifted(0, 0, 512)
    k = shifted(1, 512, 1024)
    v = shifted(2, 1024, 1536)
    lora = shifted(3, 1536, 1792)
    w_lo = lora[:, 0:64]
    a_lo = lora[:, 64:128]
    g_lo = lora[:, 128:256]

    def lora_dot(act, w_ref):
        return _dot(act.astype(BF16), w_ref[...].astype(BF16))

    wpre = w0_ref[...] + lora_dot(jnp.tanh(w_lo), w2_ref)
    w = -_softplus(-wpre) - 0.5
    lw = -jnp.exp(w)
    a = _sigmoid(a0_ref[...] + lora_dot(a_lo, a2_ref))
    g = lora_dot(_sigmoid(g_lo), g2_ref)
    if vres is not None:
        vf_ref, v0_ref, v1_ref, v2_ref = vres
        mix = _sigmoid(v0_ref[...] + lora_dot(lora_dot(v, v1_ref), v2_ref))
        v = v + (vf_ref[...].astype(F32) - v) * mix

    kk = k * kk_ref[...]
    ssq = _head_sum(kk * kk)
    kkn = kk * lax.rsqrt(jnp.maximum(ssq, 1e-24))
    k = k * (1.0 + (a - 1.0) * ka_ref[...])
    return r, k, v, lw, kkn, kkn * a, g


def _rw_scan(r_ref, k_ref, v_ref, lw_ref, kk_ref, bk_ref, g_ref, rk_ref, gw_ref, gb_ref,
             o_ref, state_ref, oacc_ref, nsub):
    t = CHUNK
    gw = RW_GW
    tc = nsub * t
    units = [(s, grp) for s in range(nsub) for grp in range(RW_GROUPS)]

    rr = _iota((tc, tc), 0)
    cc = _iota((tc, tc), 1)
    tril_bd = jnp.logical_and(cc <= rr, (cc // t) == (rr // t)).astype(BF16)
    rr4 = _iota((t, gw), 0)
    cj = jnp.bitwise_and(_iota((t, gw), 1), t - 1)
    strict = cj < rr4
    incl = cj <= rr4
    eye_c = (cj == rr4).astype(F32)
    bd_mask = (_iota((gw, gw), 0) // t) == (_iota((gw, gw), 1) // RW_DIM)
    zero_b = jnp.zeros((), BF16)

    left_head = _iota((t, LANES), 1) < RW_DIM
    zeros_tile = jnp.zeros((t, LANES), BF16)

    def bd(xc):
        xb = xc.astype(BF16)
        rows = []
        for h in range(RW_PACK):
            tile = xb[:, (h // 2) * LANES:(h // 2 + 1) * LANES]
            kept = jnp.where(left_head, tile, zero_b) if h % 2 == 0 else jnp.where(left_head, zero_b, tile)
            rows.append(jnp.concatenate([kept, zeros_tile] if h < 2 else [zeros_tile, kept], axis=1))
        return jnp.concatenate(rows, axis=0)

    def cut(x, u):
        s, grp = u
        return x[s * t:(s + 1) * t, grp * gw:(grp + 1) * gw]

    r = r_ref[...].astype(F32)
    k = k_ref[...].astype(F32)
    v = v_ref[...].astype(F32)
    lw = lw_ref[...] * LOG2E
    kk = kk_ref[...].astype(F32)
    bk = bk_ref[...].astype(F32)
    c = _dot_exact_lhs(tril_bd, lw)
    e_neg = jnp.exp2(-c)
    a_t = -kk * jnp.exp2(c - lw)
    b_t = bk * e_neg
    k_t = k * e_neg
    r_t = r * jnp.exp2(c)
    c_last = jnp.concatenate(
        [jnp.broadcast_to(c[(s + 1) * t - 1:(s + 1) * t, :], (t, RW_WIDTH)) for s in range(nsub)], axis=0)
    decay_out = jnp.exp2(c_last - c)
    b_g = bk * decay_out
    k_g = k * decay_out
    gamma = jnp.exp2(c_last)

    ar = {u: jnp.concatenate([cut(a_t, u), cut(r_t, u)], axis=0).astype(BF16) for u in units}
    x_bk = {u: _dot_nt(ar[u], jnp.concatenate([bd(cut(b_t, u)), bd(cut(k_t, u))], axis=0)) for u in units}
    l_c = {u: jnp.where(strict, x_bk[u][:t, :gw], 0.0) for u in units}
    m_c = {u: jnp.where(strict, x_bk[u][:t, gw:], 0.0).astype(BF16) for u in units}
    rb = {u: jnp.where(incl, x_bk[u][t:, :gw], 0.0).astype(BF16) for u in units}
    rkm = {u: jnp.where(incl, x_bk[u][t:, gw:], 0.0).astype(BF16) for u in units}
    v_bd = {u: bd(cut(v, u)) for u in units}

    p = {u: _dot(l_c[u].astype(BF16), bd(l_c[u])) for u in units}
    s_inv = {u: eye_c + l_c[u] for u in units}
    mo = {u: _dot(jnp.concatenate([m_c[u], rkm[u]], axis=0), v_bd[u]) for u in units}
    mv = {u: mo[u][:t] for u in units}
    o_kv = {u: mo[u][t:] for u in units}
    for lvl in range(1, 6):
        last = lvl == 5
        nxt_p, nxt_s = {}, {}
        for u in units:
            if last:
                nxt_s[u] = s_inv[u] + _dot(s_inv[u].astype(BF16), bd(p[u]))
            else:
                res = _dot(jnp.concatenate([s_inv[u], p[u]], axis=0).astype(BF16), bd(p[u]))
                nxt_s[u] = s_inv[u] + res[:t]
                nxt_p[u] = res[t:]
        p, s_inv = nxt_p, nxt_s

    rbs = {u: _dot(rb[u], bd(s_inv[u])) for u in units}
    wu = {u: _dot(jnp.concatenate([s_inv[u], rbs[u]], axis=0).astype(BF16),
                  jnp.concatenate([bd(cut(a_t, u)), bd(mv[u])], axis=1)) for u in units}
    w_t = {u: wu[u][:t, :gw] for u in units}
    u_t = {u: wu[u][:t, gw:] for u in units}
    r_hat = {u: (cut(r_t, u) + wu[u][t:, :gw]).astype(BF16) for u in units}
    o_hat = {u: wu[u][t:, gw:] + o_kv[u] for u in units}
    g_mat = {u: jnp.where(bd_mask, _dot(w_t[u].T.astype(BF16), cut(b_g, u).astype(BF16)), 0.0).astype(BF16)
             for u in units}
    c_mat = {}
    for u in units:
        uv_t = jnp.concatenate([u_t[u], cut(v, u)], axis=0).T.astype(BF16)
        bkg = jnp.concatenate([cut(b_g, u), cut(k_g, u)], axis=0).astype(BF16)
        full = jnp.where(bd_mask, _dot(uv_t, bkg), 0.0)
        c_mat[u] = full[0:t] + full[t:2 * t] + full[2 * t:3 * t] + full[3 * t:4 * t]

    for s in range(nsub):
        for grp in range(RW_GROUPS):
            u = (s, grp)
            st = state_ref[grp]
            st_b = st.astype(BF16)
            oacc_ref[s * t:(s + 1) * t, grp * gw:(grp + 1) * gw] = _dot_nt(r_hat[u], bd(st_b)) + o_hat[u]
            g_row = gamma[(s + 1) * t - 1:(s + 1) * t, grp * gw:(grp + 1) * gw]
            state_ref[grp] = st * g_row + _dot(st_b, g_mat[u]) + c_mat[u]

    o = oacc_ref[...]
    mean = _head_sum(o) * (1.0 / RW_DIM)
    dlt = o - mean
    var = _head_sum(dlt * dlt) * (1.0 / RW_DIM)
    on = dlt * lax.rsqrt(var + RW_GN_EPS) * gw_ref[...] + gb_ref[...]
    bonus = _head_sum(r * k * rk_ref[...])
    o_ref[...] = ((on + bonus * v) * g_ref[...]).astype(o_ref.dtype)


RW_SLOT_DTYPES = (BF16, BF16, BF16, F32, BF16, BF16, BF16)


def _rw_mix_body(*refs, has_vres, emit_v, nsub):
    nxt, nxt_prev, first, prm = refs[0:4], refs[4:8], refs[8:12], refs[12:20]
    pos = 20
    vres_nxt = vres_first = None
    if has_vres:
        vf_nxt, vf_first, v0, v1, v2 = refs[pos:pos + 5]
        vres_nxt, vres_first = (vf_nxt, v0, v1, v2), (vf_first, v0, v1, v2)
        pos += 5
    rk_ref, gw_ref, gb_ref = refs[pos:pos + 3]
    pos += 3
    o_ref = refs[pos]
    pos += 1
    v_out = None
    if emit_v:
        v_out = refs[pos]
        pos += 1
    state_ref, oacc_ref = refs[pos:pos + 2]
    slots = refs[pos + 2:pos + 9]

    j = pl.program_id(1)
    last = pl.num_programs(1) - 1

    def put(slot, vals):
        for ref, val in zip(slots, vals):
            ref[slot] = val.astype(ref.dtype)

    @pl.when(j == 0)
    def _():
        state_ref[...] = jnp.zeros_like(state_ref)
        put(0, _rw_prep(first, None, prm, vres_first, None))

    put((j + 1) % 2, _rw_prep(nxt, nxt_prev, prm, vres_nxt, jnp.minimum(j + 1, last) > 0))

    cur = [ref.at[j % 2] for ref in slots]
    _rw_scan(*cur, rk_ref, gw_ref, gb_ref, o_ref, state_ref, oacc_ref, nsub)
    if emit_v:
        v_out[...] = cur[2][...]


def _rw_mix_call(u3, mu, w0, w2, a0, a2, g2, k_k, k_a, vres, r_k, gn_w, gn_b, emit_v):
    bsz, seq, _ = u3.shape
    tc = min(256, seq)
    n = seq // tc
    w = RW_WIDTH
    has_vres = vres is not None

    def full(arr):
        return pl.BlockSpec(arr.shape, lambda b, j: (0,) * arr.ndim)

    def nxt(col, width):
        return pl.BlockSpec((None, tc, width), lambda b, j: (b, jnp.minimum(j + 1, n - 1), col // width))

    def nxt_prev(col, width):
        return pl.BlockSpec(
            (None, SUB, width),
            lambda b, j: (b, jnp.maximum(jnp.minimum(j + 1, n - 1) * (tc // SUB) - 1, 0), col // width))

    def first(col, width):
        return pl.BlockSpec((None, tc, width), lambda b, j: (b, 0, col // width))

    cols = [(COL_RR, w), (COL_RK, w), (COL_RV, w), (COL_RL, RW_LORA_W)]
    params = [mu.reshape(1, -1), w0.reshape(1, w), w2, a0.reshape(1, w), a2, g2,
              k_k.reshape(1, w), k_a.reshape(1, w)]
    in_specs = [f(c, wd) for f in (nxt, nxt_prev, first) for c, wd in cols] + [full(p) for p in params]
    args = [u3] * 12 + params
    if has_vres:
        v_first, v0, v1, v2 = vres
        extra = [v0.reshape(1, w), v1, v2]
        in_specs += [pl.BlockSpec((None, tc, w), lambda b, j: (b, jnp.minimum(j + 1, n - 1), 0)),
                     pl.BlockSpec((None, tc, w), lambda b, j: (b, 0, 0))] + [full(p) for p in extra]
        args += [v_first, v_first] + extra
    tail = [r_k.reshape(1, w), gn_w.reshape(1, w), gn_b.reshape(1, w)]
    in_specs += [full(p) for p in tail]
    args += tail

    out_spec = pl.BlockSpec((None, tc, w), lambda b, j: (b, j, 0))
    out_shape = jax.ShapeDtypeStruct((bsz, seq, w), BF16)
    scratch = [pltpu.VMEM((RW_GROUPS, RW_DIM, RW_GW), F32), pltpu.VMEM((tc, w), F32)]
    scratch += [pltpu.VMEM((2, tc, w), dt) for dt in RW_SLOT_DTYPES]
    return pl.pallas_call(
        functools.partial(_rw_mix_body, has_vres=has_vres, emit_v=emit_v, nsub=tc // CHUNK),
        grid=(bsz, n),
        in_specs=in_specs,
        out_specs=[out_spec, out_spec] if emit_v else out_spec,
        out_shape=[out_shape, out_shape] if emit_v else out_shape,
        scratch_shapes=scratch,
        compiler_params=_cp(("parallel", "arbitrary")),
        name="rwkv7",
    )(*args)


def _merge_body(ga_ref, gb_ref, gc_ref, oa_ref, ob_ref, oc_ref, x_ref, gate_ref,
                wa_ref, wb_ref, wc_ref, wo_ref, o_ref):
    merged = (_sigmoid(ga_ref[...].astype(F32)) * _dot(oa_ref[...], wa_ref[...])
              + _sigmoid(gb_ref[...].astype(F32)) * _dot(ob_ref[...], wb_ref[...])
              + _sigmoid(gc_ref[...].astype(F32)) * _dot(oc_ref[...], wc_ref[...]))
    mix = _dot(merged.astype(BF16), wo_ref[...])
    o_ref[...] = x_ref[...] + gate_ref[...] * mix


def _merge_call(u2, o_a, o_b, o_c, x2, ada3, wa, wb, wc, wo, layer, bsz, seq):
    m, d = x2.shape
    tm = min(512, seq)
    per_b = seq // tm
    base = layer * bsz * 6
    wdt = o_a.shape[1]

    def const(arr):
        return pl.BlockSpec(arr.shape, lambda i: (0, 0))

    return pl.pallas_call(
        _merge_body,
        grid=(m // tm,),
        in_specs=[
            pl.BlockSpec((tm, d), lambda i: (i, 0)),
            pl.BlockSpec((tm, d), lambda i: (i, 1)),
            pl.BlockSpec((tm, d), lambda i: (i, 2)),
            pl.BlockSpec((tm, wdt), lambda i: (i, 0)),
            pl.BlockSpec((tm, wdt), lambda i: (i, 0)),
            pl.BlockSpec((tm, wdt), lambda i: (i, 0)),
            pl.BlockSpec((tm, d), lambda i: (i, 0)),
            pl.BlockSpec((None, 1, d), lambda i: (base + (i // per_b) * 6 + 2, 0, 0)),
            const(wa), const(wb), const(wc), const(wo),
        ],
        out_specs=pl.BlockSpec((tm, d), lambda i: (i, 0)),
        out_shape=jax.ShapeDtypeStruct((m, d), F32),
        compiler_params=_cp(("parallel",)),
        name="merge_out",
    )(u2, u2, u2, o_a, o_b, o_c, x2, ada3, wa, wb, wc, wo)


def _ffn_body(x_ref, nw_ref, shift_ref, scale_ref, gate_ref, wi_ref, wo_ref, fn_ref, o_ref, *, final):
    x = x_ref[...]
    h = _norm_mod(x, nw_ref[...], scale_ref[...], shift_ref[...]).astype(BF16)
    acc = jnp.zeros(x.shape, F32)
    lo = 0
    for width in FFN_CHUNKS:
        gh = _dot(h, wi_ref[:, lo:lo + width])
        uh = _dot(h, wi_ref[:, FFN_HIDDEN + lo:FFN_HIDDEN + lo + width])
        act = (gh * _sigmoid(gh) * uh).astype(BF16)
        acc = acc + _dot(act, wo_ref[lo:lo + width, :])
        lo += width
    y = x + gate_ref[...] * acc
    if final:
        ms = jnp.mean(y * y, axis=-1, keepdims=True)
        y = y * lax.rsqrt(ms + NORM_EPS) * fn_ref[...]
    o_ref[...] = y


def _ffn_call(x2, nw, ada3, wi, wo, final_w, layer, bsz, seq, final):
    m, d = x2.shape
    tm = min(512, seq)
    per_b = seq // tm
    base = layer * bsz * 6

    def ada_spec(k):
        return pl.BlockSpec((None, 1, d), lambda i: (base + (i // per_b) * 6 + k, 0, 0))

    def const(arr):
        return pl.BlockSpec(arr.shape, lambda i: (0, 0), pipeline_mode=pl.Buffered(1))

    return pl.pallas_call(
        functools.partial(_ffn_body, final=final),
        grid=(m // tm,),
        in_specs=[
            pl.BlockSpec((tm, d), lambda i: (i, 0)),
            pl.BlockSpec((1, d), lambda i: (0, 0)),
            ada_spec(3), ada_spec(4), ada_spec(5),
            const(wi), const(wo),
            pl.BlockSpec((1, d), lambda i: (0, 0)),
        ],
        out_specs=pl.BlockSpec((tm, d), lambda i: (i, 0)),
        out_shape=jax.ShapeDtypeStruct((m, d), F32),
        compiler_params=_cp(("parallel",)),
        name="ffn_final" if final else "ffn",
    )(x2, nw, ada3, ada3, ada3, wi, wo, final_w)


def _permute_w_in(w):
    def qk_lanes(t):
        t = t.reshape(t.shape[0], DA_HEADS, 2, 2, DA_QK_DIM // 2)
        return jnp.swapaxes(t, 2, 3).reshape(t.shape[0], DA_HEADS * 2 * DA_QK_DIM)

    q, k, v = qk_lanes(w[:, 0:512]), qk_lanes(w[:, 512:1024]), w[:, 1024:1536]
    hgrn = w[:, 1536:3584]
    rw = w[:, 3584:5376]
    gates = w[:, 5376:8448]
    return jnp.concatenate([gates, hgrn, q, k, v, rw], axis=1)


def kernel(x, c, positions, ada_w, ada_b, norm_mix_w, norm_ffn_w, w_in, da_lambda, da_subln_w, hg_lb, hg_norm_w, rw_mu, rw_w0, rw_w2, rw_a0, rw_a2, rw_g2, rw_k_k, rw_k_a, rw_r_k, rw_gn_w, rw_gn_b, rw_v0, rw_v1, rw_v2, w_branch_a, w_branch_b, w_branch_c, w_out, ffn_w_in, ffn_w_out, final_norm_w):
    bsz, seq, d = x.shape
    depth = ada_w.shape[0]
    m = bsz * seq

    ada = _ada_call(c, ada_w, ada_b)
    ada3 = ada.reshape(depth * bsz * 6, 1, d)
    cos, sin_s = _rope_call(positions)

    x2 = x.reshape(m, d)
    v_first = None
    for l in range(depth):
        w_l = _permute_w_in(w_in[l]).astype(BF16)
        u2 = _proj_in_call(x2, norm_mix_w[l].reshape(1, d), ada3, cos, sin_s, w_l, l, bsz, seq)
        u3 = u2.reshape(bsz, seq, IN_COLS)

        o_a = _attn_call(u3, da_lambda[l], da_subln_w[l], l)
        o_b = _hgrn_call(u3, hg_lb, hg_norm_w[l], l)
        vres = None if l == 0 else (v_first, rw_v0[l - 1], rw_v1[l - 1], rw_v2[l - 1])
        emit_v = l == 0 and depth > 1
        res = _rw_mix_call(u3, rw_mu[l], rw_w0[l], rw_w2[l], rw_a0[l], rw_a2[l], rw_g2[l], rw_k_k[l],
                           rw_k_a[l], vres, rw_r_k[l], rw_gn_w[l], rw_gn_b[l], emit_v)
        if emit_v:
            o_c, v_first = res
        else:
            o_c = res

        x2 = _merge_call(u2, o_a.reshape(m, -1), o_b.reshape(m, -1), o_c.reshape(m, -1), x2, ada3,
                         w_branch_a[l].astype(BF16), w_branch_b[l].astype(BF16),
                         w_branch_c[l].astype(BF16), w_out[l].astype(BF16), l, bsz, seq)
        x2 = _ffn_call(x2, norm_ffn_w[l].reshape(1, d), ada3, ffn_w_in[l].astype(BF16),
                       ffn_w_out[l].astype(BF16), final_norm_w.reshape(1, d), l, bsz, seq,
                       final=(l == depth - 1))
    return x2.reshape(bsz, seq, d)
```

```python
import functools
import math

import jax
import jax.numpy as jnp
from jax import lax
from jax.experimental import pallas as pl
from jax.experimental.pallas import tpu as pltpu

F32 = jnp.float32
BF16 = jnp.bfloat16

D_MODEL = 1024
CHUNK = 64
ROPE_THETA = 10000.0
NORM_EPS = 1e-6

DA_HEADS = 4
DA_QK_DIM = 64
DA_V_DIM = 128
HG_HEADS = 4
HG_DIM = 128
RW_HEADS = 8
RW_DIM = 64
RW_WIDTH = 512
RW_GN_EPS = 64e-5
FFN_HIDDEN = 2816
FFN_CHUNKS = (768, 768, 768, 512)
IN_COLS = 8448

LANES = 128
ATTN_LOCKSTEP = 4
LOG2E = 1.4426950408889634
Q_SCALE = DA_QK_DIM ** -0.5 * LOG2E
VT_ROWS = DA_V_DIM + 16
SUB = 16
RW_PACK = 4
RW_GW = RW_PACK * RW_DIM
RW_GROUPS = RW_HEADS // RW_PACK

COL_GATES = 0
COL_HF = 3072
COL_HI = 3584
COL_HQ = 4096
COL_HG = 4608
COL_Q = 5120
COL_K = 5632
COL_V = 6144
COL_RR = 6656
COL_RK = 7168
COL_RV = 7680
COL_RL = 8192
RW_LORA_W = 256
HEADW = 512
PROJ_TN = 4224
GROUPS_PER_TILE = PROJ_TN // LANES
ROPE_Q0 = COL_Q // LANES
ROPE_K0 = COL_K // LANES

VMEM_LIMIT = 56 * 1024 * 1024


def _cp(sem):
    return pltpu.CompilerParams(dimension_semantics=sem, vmem_limit_bytes=VMEM_LIMIT)


def _sigmoid(x):
    return 1.0 / (1.0 + jnp.exp(-x))


def _softplus(x):
    return jnp.maximum(x, 0.0) + jnp.log(1.0 + jnp.exp(-jnp.abs(x)))


def _dot(a, b):
    return jnp.dot(a, b, preferred_element_type=F32)


def _dot_nt(a, b):
    return lax.dot_general(a, b, (((1,), (1,)), ((), ())), preferred_element_type=F32)


def _dot_exact_lhs(a01, x):
    n = x.shape[1]
    hi = x.astype(BF16)
    lo = (x - hi.astype(F32)).astype(BF16)
    res = _dot(a01, jnp.concatenate([hi, lo], axis=1))
    return res[:, 0:n] + res[:, n:2 * n]


def _head_sum(x):
    ones_g = _head_ones(RW_GW, RW_DIM)
    xb = x.astype(BF16)
    return jnp.concatenate(
        [_dot(xb[:, grp * RW_GW:(grp + 1) * RW_GW], ones_g) for grp in range(x.shape[1] // RW_GW)], axis=1)


def _dot_hp(a, b):
    a_hi = a.astype(BF16)
    a_lo = (a - a_hi.astype(F32)).astype(BF16)
    b_hi = b.astype(BF16)
    b_lo = (b - b_hi.astype(F32)).astype(BF16)
    return _dot(a_hi, b_hi) + _dot(a_hi, b_lo) + _dot(a_lo, b_hi)


def _iota(shape, dim):
    return lax.broadcasted_iota(jnp.int32, shape, dim)


def _ada_body(c_ref, w_ref, b_ref, o_ref):
    c = c_ref[...]
    ca = c * _sigmoid(c)
    o_ref[...] = _dot_hp(ca, w_ref[...]) + b_ref[...]


def _ada_call(c, ada_w, ada_b):
    nl, d, n6 = ada_w.shape
    bsz = c.shape[0]
    tn = 1536
    return pl.pallas_call(
        _ada_body,
        grid=(nl, n6 // tn),
        in_specs=[
            pl.BlockSpec((bsz, d), lambda l, j: (0, 0)),
            pl.BlockSpec((None, d, tn), lambda l, j: (l, 0, j)),
            pl.BlockSpec((None, 1, tn), lambda l, j: (l, 0, j)),
        ],
        out_specs=pl.BlockSpec((None, bsz, tn), lambda l, j: (l, 0, j)),
        out_shape=jax.ShapeDtypeStruct((nl, bsz, n6), F32),
        compiler_params=_cp(("parallel", "parallel")),
        name="ada",
    )(c, ada_w, ada_b.reshape(nl, 1, n6))


def _rope_body(pos_ref, invf_ref, sgn_ref, cos_ref, sin_ref):
    ang = pos_ref[...].astype(F32) * invf_ref[...]
    cos_ref[...] = jnp.cos(ang)
    sin_ref[...] = jnp.sin(ang) * sgn_ref[...]


def _rope_call(positions):
    m = positions.size
    tm = min(2048, m)
    inv_freq = ROPE_THETA ** (-jnp.arange(0, DA_QK_DIM, 2, dtype=F32) / DA_QK_DIM)
    invf = jnp.tile(inv_freq, LANES // (DA_QK_DIM // 2)).reshape(1, LANES)
    sgn = jnp.where(jnp.arange(LANES) < LANES // 2, -1.0, 1.0).astype(F32).reshape(1, LANES)
    return pl.pallas_call(
        _rope_body,
        grid=(m // tm,),
        in_specs=[
            pl.BlockSpec((tm, 1), lambda i: (i, 0)),
            pl.BlockSpec((1, LANES), lambda i: (0, 0)),
            pl.BlockSpec((1, LANES), lambda i: (0, 0)),
        ],
        out_specs=[pl.BlockSpec((tm, LANES), lambda i: (i, 0))] * 2,
        out_shape=[jax.ShapeDtypeStruct((m, LANES), F32)] * 2,
        compiler_params=_cp(("parallel",)),
        name="rope_tables",
    )(positions.reshape(m, 1), invf, sgn)


def _norm_mod(x, nw, scale, shift):
    ms = jnp.mean(x * x, axis=-1, keepdims=True)
    y = x * lax.rsqrt(ms + NORM_EPS) * nw
    return y * (1.0 + scale) + shift


def _rope_kind(group):
    if ROPE_Q0 <= group < ROPE_Q0 + DA_HEADS:
        return "q"
    if ROPE_K0 <= group < ROPE_K0 + DA_HEADS:
        return "k"
    return None


_ROPE_TILES = sorted({g // GROUPS_PER_TILE for g in range(IN_COLS // LANES) if _rope_kind(g)})


def _proj_in_body(x_ref, nw_ref, shift_ref, scale_ref, cos_ref, sin_ref, w_ref, o_ref, h_ref):
    j = pl.program_id(1)

    @pl.when(j == 0)
    def _():
        h_ref[...] = _norm_mod(x_ref[...], nw_ref[...], scale_ref[...], shift_ref[...]).astype(BF16)

    for jt in _ROPE_TILES:
        @pl.when(j == jt)
        def _(jt=jt):
            acc = _dot(h_ref[...], w_ref[...])
            cos = cos_ref[...]
            sin_s = sin_ref[...]
            for g in range(GROUPS_PER_TILE):
                slab = acc[:, g * LANES:(g + 1) * LANES]
                kind = _rope_kind(jt * GROUPS_PER_TILE + g)
                if kind is not None:
                    slab = slab * cos + pltpu.roll(slab, LANES // 2, 1) * sin_s
                    if kind == "q":
                        slab = slab * Q_SCALE
                o_ref[:, g * LANES:(g + 1) * LANES] = slab.astype(o_ref.dtype)

    plain = j != _ROPE_TILES[0]
    for jt in _ROPE_TILES[1:]:
        plain = jnp.logical_and(plain, j != jt)

    @pl.when(plain)
    def _():
        o_ref[...] = _dot(h_ref[...], w_ref[...]).astype(o_ref.dtype)


def _proj_in_call(x2, nw, ada3, cos, sin_s, w_bf16, layer, bsz, seq):
    m, d = x2.shape
    n = w_bf16.shape[1]
    tm = min(1024, seq)
    per_b = seq // tm
    base = layer * bsz * 6

    def ada_spec(k):
        return pl.BlockSpec((None, 1, d), lambda i, j: (base + (i // per_b) * 6 + k, 0, 0))

    return pl.pallas_call(
        _proj_in_body,
        grid=(m // tm, n // PROJ_TN),
        in_specs=[
            pl.BlockSpec((tm, d), lambda i, j: (i, 0)),
            pl.BlockSpec((1, d), lambda i, j: (0, 0)),
            ada_spec(0),
            ada_spec(1),
            pl.BlockSpec((tm, LANES), lambda i, j: (i, 0)),
            pl.BlockSpec((tm, LANES), lambda i, j: (i, 0)),
            pl.BlockSpec((d, PROJ_TN), lambda i, j: (0, j)),
        ],
        out_specs=pl.BlockSpec((tm, PROJ_TN), lambda i, j: (i, j)),
        out_shape=jax.ShapeDtypeStruct((m, n), BF16),
        scratch_shapes=[pltpu.VMEM((tm, d), BF16)],
        compiler_params=_cp(("parallel", "arbitrary")),
        name="proj_in",
    )(x2, nw, ada3, ada3, cos, sin_s, w_bf16)


def _attn_body(lam_ref, q_ref, k_ref, v_ref, sw_ref, o_ref, vt_ref, *, tq, lam_init):
    i = pl.program_id(1)

    @pl.when(i == 0)
    def _():
        for h in range(DA_HEADS):
            vt_ref[h, 0:DA_V_DIM, :] = v_ref[:, h * LANES:(h + 1) * LANES].astype(F32).T.astype(BF16)
            vt_ref[h, DA_V_DIM:VT_ROWS, :] = jnp.ones((VT_ROWS - DA_V_DIM, vt_ref.shape[2]), BF16)

    lv = lam_ref[...]
    lam = (jnp.exp(jnp.sum(lv[0:1] * lv[1:2], axis=1, keepdims=True))
           - jnp.exp(jnp.sum(lv[2:3] * lv[3:4], axis=1, keepdims=True)) + lam_init)

    map0 = jnp.bitwise_and(_iota((1, LANES), 1), DA_QK_DIM - 1) < DA_QK_DIM // 2
    zero = jnp.zeros((), BF16)
    qs = []
    for h in range(DA_HEADS):
        q = q_ref[:, h * LANES:(h + 1) * LANES]
        q0 = jnp.where(map0, q, zero)
        q1 = jnp.where(map0, zero, q)
        qs.append(jnp.concatenate([q0, q1], axis=0))

    def mask_for(nk):
        k_chunk = jnp.right_shift(_iota((nk, 1), 0), 6)
        q_chunk = jnp.right_shift(jnp.bitwise_and(_iota((1, 2 * tq), 1), tq - 1), 6) + (nk - tq) // CHUNK
        return k_chunk <= q_chunk

    def step(start, nk, carry, allowed):
        out = [None] * DA_HEADS
        for g0 in range(0, DA_HEADS, ATTN_LOCKSTEP):
            grp = range(g0, g0 + ATTN_LOCKSTEP)
            ss = {h: _dot_nt(k_ref[pl.ds(start, nk), h * LANES:(h + 1) * LANES], qs[h]) for h in grp}
            ps, stats = {}, {}
            for h in grp:
                m_i, _ = carry[h]
                s = ss[h]
                if allowed is not None:
                    s = jnp.where(allowed, s, -jnp.inf)
                m_new = jnp.maximum(m_i, jnp.max(s, axis=0, keepdims=True))
                ps[h] = jnp.exp2(s - m_new).astype(BF16)
                stats[h] = (m_new, jnp.exp2(m_i - m_new))
            for h in grp:
                m_new, alpha = stats[h]
                pv = _dot(vt_ref[h, :, pl.ds(start, nk)], ps[h])
                out[h] = (m_new, alpha * carry[h][1] + pv)
        return tuple(out)

    init = tuple((jnp.full((1, 2 * tq), -jnp.inf, F32), jnp.zeros((VT_ROWS, 2 * tq), F32))
                 for _ in range(DA_HEADS))
    carry = lax.fori_loop(
        0, i // 2, lambda j, c: step(pl.multiple_of(j * (2 * tq), 2 * tq), 2 * tq, c, None), init)
    carry = lax.cond(
        i % 2 == 1,
        lambda c: step(pl.multiple_of((i - 1) * tq, tq), 2 * tq, c, mask_for(2 * tq)),
        lambda c: step(pl.multiple_of(i * tq, tq), tq, c, mask_for(tq)),
        carry)

    for h in range(DA_HEADS):
        acc = carry[h][1]
        o_t = acc[:DA_V_DIM] / acc[DA_V_DIM:DA_V_DIM + 1]
        o_t = o_t[:, :tq] - lam * o_t[:, tq:]
        ms = jnp.mean(o_t * o_t, axis=0, keepdims=True)
        o = (o_t * lax.rsqrt(ms + NORM_EPS)).T * sw_ref[...] * (1.0 - lam_init)
        o_ref[:, h * LANES:(h + 1) * LANES] = o.astype(o_ref.dtype)


def _attn_call(u3, lam_vecs, subln_w, layer):
    bsz, seq, _ = u3.shape
    tq = min(256, seq)
    lam_init = 0.8 - 0.6 * math.exp(-0.3 * layer)
    return pl.pallas_call(
        functools.partial(_attn_body, tq=tq, lam_init=lam_init),
        grid=(bsz, seq // tq),
        in_specs=[
            pl.BlockSpec((4, DA_QK_DIM), lambda b, i: (0, 0)),
            pl.BlockSpec((None, tq, HEADW), lambda b, i: (b, i, COL_Q // HEADW)),
            pl.BlockSpec((None, seq, HEADW), lambda b, i: (b, 0, COL_K // HEADW)),
            pl.BlockSpec((None, seq, HEADW), lambda b, i: (b, 0, COL_V // HEADW)),
            pl.BlockSpec((1, LANES), lambda b, i: (0, 0)),
        ],
        out_specs=pl.BlockSpec((None, tq, HEADW), lambda b, i: (b, i, 0)),
        out_shape=jax.ShapeDtypeStruct((bsz, seq, HEADW), BF16),
        scratch_shapes=[pltpu.VMEM((DA_HEADS, VT_ROWS, seq), BF16)],
        compiler_params=_cp(("parallel", "arbitrary")),
        name="diff_attn",
    )(lam_vecs, u3, u3, u3, subln_w.reshape(1, LANES))


def _hgrn_body(lbp_ref, z_ref, i_ref, q_ref, g_ref, nw_ref, o_ref, *, layer, nchunks):
    lbp = lbp_ref[...]
    e = jnp.exp(lbp - jnp.max(lbp, axis=0, keepdims=True))
    sm = e / jnp.sum(e, axis=0, keepdims=True)
    cs = sm[0:1]
    for t in range(1, layer + 1):
        cs = cs + sm[t:t + 1]
    lb = cs - sm[0:1]
    log_lb = jnp.log(lb)
    log1m = jnp.log1p(-lb)

    rr = _iota((CHUNK, CHUNK), 0)
    cc = _iota((CHUNK, CHUNK), 1)
    sub_mid = jnp.bitwise_and(rr, ~(SUB - 1)) + (SUB // 2 - 1)
    cs_mat = jnp.concatenate([(cc <= rr).astype(BF16), (cc <= sub_mid).astype(BF16)], axis=0)
    diag_ok = jnp.logical_and(cc <= rr, jnp.right_shift(cc, 4) == jnp.right_shift(rr, 4))
    nsub = CHUNK // SUB
    heads = range(HG_HEADS)

    def hcols(x, h):
        return x[:, h * HG_DIM:(h + 1) * HG_DIM]

    def chunk(c, states):
        sl = pl.ds(pl.multiple_of(c * CHUNK, CHUNK), CHUNK)
        z = z_ref[sl, :].astype(F32)
        e_z = jnp.exp(-jnp.abs(z))
        t_z = 1.0 + e_z
        y = log1m - (jnp.maximum(-z, 0.0) + jnp.log(t_z))
        mx = jnp.maximum(log_lb, y)
        lf = mx + jnp.log(1.0 + jnp.exp(-jnp.abs(log_lb - y)))
        key = (1.0 - lb) * (jnp.where(z > 0.0, e_z, 1.0) / t_z)
        qc = q_ref[sl, :].astype(F32)
        vc = i_ref[sl, :]
        vt = vc.astype(F32).T.astype(BF16)

        tot = _dot_exact_lhs(cs_mat, lf * LOG2E)
        b = tot[:CHUNK]
        d = b - tot[CHUNK:]
        b_last = b[CHUNK - 1:CHUNK, :]

        q_d = (qc * jnp.exp2(d)).astype(BF16)
        k_d = (key * jnp.exp2(-d)).astype(BF16)
        q_js, k_js = [], []
        for jb in range(nsub - 1):
            lo, hi = jb * SUB, (jb + 1) * SUB
            e_j = b[hi - 1:hi, :]
            q_j = (qc[hi:] * jnp.exp2(b[hi:] - e_j)).astype(BF16)
            k_j = (key[lo:hi] * jnp.exp2(e_j - b[lo:hi])).astype(BF16)
            q_js.append(jnp.concatenate([jnp.zeros((hi, HEADW), BF16), q_j], axis=0))
            pieces = [k_j, jnp.zeros((CHUNK - hi, HEADW), BF16)]
            if lo:
                pieces.insert(0, jnp.zeros((lo, HEADW), BF16))
            k_js.append(jnp.concatenate(pieces, axis=0))
        q_in = (qc * jnp.exp2(b)).astype(BF16)
        k_out = (key * jnp.exp2(b_last - b)).astype(BF16)
        decay = jnp.exp2(b_last)

        s_diag = [_dot_nt(hcols(q_d, h), hcols(k_d, h)) for h in heads]
        s_off = [_dot_nt(jnp.concatenate([hcols(x, h) for x in q_js], axis=1),
                         jnp.concatenate([hcols(x, h) for x in k_js], axis=1)) for h in heads]
        scores = [(jnp.where(diag_ok, s_diag[h], 0.0) + s_off[h]).astype(BF16) for h in heads]
        o_intra = [_dot(scores[h], hcols(vc, h)) for h in heads]
        o_inter = [_dot_nt(hcols(q_in, h), states[h].astype(BF16)) for h in heads]
        upd = [_dot(vt[h * HG_DIM:(h + 1) * HG_DIM, :], hcols(k_out, h)) for h in heads]
        new_states = tuple(states[h] * hcols(decay, h) + upd[h] for h in heads)

        outs = []
        for h in heads:
            o = o_intra[h] + o_inter[h]
            ms = jnp.mean(o * o, axis=-1, keepdims=True)
            outs.append(o * lax.rsqrt(ms + NORM_EPS))
        g = g_ref[sl, :].astype(F32)
        o = jnp.concatenate(outs, axis=1) * nw_ref[...] * (g * _sigmoid(g))
        o_ref[sl, :] = o.astype(o_ref.dtype)
        return new_states

    init = tuple(jnp.zeros((HG_DIM, HG_DIM), F32) for _ in heads)
    lax.fori_loop(0, nchunks, chunk, init, unroll=16)


def _hgrn_call(u3, hg_lb, norm_w, layer):
    bsz, seq, _ = u3.shape
    nl = hg_lb.shape[0]

    def col(base):
        return lambda b: (b, 0, base // HEADW)

    blk = (None, seq, HEADW)
    return pl.pallas_call(
        functools.partial(_hgrn_body, layer=layer, nchunks=seq // CHUNK),
        grid=(bsz,),
        in_specs=[
            pl.BlockSpec((nl, HEADW), lambda b: (0, 0)),
            pl.BlockSpec(blk, col(COL_HF)),
            pl.BlockSpec(blk, col(COL_HI)),
            pl.BlockSpec(blk, col(COL_HQ)),
            pl.BlockSpec(blk, col(COL_HG)),
            pl.BlockSpec((1, HEADW), lambda b: (0, 0)),
        ],
        out_specs=pl.BlockSpec(blk, lambda b: (b, 0, 0)),
        out_shape=jax.ShapeDtypeStruct((bsz, seq, HEADW), BF16),
        compiler_params=_cp(("parallel",)),
        name="hgrn2",
    )(hg_lb, u3, u3, u3, u3, jnp.tile(norm_w, HG_HEADS).reshape(1, HEADW))


def _head_ones(n, width):
    r = _iota((n, n), 0) // width
    c = _iota((n, n), 1) // width
    return (r == c).astype(BF16)


def _rw_prep(cur, prev, prm, vres, has_prev):
    mu_ref, w0_ref, w2_ref, a0_ref, a2_ref, g2_ref, kk_ref, ka_ref = prm
    ts = cur[0].shape[0]
    row = _iota((ts, 1), 0)

    def shifted(idx, lo, hi):
        u = cur[idx][...].astype(F32)
        if prev is None:
            prev_row = jnp.zeros((1, u.shape[1]), F32)
        else:
            prev_row = prev[idx][SUB - 1:SUB, :].astype(F32) * has_prev.astype(F32)
        u_prev = jnp.where(row == 0, prev_row, pltpu.roll(u, 1, 0))
        return u + (u_prev - u) * mu_ref[:, lo:hi]

    r = shifted(0, 0, 512)
    k = shifted(1, 512, 1024)
    v = shifted(2, 1024, 1536)
    lora = shifted(3, 1536, 1792)
    w_lo = lora[:, 0:64]
    a_lo = lora[:, 64:128]
    g_lo = lora[:, 128:256]

    def lora_dot(act, w_ref):
        return _dot(act.astype(BF16), w_ref[...].astype(BF16))

    wpre = w0_ref[...] + lora_dot(jnp.tanh(w_lo), w2_ref)
    w = -_softplus(-wpre) - 0.5
    lw = -jnp.exp(w)
    a = _sigmoid(a0_ref[...] + lora_dot(a_lo, a2_ref))
    g = lora_dot(_sigmoid(g_lo), g2_ref)
    if vres is not None:
        vf_ref, v0_ref, v1_ref, v2_ref = vres
        mix = _sigmoid(v0_ref[...] + lora_dot(lora_dot(v, v1_ref), v2_ref))
        v = v + (vf_ref[...].astype(F32) - v) * mix

    kk = k * kk_ref[...]
    ssq = _head_sum(kk * kk)
    kkn = kk * lax.rsqrt(jnp.maximum(ssq, 1e-24))
    k = k * (1.0 + (a - 1.0) * ka_ref[...])
    return r, k, v, lw, kkn, kkn * a, g


def _rw_scan(r_ref, k_ref, v_ref, lw_ref, kk_ref, bk_ref, g_ref, rk_ref, gw_ref, gb_ref,
             o_ref, state_ref, oacc_ref, nsub):
    t = CHUNK
    gw = RW_GW
    tc = nsub * t
    units = [(s, grp) for s in range(nsub) for grp in range(RW_GROUPS)]

    rr = _iota((tc, tc), 0)
    cc = _iota((tc, tc), 1)
    tril_bd = jnp.logical_and(cc <= rr, (cc // t) == (rr // t)).astype(BF16)
    rr4 = _iota((t, gw), 0)
    cj = jnp.bitwise_and(_iota((t, gw), 1), t - 1)
    strict = cj < rr4
    incl = cj <= rr4
    eye_c = (cj == rr4).astype(F32)
    bd_mask = (_iota((gw, gw), 0) // t) == (_iota((gw, gw), 1) // RW_DIM)
    zero_b = jnp.zeros((), BF16)

    left_head = _iota((t, LANES), 1) < RW_DIM
    zeros_tile = jnp.zeros((t, LANES), BF16)

    def bd(xc):
        xb = xc.astype(BF16)
        rows = []
        for h in range(RW_PACK):
            tile = xb[:, (h // 2) * LANES:(h // 2 + 1) * LANES]
            kept = jnp.where(left_head, tile, zero_b) if h % 2 == 0 else jnp.where(left_head, zero_b, tile)
            rows.append(jnp.concatenate([kept, zeros_tile] if h < 2 else [zeros_tile, kept], axis=1))
        return jnp.concatenate(rows, axis=0)

    def cut(x, u):
        s, grp = u
        return x[s * t:(s + 1) * t, grp * gw:(grp + 1) * gw]

    r = r_ref[...].astype(F32)
    k = k_ref[...].astype(F32)
    v = v_ref[...].astype(F32)
    lw = lw_ref[...] * LOG2E
    kk = kk_ref[...].astype(F32)
    bk = bk_ref[...].astype(F32)
    c = _dot_exact_lhs(tril_bd, lw)
    e_neg = jnp.exp2(-c)
    a_t = -kk * jnp.exp2(c - lw)
    b_t = bk * e_neg
    k_t = k * e_neg
    r_t = r * jnp.exp2(c)
    c_last = jnp.concatenate(
        [jnp.broadcast_to(c[(s + 1) * t - 1:(s + 1) * t, :], (t, RW_WIDTH)) for s in range(nsub)], axis=0)
    decay_out = jnp.exp2(c_last - c)
    b_g = bk * decay_out
    k_g = k * decay_out
    gamma = jnp.exp2(c_last)

    ar = {u: jnp.concatenate([cut(a_t, u), cut(r_t, u)], axis=0).astype(BF16) for u in units}
    x_bk = {u: _dot_nt(ar[u], jnp.concatenate([bd(cut(b_t, u)), bd(cut(k_t, u))], axis=0)) for u in units}
    l_c = {u: jnp.where(strict, x_bk[u][:t, :gw], 0.0) for u in units}
    m_c = {u: jnp.where(strict, x_bk[u][:t, gw:], 0.0).astype(BF16) for u in units}
    rb = {u: jnp.where(incl, x_bk[u][t:, :gw], 0.0).astype(BF16) for u in units}
    rkm = {u: jnp.where(incl, x_bk[u][t:, gw:], 0.0).astype(BF16) for u in units}
    v_bd = {u: bd(cut(v, u)) for u in units}

    p = {u: _dot(l_c[u].astype(BF16), bd(l_c[u])) for u in units}
    s_inv = {u: eye_c + l_c[u] for u in units}
    mo = {u: _dot(jnp.concatenate([m_c[u], rkm[u]], axis=0), v_bd[u]) for u in units}
    mv = {u: mo[u][:t] for u in units}
    o_kv = {u: mo[u][t:] for u in units}
    for lvl in range(1, 6):
        last = lvl == 5
        nxt_p, nxt_s = {}, {}
        for u in units:
            if last:
                nxt_s[u] = s_inv[u] + _dot(s_inv[u].astype(BF16), bd(p[u]))
            else:
                res = _dot(jnp.concatenate([s_inv[u], p[u]], axis=0).astype(BF16), bd(p[u]))
                nxt_s[u] = s_inv[u] + res[:t]
                nxt_p[u] = res[t:]
        p, s_inv = nxt_p, nxt_s

    rbs = {u: _dot(rb[u], bd(s_inv[u])) for u in units}
    wu = {u: _dot(jnp.concatenate([s_inv[u], rbs[u]], axis=0).astype(BF16),
                  jnp.concatenate([bd(cut(a_t, u)), bd(mv[u])], axis=1)) for u in units}
    w_t = {u: wu[u][:t, :gw] for u in units}
    u_t = {u: wu[u][:t, gw:] for u in units}
    r_hat = {u: (cut(r_t, u) + wu[u][t:, :gw]).astype(BF16) for u in units}
    o_hat = {u: wu[u][t:, gw:] + o_kv[u] for u in units}
    g_mat = {u: jnp.where(bd_mask, _dot(w_t[u].T.astype(BF16), cut(b_g, u).astype(BF16)), 0.0).astype(BF16)
             for u in units}
    c_mat = {}
    for u in units:
        uv_t = jnp.concatenate([u_t[u], cut(v, u)], axis=0).T.astype(BF16)
        bkg = jnp.concatenate([cut(b_g, u), cut(k_g, u)], axis=0).astype(BF16)
        full = jnp.where(bd_mask, _dot(uv_t, bkg), 0.0)
        c_mat[u] = full[0:t] + full[t:2 * t] + full[2 * t:3 * t] + full[3 * t:4 * t]

    for s in range(nsub):
        for grp in range(RW_GROUPS):
            u = (s, grp)
            st = state_ref[grp]
            st_b = st.astype(BF16)
            oacc_ref[s * t:(s + 1) * t, grp * gw:(grp + 1) * gw] = _dot_nt(r_hat[u], bd(st_b)) + o_hat[u]
            g_row = gamma[(s + 1) * t - 1:(s + 1) * t, grp * gw:(grp + 1) * gw]
            state_ref[grp] = st * g_row + _dot(st_b, g_mat[u]) + c_mat[u]

    o = oacc_ref[...]
    mean = _head_sum(o) * (1.0 / RW_DIM)
    dlt = o - mean
    var = _head_sum(dlt * dlt) * (1.0 / RW_DIM)
    on = dlt * lax.rsqrt(var + RW_GN_EPS) * gw_ref[...] + gb_ref[...]
    bonus = _head_sum(r * k * rk_ref[...])
    o_ref[...] = ((on + bonus * v) * g_ref[...]).astype(o_ref.dtype)


RW_SLOT_DTYPES = (BF16, BF16, BF16, F32, BF16, BF16, BF16)


def _rw_mix_body(*refs, has_vres, emit_v, nsub):
    nxt, nxt_prev, first, prm = refs[0:4], refs[4:8], refs[8:12], refs[12:20]
    pos = 20
    vres_nxt = vres_first = None
    if has_vres:
        vf_nxt, vf_first, v0, v1, v2 = refs[pos:pos + 5]
        vres_nxt, vres_first = (vf_nxt, v0, v1, v2), (vf_first, v0, v1, v2)
        pos += 5
    rk_ref, gw_ref, gb_ref = refs[pos:pos + 3]
    pos += 3
    o_ref = refs[pos]
    pos += 1
    v_out = None
    if emit_v:
        v_out = refs[pos]
        pos += 1
    state_ref, oacc_ref = refs[pos:pos + 2]
    slots = refs[pos + 2:pos + 9]

    j = pl.program_id(1)
    last = pl.num_programs(1) - 1

    def put(slot, vals):
        for ref, val in zip(slots, vals):
            ref[slot] = val.astype(ref.dtype)

    @pl.when(j == 0)
    def _():
        state_ref[...] = jnp.zeros_like(state_ref)
        put(0, _rw_prep(first, None, prm, vres_first, None))

    put((j + 1) % 2, _rw_prep(nxt, nxt_prev, prm, vres_nxt, jnp.minimum(j + 1, last) > 0))

    cur = [ref.at[j % 2] for ref in slots]
    _rw_scan(*cur, rk_ref, gw_ref, gb_ref, o_ref, state_ref, oacc_ref, nsub)
    if emit_v:
        v_out[...] = cur[2][...]


def _rw_mix_call(u3, mu, w0, w2, a0, a2, g2, k_k, k_a, vres, r_k, gn_w, gn_b, emit_v):
    bsz, seq, _ = u3.shape
    tc = min(256, seq)
    n = seq // tc
    w = RW_WIDTH
    has_vres = vres is not None

    def full(arr):
        return pl.BlockSpec(arr.shape, lambda b, j: (0,) * arr.ndim)

    def nxt(col, width):
        return pl.BlockSpec((None, tc, width), lambda b, j: (b, jnp.minimum(j + 1, n - 1), col // width))

    def nxt_prev(col, width):
        return pl.BlockSpec(
            (None, SUB, width),
            lambda b, j: (b, jnp.maximum(jnp.minimum(j + 1, n - 1) * (tc // SUB) - 1, 0), col // width))

    def first(col, width):
        return pl.BlockSpec((None, tc, width), lambda b, j: (b, 0, col // width))

    cols = [(COL_RR, w), (COL_RK, w), (COL_RV, w), (COL_RL, RW_LORA_W)]
    params = [mu.reshape(1, -1), w0.reshape(1, w), w2, a0.reshape(1, w), a2, g2,
              k_k.reshape(1, w), k_a.reshape(1, w)]
    in_specs = [f(c, wd) for f in (nxt, nxt_prev, first) for c, wd in cols] + [full(p) for p in params]
    args = [u3] * 12 + params
    if has_vres:
        v_first, v0, v1, v2 = vres
        extra = [v0.reshape(1, w), v1, v2]
        in_specs += [pl.BlockSpec((None, tc, w), lambda b, j: (b, jnp.minimum(j + 1, n - 1), 0)),
                     pl.BlockSpec((None, tc, w), lambda b, j: (b, 0, 0))] + [full(p) for p in extra]
        args += [v_first, v_first] + extra
    tail = [r_k.reshape(1, w), gn_w.reshape(1, w), gn_b.reshape(1, w)]
    in_specs += [full(p) for p in tail]
    args += tail

    out_spec = pl.BlockSpec((None, tc, w), lambda b, j: (b, j, 0))
    out_shape = jax.ShapeDtypeStruct((bsz, seq, w), BF16)
    scratch = [pltpu.VMEM((RW_GROUPS, RW_DIM, RW_GW), F32), pltpu.VMEM((tc, w), F32)]
    scratch += [pltpu.VMEM((2, tc, w), dt) for dt in RW_SLOT_DTYPES]
    return pl.pallas_call(
        functools.partial(_rw_mix_body, has_vres=has_vres, emit_v=emit_v, nsub=tc // CHUNK),
        grid=(bsz, n),
        in_specs=in_specs,
        out_specs=[out_spec, out_spec] if emit_v else out_spec,
        out_shape=[out_shape, out_shape] if emit_v else out_shape,
        scratch_shapes=scratch,
        compiler_params=_cp(("parallel", "arbitrary")),
        name="rwkv7",
    )(*args)


def _merge_body(ga_ref, gb_ref, gc_ref, oa_ref, ob_ref, oc_ref, x_ref, gate_ref,
                wa_ref, wb_ref, wc_ref, wo_ref, o_ref):
    merged = (_sigmoid(ga_ref[...].astype(F32)) * _dot(oa_ref[...], wa_ref[...])
              + _sigmoid(gb_ref[...].astype(F32)) * _dot(ob_ref[...], wb_ref[...])
              + _sigmoid(gc_ref[...].astype(F32)) * _dot(oc_ref[...], wc_ref[...]))
    mix = _dot(merged.astype(BF16), wo_ref[...])
    o_ref[...] = x_ref[...] + gate_ref[...] * mix


def _merge_call(u2, o_a, o_b, o_c, x2, ada3, wa, wb, wc, wo, layer, bsz, seq):
    m, d = x2.shape
    tm = min(512, seq)
    per_b = seq // tm
    base = layer * bsz * 6
    wdt = o_a.shape[1]

    def const(arr):
        return pl.BlockSpec(arr.shape, lambda i: (0, 0))

    return pl.pallas_call(
        _merge_body,
        grid=(m // tm,),
        in_specs=[
            pl.BlockSpec((tm, d), lambda i: (i, 0)),
            pl.BlockSpec((tm, d), lambda i: (i, 1)),
            pl.BlockSpec((tm, d), lambda i: (i, 2)),
            pl.BlockSpec((tm, wdt), lambda i: (i, 0)),
            pl.BlockSpec((tm, wdt), lambda i: (i, 0)),
            pl.BlockSpec((tm, wdt), lambda i: (i, 0)),
            pl.BlockSpec((tm, d), lambda i: (i, 0)),
            pl.BlockSpec((None, 1, d), lambda i: (base + (i // per_b) * 6 + 2, 0, 0)),
            const(wa), const(wb), const(wc), const(wo),
        ],
        out_specs=pl.BlockSpec((tm, d), lambda i: (i, 0)),
        out_shape=jax.ShapeDtypeStruct((m, d), F32),
        compiler_params=_cp(("parallel",)),
        name="merge_out",
    )(u2, u2, u2, o_a, o_b, o_c, x2, ada3, wa, wb, wc, wo)


def _ffn_body(x_ref, nw_ref, shift_ref, scale_ref, gate_ref, wi_ref, wo_ref, fn_ref, o_ref, *, final):
    x = x_ref[...]
    h = _norm_mod(x, nw_ref[...], scale_ref[...], shift_ref[...]).astype(BF16)
    acc = jnp.zeros(x.shape, F32)
    lo = 0
    for width in FFN_CHUNKS:
        gh = _dot(h, wi_ref[:, lo:lo + width])
        uh = _dot(h, wi_ref[:, FFN_HIDDEN + lo:FFN_HIDDEN + lo + width])
        act = (gh * _sigmoid(gh) * uh).astype(BF16)
        acc = acc + _dot(act, wo_ref[lo:lo + width, :])
        lo += width
    y = x + gate_ref[...] * acc
    if final:
        ms = jnp.mean(y * y, axis=-1, keepdims=True)
        y = y * lax.rsqrt(ms + NORM_EPS) * fn_ref[...]
    o_ref[...] = y


def _ffn_call(x2, nw, ada3, wi, wo, final_w, layer, bsz, seq, final):
    m, d = x2.shape
    tm = min(1024, seq)
    per_b = seq // tm
    base = layer * bsz * 6

    def ada_spec(k):
        return pl.BlockSpec((None, 1, d), lambda i: (base + (i // per_b) * 6 + k, 0, 0))

    def const(arr):
        return pl.BlockSpec(arr.shape, lambda i: (0, 0), pipeline_mode=pl.Buffered(1))

    return pl.pallas_call(
        functools.partial(_ffn_body, final=final),
        grid=(m // tm,),
        in_specs=[
            pl.BlockSpec((tm, d), lambda i: (i, 0)),
            pl.BlockSpec((1, d), lambda i: (0, 0)),
            ada_spec(3), ada_spec(4), ada_spec(5),
            const(wi), const(wo),
            pl.BlockSpec((1, d), lambda i: (0, 0)),
        ],
        out_specs=pl.BlockSpec((tm, d), lambda i: (i, 0)),
        out_shape=jax.ShapeDtypeStruct((m, d), F32),
        compiler_params=_cp(("parallel",)),
        name="ffn_final" if final else "ffn",
    )(x2, nw, ada3, ada3, ada3, wi, wo, final_w)


def _permute_w_in(w):
    def qk_lanes(t):
        t = t.reshape(t.shape[0], DA_HEADS, 2, 2, DA_QK_DIM // 2)
        return jnp.swapaxes(t, 2, 3).reshape(t.shape[0], DA_HEADS * 2 * DA_QK_DIM)

    q, k, v = qk_lanes(w[:, 0:512]), qk_lanes(w[:, 512:1024]), w[:, 1024:1536]
    hgrn = w[:, 1536:3584]
    rw = w[:, 3584:5376]
    gates = w[:, 5376:8448]
    return jnp.concatenate([gates, hgrn, q, k, v, rw], axis=1)


def kernel(x, c, positions, ada_w, ada_b, norm_mix_w, norm_ffn_w, w_in, da_lambda, da_subln_w, hg_lb, hg_norm_w, rw_mu, rw_w0, rw_w2, rw_a0, rw_a2, rw_g2, rw_k_k, rw_k_a, rw_r_k, rw_gn_w, rw_gn_b, rw_v0, rw_v1, rw_v2, w_branch_a, w_branch_b, w_branch_c, w_out, ffn_w_in, ffn_w_out, final_norm_w):
    bsz, seq, d = x.shape
    depth = ada_w.shape[0]
    m = bsz * seq

    ada = _ada_call(c, ada_w, ada_b)
    ada3 = ada.reshape(depth * bsz * 6, 1, d)
    cos, sin_s = _rope_call(positions)

    x2 = x.reshape(m, d)
    v_first = None
    for l in range(depth):
        w_l = _permute_w_in(w_in[l]).astype(BF16)
        u2 = _proj_in_call(x2, norm_mix_w[l].reshape(1, d), ada3, cos, sin_s, w_l, l, bsz, seq)
        u3 = u2.reshape(bsz, seq, IN_COLS)

        o_a = _attn_call(u3, da_lambda[l], da_subln_w[l], l)
        o_b = _hgrn_call(u3, hg_lb, hg_norm_w[l], l)
        vres = None if l == 0 else (v_first, rw_v0[l - 1], rw_v1[l - 1], rw_v2[l - 1])
        emit_v = l == 0 and depth > 1
        res = _rw_mix_call(u3, rw_mu[l], rw_w0[l], rw_w2[l], rw_a0[l], rw_a2[l], rw_g2[l], rw_k_k[l],
                           rw_k_a[l], vres, rw_r_k[l], rw_gn_w[l], rw_gn_b[l], emit_v)
        if emit_v:
            o_c, v_first = res
        else:
            o_c = res

        x2 = _merge_call(u2, o_a.reshape(m, -1), o_b.reshape(m, -1), o_c.reshape(m, -1), x2, ada3,
                         w_branch_a[l].astype(BF16), w_branch_b[l].astype(BF16),
                         w_branch_c[l].astype(BF16), w_out[l].astype(BF16), l, bsz, seq)
        x2 = _ffn_call(x2, norm_ffn_w[l].reshape(1, d), ada3, ffn_w_in[l].astype(BF16),
                       ffn_w_out[l].astype(BF16), final_norm_w.reshape(1, d), l, bsz, seq,
                       final=(l == depth - 1))
    return x2.reshape(bsz, seq, d)
```

```python
import functools
import math

import jax
import jax.numpy as jnp
from jax import lax
from jax.experimental import pallas as pl
from jax.experimental.pallas import tpu as pltpu

F32 = jnp.float32
BF16 = jnp.bfloat16

D_MODEL = 1024
CHUNK = 64
ROPE_THETA = 10000.0
NORM_EPS = 1e-6

DA_HEADS = 4
DA_QK_DIM = 64
DA_V_DIM = 128
HG_HEADS = 4
HG_DIM = 128
RW_HEADS = 8
RW_DIM = 64
RW_WIDTH = 512
RW_GN_EPS = 64e-5
FFN_HIDDEN = 2816
FFN_CHUNKS = (768, 768, 768, 512)
IN_COLS = 8448

LANES = 128
ATTN_LOCKSTEP = 4
LOG2E = 1.4426950408889634
Q_SCALE = DA_QK_DIM ** -0.5 * LOG2E
VT_ROWS = DA_V_DIM + 16
SUB = 16
RW_PACK = 4
RW_GW = RW_PACK * RW_DIM
RW_GROUPS = RW_HEADS // RW_PACK

COL_GATES = 0
COL_HF = 3072
COL_HI = 3584
COL_HQ = 4096
COL_HG = 4608
COL_Q = 5120
COL_K = 5632
COL_V = 6144
COL_RR = 6656
COL_RK = 7168
COL_RV = 7680
COL_RL = 8192
RW_LORA_W = 256
HEADW = 512
PROJ_TN = 4224
GROUPS_PER_TILE = PROJ_TN // LANES
ROPE_Q0 = COL_Q // LANES
ROPE_K0 = COL_K // LANES

VMEM_LIMIT = 56 * 1024 * 1024


def _cp(sem):
    return pltpu.CompilerParams(dimension_semantics=sem, vmem_limit_bytes=VMEM_LIMIT)


def _sigmoid(x):
    return 1.0 / (1.0 + jnp.exp(-x))


def _softplus(x):
    return jnp.maximum(x, 0.0) + jnp.log(1.0 + jnp.exp(-jnp.abs(x)))


def _dot(a, b):
    return jnp.dot(a, b, preferred_element_type=F32)


def _dot_nt(a, b):
    return lax.dot_general(a, b, (((1,), (1,)), ((), ())), preferred_element_type=F32)


def _dot_exact_lhs(a01, x):
    n = x.shape[1]
    hi = x.astype(BF16)
    lo = (x - hi.astype(F32)).astype(BF16)
    res = _dot(a01, jnp.concatenate([hi, lo], axis=1))
    return res[:, 0:n] + res[:, n:2 * n]


def _head_sum(x):
    ones_g = _head_ones(RW_GW, RW_DIM)
    xb = x.astype(BF16)
    return jnp.concatenate(
        [_dot(xb[:, grp * RW_GW:(grp + 1) * RW_GW], ones_g) for grp in range(x.shape[1] // RW_GW)], axis=1)


def _dot_hp(a, b):
    a_hi = a.astype(BF16)
    a_lo = (a - a_hi.astype(F32)).astype(BF16)
    b_hi = b.astype(BF16)
    b_lo = (b - b_hi.astype(F32)).astype(BF16)
    return _dot(a_hi, b_hi) + _dot(a_hi, b_lo) + _dot(a_lo, b_hi)


def _iota(shape, dim):
    return lax.broadcasted_iota(jnp.int32, shape, dim)


def _ada_body(c_ref, w_ref, b_ref, o_ref):
    c = c_ref[...]
    ca = c * _sigmoid(c)
    o_ref[...] = _dot_hp(ca, w_ref[...]) + b_ref[...]


def _ada_call(c, ada_w, ada_b):
    nl, d, n6 = ada_w.shape
    bsz = c.shape[0]
    tn = 1536
    return pl.pallas_call(
        _ada_body,
        grid=(nl, n6 // tn),
        in_specs=[
            pl.BlockSpec((bsz, d), lambda l, j: (0, 0)),
            pl.BlockSpec((None, d, tn), lambda l, j: (l, 0, j)),
            pl.BlockSpec((None, 1, tn), lambda l, j: (l, 0, j)),
        ],
        out_specs=pl.BlockSpec((None, bsz, tn), lambda l, j: (l, 0, j)),
        out_shape=jax.ShapeDtypeStruct((nl, bsz, n6), F32),
        compiler_params=_cp(("parallel", "parallel")),
        name="ada",
    )(c, ada_w, ada_b.reshape(nl, 1, n6))


def _rope_body(pos_ref, invf_ref, sgn_ref, cos_ref, sin_ref):
    ang = pos_ref[...].astype(F32) * invf_ref[...]
    cos_ref[...] = jnp.cos(ang)
    sin_ref[...] = jnp.sin(ang) * sgn_ref[...]


def _rope_call(positions):
    m = positions.size
    tm = min(2048, m)
    inv_freq = ROPE_THETA ** (-jnp.arange(0, DA_QK_DIM, 2, dtype=F32) / DA_QK_DIM)
    invf = jnp.tile(inv_freq, LANES // (DA_QK_DIM // 2)).reshape(1, LANES)
    sgn = jnp.where(jnp.arange(LANES) < LANES // 2, -1.0, 1.0).astype(F32).reshape(1, LANES)
    return pl.pallas_call(
        _rope_body,
        grid=(m // tm,),
        in_specs=[
            pl.BlockSpec((tm, 1), lambda i: (i, 0)),
            pl.BlockSpec((1, LANES), lambda i: (0, 0)),
            pl.BlockSpec((1, LANES), lambda i: (0, 0)),
        ],
        out_specs=[pl.BlockSpec((tm, LANES), lambda i: (i, 0))] * 2,
        out_shape=[jax.ShapeDtypeStruct((m, LANES), F32)] * 2,
        compiler_params=_cp(("parallel",)),
        name="rope_tables",
    )(positions.reshape(m, 1), invf, sgn)


def _norm_mod(x, nw, scale, shift):
    ms = jnp.mean(x * x, axis=-1, keepdims=True)
    y = x * lax.rsqrt(ms + NORM_EPS) * nw
    return y * (1.0 + scale) + shift


def _rope_kind(group):
    if ROPE_Q0 <= group < ROPE_Q0 + DA_HEADS:
        return "q"
    if ROPE_K0 <= group < ROPE_K0 + DA_HEADS:
        return "k"
    return None


_ROPE_TILES = sorted({g // GROUPS_PER_TILE for g in range(IN_COLS // LANES) if _rope_kind(g)})


def _proj_in_body(x_ref, nw_ref, shift_ref, scale_ref, cos_ref, sin_ref, w_ref, o_ref, h_ref):
    j = pl.program_id(1)

    @pl.when(j == 0)
    def _():
        h_ref[...] = _norm_mod(x_ref[...], nw_ref[...], scale_ref[...], shift_ref[...]).astype(BF16)

    for jt in _ROPE_TILES:
        @pl.when(j == jt)
        def _(jt=jt):
            acc = _dot(h_ref[...], w_ref[...])
            cos = cos_ref[...]
            sin_s = sin_ref[...]
            for g in range(GROUPS_PER_TILE):
                slab = acc[:, g * LANES:(g + 1) * LANES]
                kind = _rope_kind(jt * GROUPS_PER_TILE + g)
                if kind is not None:
                    slab = slab * cos + pltpu.roll(slab, LANES // 2, 1) * sin_s
                    if kind == "q":
                        slab = slab * Q_SCALE
                o_ref[:, g * LANES:(g + 1) * LANES] = slab.astype(o_ref.dtype)

    plain = j != _ROPE_TILES[0]
    for jt in _ROPE_TILES[1:]:
        plain = jnp.logical_and(plain, j != jt)

    @pl.when(plain)
    def _():
        o_ref[...] = _dot(h_ref[...], w_ref[...]).astype(o_ref.dtype)


def _proj_in_call(x2, nw, ada3, cos, sin_s, w_bf16, layer, bsz, seq):
    m, d = x2.shape
    n = w_bf16.shape[1]
    tm = min(1024, seq)
    per_b = seq // tm
    base = layer * bsz * 6

    def ada_spec(k):
        return pl.BlockSpec((None, 1, d), lambda i, j: (base + (i // per_b) * 6 + k, 0, 0))

    return pl.pallas_call(
        _proj_in_body,
        grid=(m // tm, n // PROJ_TN),
        in_specs=[
            pl.BlockSpec((tm, d), lambda i, j: (i, 0)),
            pl.BlockSpec((1, d), lambda i, j: (0, 0)),
            ada_spec(0),
            ada_spec(1),
            pl.BlockSpec((tm, LANES), lambda i, j: (i, 0)),
            pl.BlockSpec((tm, LANES), lambda i, j: (i, 0)),
            pl.BlockSpec((d, PROJ_TN), lambda i, j: (0, j)),
        ],
        out_specs=pl.BlockSpec((tm, PROJ_TN), lambda i, j: (i, j)),
        out_shape=jax.ShapeDtypeStruct((m, n), BF16),
        scratch_shapes=[pltpu.VMEM((tm, d), BF16)],
        compiler_params=_cp(("parallel", "arbitrary")),
        name="proj_in",
    )(x2, nw, ada3, ada3, cos, sin_s, w_bf16)


def _attn_body(lam_ref, q_ref, k_ref, v_ref, sw_ref, o_ref, vt_ref, *, tq, lam_init):
    i = pl.program_id(1)

    @pl.when(i == 0)
    def _():
        for h in range(DA_HEADS):
            vt_ref[h, 0:DA_V_DIM, :] = v_ref[:, h * LANES:(h + 1) * LANES].astype(F32).T.astype(BF16)
            vt_ref[h, DA_V_DIM:VT_ROWS, :] = jnp.ones((VT_ROWS - DA_V_DIM, vt_ref.shape[2]), BF16)

    lv = lam_ref[...]
    lam = (jnp.exp(jnp.sum(lv[0:1] * lv[1:2], axis=1, keepdims=True))
           - jnp.exp(jnp.sum(lv[2:3] * lv[3:4], axis=1, keepdims=True)) + lam_init)

    map0 = jnp.bitwise_and(_iota((1, LANES), 1), DA_QK_DIM - 1) < DA_QK_DIM // 2
    zero = jnp.zeros((), BF16)
    qs = []
    for h in range(DA_HEADS):
        q = q_ref[:, h * LANES:(h + 1) * LANES]
        q0 = jnp.where(map0, q, zero)
        q1 = jnp.where(map0, zero, q)
        qs.append(jnp.concatenate([q0, q1], axis=0))

    def mask_for(nk):
        k_chunk = jnp.right_shift(_iota((nk, 1), 0), 6)
        q_chunk = jnp.right_shift(jnp.bitwise_and(_iota((1, 2 * tq), 1), tq - 1), 6) + (nk - tq) // CHUNK
        return k_chunk <= q_chunk

    def step(start, nk, carry, allowed):
        out = [None] * DA_HEADS
        for g0 in range(0, DA_HEADS, ATTN_LOCKSTEP):
            grp = range(g0, g0 + ATTN_LOCKSTEP)
            ss = {h: _dot_nt(k_ref[pl.ds(start, nk), h * LANES:(h + 1) * LANES], qs[h]) for h in grp}
            ps, stats = {}, {}
            for h in grp:
                m_i, _ = carry[h]
                s = ss[h]
                if allowed is not None:
                    s = jnp.where(allowed, s, -jnp.inf)
                m_new = jnp.maximum(m_i, jnp.max(s, axis=0, keepdims=True))
                ps[h] = jnp.exp2(s - m_new).astype(BF16)
                stats[h] = (m_new, jnp.exp2(m_i - m_new))
            for h in grp:
                m_new, alpha = stats[h]
                pv = _dot(vt_ref[h, :, pl.ds(start, nk)], ps[h])
                out[h] = (m_new, alpha * carry[h][1] + pv)
        return tuple(out)

    init = tuple((jnp.full((1, 2 * tq), -jnp.inf, F32), jnp.zeros((VT_ROWS, 2 * tq), F32))
                 for _ in range(DA_HEADS))
    carry = lax.fori_loop(
        0, i // 2, lambda j, c: step(pl.multiple_of(j * (2 * tq), 2 * tq), 2 * tq, c, None), init)
    carry = lax.cond(
        i % 2 == 1,
        lambda c: step(pl.multiple_of((i - 1) * tq, tq), 2 * tq, c, mask_for(2 * tq)),
        lambda c: step(pl.multiple_of(i * tq, tq), tq, c, mask_for(tq)),
        carry)

    for h in range(DA_HEADS):
        acc = carry[h][1]
        o_t = acc[:DA_V_DIM] / acc[DA_V_DIM:DA_V_DIM + 1]
        o_t = o_t[:, :tq] - lam * o_t[:, tq:]
        ms = jnp.mean(o_t * o_t, axis=0, keepdims=True)
        o = (o_t * lax.rsqrt(ms + NORM_EPS)).T * sw_ref[...] * (1.0 - lam_init)
        o_ref[:, h * LANES:(h + 1) * LANES] = o.astype(o_ref.dtype)


def _attn_call(u3, lam_vecs, subln_w, layer):
    bsz, seq, _ = u3.shape
    tq = min(256, seq)
    lam_init = 0.8 - 0.6 * math.exp(-0.3 * layer)
    return pl.pallas_call(
        functools.partial(_attn_body, tq=tq, lam_init=lam_init),
        grid=(bsz, seq // tq),
        in_specs=[
            pl.BlockSpec((4, DA_QK_DIM), lambda b, i: (0, 0)),
            pl.BlockSpec((None, tq, HEADW), lambda b, i: (b, i, COL_Q // HEADW)),
            pl.BlockSpec((None, seq, HEADW), lambda b, i: (b, 0, COL_K // HEADW)),
            pl.BlockSpec((None, seq, HEADW), lambda b, i: (b, 0, COL_V // HEADW)),
            pl.BlockSpec((1, LANES), lambda b, i: (0, 0)),
        ],
        out_specs=pl.BlockSpec((None, tq, HEADW), lambda b, i: (b, i, 0)),
        out_shape=jax.ShapeDtypeStruct((bsz, seq, HEADW), BF16),
        scratch_shapes=[pltpu.VMEM((DA_HEADS, VT_ROWS, seq), BF16)],
        compiler_params=_cp(("parallel", "arbitrary")),
        name="diff_attn",
    )(lam_vecs, u3, u3, u3, subln_w.reshape(1, LANES))


def _hgrn_body(lbp_ref, z_ref, i_ref, q_ref, g_ref, nw_ref, o_ref, *, layer, nchunks):
    lbp = lbp_ref[...]
    e = jnp.exp(lbp - jnp.max(lbp, axis=0, keepdims=True))
    sm = e / jnp.sum(e, axis=0, keepdims=True)
    cs = sm[0:1]
    for t in range(1, layer + 1):
        cs = cs + sm[t:t + 1]
    lb = cs - sm[0:1]
    log_lb = jnp.log(lb)
    log1m = jnp.log1p(-lb)

    rr = _iota((CHUNK, CHUNK), 0)
    cc = _iota((CHUNK, CHUNK), 1)
    sub_mid = jnp.bitwise_and(rr, ~(SUB - 1)) + (SUB // 2 - 1)
    cs_mat = jnp.concatenate([(cc <= rr).astype(BF16), (cc <= sub_mid).astype(BF16)], axis=0)
    diag_ok = jnp.logical_and(cc <= rr, jnp.right_shift(cc, 4) == jnp.right_shift(rr, 4))
    nsub = CHUNK // SUB
    heads = range(HG_HEADS)

    def hcols(x, h):
        return x[:, h * HG_DIM:(h + 1) * HG_DIM]

    def chunk(c, states):
        sl = pl.ds(pl.multiple_of(c * CHUNK, CHUNK), CHUNK)
        z = z_ref[sl, :].astype(F32)
        e_z = jnp.exp(-jnp.abs(z))
        t_z = 1.0 + e_z
        y = log1m - (jnp.maximum(-z, 0.0) + jnp.log(t_z))
        mx = jnp.maximum(log_lb, y)
        lf = mx + jnp.log(1.0 + jnp.exp(-jnp.abs(log_lb - y)))
        key = (1.0 - lb) * (jnp.where(z > 0.0, e_z, 1.0) / t_z)
        qc = q_ref[sl, :].astype(F32)
        vc = i_ref[sl, :]
        vt = vc.astype(F32).T.astype(BF16)

        tot = _dot_exact_lhs(cs_mat, lf * LOG2E)
        b = tot[:CHUNK]
        d = b - tot[CHUNK:]
        b_last = b[CHUNK - 1:CHUNK, :]

        q_d = (qc * jnp.exp2(d)).astype(BF16)
        k_d = (key * jnp.exp2(-d)).astype(BF16)
        q_js, k_js = [], []
        for jb in range(nsub - 1):
            lo, hi = jb * SUB, (jb + 1) * SUB
            e_j = b[hi - 1:hi, :]
            q_j = (qc[hi:] * jnp.exp2(b[hi:] - e_j)).astype(BF16)
            k_j = (key[lo:hi] * jnp.exp2(e_j - b[lo:hi])).astype(BF16)
            q_js.append(jnp.concatenate([jnp.zeros((hi, HEADW), BF16), q_j], axis=0))
            pieces = [k_j, jnp.zeros((CHUNK - hi, HEADW), BF16)]
            if lo:
                pieces.insert(0, jnp.zeros((lo, HEADW), BF16))
            k_js.append(jnp.concatenate(pieces, axis=0))
        q_in = (qc * jnp.exp2(b)).astype(BF16)
        k_out = (key * jnp.exp2(b_last - b)).astype(BF16)
        decay = jnp.exp2(b_last)

        s_diag = [_dot_nt(hcols(q_d, h), hcols(k_d, h)) for h in heads]
        s_off = [_dot_nt(jnp.concatenate([hcols(x, h) for x in q_js], axis=1),
                         jnp.concatenate([hcols(x, h) for x in k_js], axis=1)) for h in heads]
        scores = [(jnp.where(diag_ok, s_diag[h], 0.0) + s_off[h]).astype(BF16) for h in heads]
        o_intra = [_dot(scores[h], hcols(vc, h)) for h in heads]
        o_inter = [_dot_nt(hcols(q_in, h), states[h].astype(BF16)) for h in heads]
        upd = [_dot(vt[h * HG_DIM:(h + 1) * HG_DIM, :], hcols(k_out, h)) for h in heads]
        new_states = tuple(states[h] * hcols(decay, h) + upd[h] for h in heads)

        outs = []
        for h in heads:
            o = o_intra[h] + o_inter[h]
            ms = jnp.mean(o * o, axis=-1, keepdims=True)
            outs.append(o * lax.rsqrt(ms + NORM_EPS))
        g = g_ref[sl, :].astype(F32)
        o = jnp.concatenate(outs, axis=1) * nw_ref[...] * (g * _sigmoid(g))
        o_ref[sl, :] = o.astype(o_ref.dtype)
        return new_states

    init = tuple(jnp.zeros((HG_DIM, HG_DIM), F32) for _ in heads)
    lax.fori_loop(0, nchunks, chunk, init, unroll=16)


def _hgrn_call(u3, hg_lb, norm_w, layer):
    bsz, seq, _ = u3.shape
    nl = hg_lb.shape[0]

    def col(base):
        return lambda b: (b, 0, base // HEADW)

    blk = (None, seq, HEADW)
    return pl.pallas_call(
        functools.partial(_hgrn_body, layer=layer, nchunks=seq // CHUNK),
        grid=(bsz,),
        in_specs=[
            pl.BlockSpec((nl, HEADW), lambda b: (0, 0)),
            pl.BlockSpec(blk, col(COL_HF)),
            pl.BlockSpec(blk, col(COL_HI)),
            pl.BlockSpec(blk, col(COL_HQ)),
            pl.BlockSpec(blk, col(COL_HG)),
            pl.BlockSpec((1, HEADW), lambda b: (0, 0)),
        ],
        out_specs=pl.BlockSpec(blk, lambda b: (b, 0, 0)),
        out_shape=jax.ShapeDtypeStruct((bsz, seq, HEADW), BF16),
        compiler_params=_cp(("parallel",)),
        name="hgrn2",
    )(hg_lb, u3, u3, u3, u3, jnp.tile(norm_w, HG_HEADS).reshape(1, HEADW))


def _head_ones(n, width):
    r = _iota((n, n), 0) // width
    c = _iota((n, n), 1) // width
    return (r == c).astype(BF16)


def _rw_prep(cur, prev, prm, vres, has_prev):
    mu_ref, w0_ref, w2_ref, a0_ref, a2_ref, g2_ref, kk_ref, ka_ref = prm
    ts = cur[0].shape[0]
    row = _iota((ts, 1), 0)

    def shifted(idx, lo, hi):
        u = cur[idx][...].astype(F32)
        if prev is None:
            prev_row = jnp.zeros((1, u.shape[1]), F32)
        else:
            prev_row = prev[idx][SUB - 1:SUB, :].astype(F32) * has_prev.astype(F32)
        u_prev = jnp.where(row == 0, prev_row, pltpu.roll(u, 1, 0))
        return u + (u_prev - u) * mu_ref[:, lo:hi]

    r = shifted(0, 0, 512)
    k = shifted(1, 512, 1024)
    v = shifted(2, 1024, 1536)
    lora = shifted(3, 1536, 1792)
    w_lo = lora[:, 0:64]
    a_lo = lora[:, 64:128]
    g_lo = lora[:, 128:256]

    def lora_dot(act, w_ref):
        return _dot(act.astype(BF16), w_ref[...].astype(BF16))

    wpre = w0_ref[...] + lora_dot(jnp.tanh(w_lo), w2_ref)
    lw = -math.exp(-0.5) * _sigmoid(wpre)
    a = _sigmoid(a0_ref[...] + lora_dot(a_lo, a2_ref))
    g = lora_dot(_sigmoid(g_lo), g2_ref)
    if vres is not None:
        vf_ref, v0_ref, v1_ref, v2_ref = vres
        mix = _sigmoid(v0_ref[...] + lora_dot(lora_dot(v, v1_ref), v2_ref))
        v = v + (vf_ref[...].astype(F32) - v) * mix

    kk = k * kk_ref[...]
    ssq = _head_sum(kk * kk)
    kkn = kk * lax.rsqrt(jnp.maximum(ssq, 1e-24))
    k = k * (1.0 + (a - 1.0) * ka_ref[...])
    return r, k, v, lw, kkn, kkn * a, g


def _rw_scan(r_ref, k_ref, v_ref, lw_ref, kk_ref, bk_ref, g_ref, rk_ref, gw_ref, gb_ref,
             o_ref, state_ref, oacc_ref, nsub):
    t = CHUNK
    gw = RW_GW
    tc = nsub * t
    units = [(s, grp) for s in range(nsub) for grp in range(RW_GROUPS)]

    rr = _iota((tc, tc), 0)
    cc = _iota((tc, tc), 1)
    tril_bd = jnp.logical_and(cc <= rr, (cc // t) == (rr // t)).astype(BF16)
    rr4 = _iota((t, gw), 0)
    cj = jnp.bitwise_and(_iota((t, gw), 1), t - 1)
    strict = cj < rr4
    incl = cj <= rr4
    eye_c = (cj == rr4).astype(F32)
    bd_mask = (_iota((gw, gw), 0) // t) == (_iota((gw, gw), 1) // RW_DIM)
    zero_b = jnp.zeros((), BF16)

    left_head = _iota((t, LANES), 1) < RW_DIM
    zeros_tile = jnp.zeros((t, LANES), BF16)

    def bd(xc):
        xb = xc.astype(BF16)
        rows = []
        for h in range(RW_PACK):
            tile = xb[:, (h // 2) * LANES:(h // 2 + 1) * LANES]
            kept = jnp.where(left_head, tile, zero_b) if h % 2 == 0 else jnp.where(left_head, zero_b, tile)
            rows.append(jnp.concatenate([kept, zeros_tile] if h < 2 else [zeros_tile, kept], axis=1))
        return jnp.concatenate(rows, axis=0)

    def cut(x, u):
        s, grp = u
        return x[s * t:(s + 1) * t, grp * gw:(grp + 1) * gw]

    r = r_ref[...].astype(F32)
    k = k_ref[...].astype(F32)
    v = v_ref[...].astype(F32)
    lw = lw_ref[...] * LOG2E
    kk = kk_ref[...].astype(F32)
    bk = bk_ref[...].astype(F32)
    c = _dot_exact_lhs(tril_bd, lw)
    e_neg = jnp.exp2(-c)
    a_t = -kk * jnp.exp2(c - lw)
    b_t = bk * e_neg
    k_t = k * e_neg
    r_t = r * jnp.exp2(c)
    c_last = jnp.concatenate(
        [jnp.broadcast_to(c[(s + 1) * t - 1:(s + 1) * t, :], (t, RW_WIDTH)) for s in range(nsub)], axis=0)
    decay_out = jnp.exp2(c_last - c)
    b_g = bk * decay_out
    k_g = k * decay_out
    gamma = jnp.exp2(c_last)

    ar = {u: jnp.concatenate([cut(a_t, u), cut(r_t, u)], axis=0).astype(BF16) for u in units}
    x_bk = {u: _dot_nt(ar[u], jnp.concatenate([bd(cut(b_t, u)), bd(cut(k_t, u))], axis=0)) for u in units}
    l_c = {u: jnp.where(strict, x_bk[u][:t, :gw], 0.0) for u in units}
    m_c = {u: jnp.where(strict, x_bk[u][:t, gw:], 0.0).astype(BF16) for u in units}
    rb = {u: jnp.where(incl, x_bk[u][t:, :gw], 0.0).astype(BF16) for u in units}
    rkm = {u: jnp.where(incl, x_bk[u][t:, gw:], 0.0).astype(BF16) for u in units}
    v_bd = {u: bd(cut(v, u)) for u in units}

    p = {u: _dot(l_c[u].astype(BF16), bd(l_c[u])) for u in units}
    s_inv = {u: eye_c + l_c[u] for u in units}
    mo = {u: _dot(jnp.concatenate([m_c[u], rkm[u]], axis=0), v_bd[u]) for u in units}
    mv = {u: mo[u][:t] for u in units}
    o_kv = {u: mo[u][t:] for u in units}
    for lvl in range(1, 6):
        last = lvl == 5
        nxt_p, nxt_s = {}, {}
        for u in units:
            if last:
                nxt_s[u] = s_inv[u] + _dot(s_inv[u].astype(BF16), bd(p[u]))
            else:
                res = _dot(jnp.concatenate([s_inv[u], p[u]], axis=0).astype(BF16), bd(p[u]))
                nxt_s[u] = s_inv[u] + res[:t]
                nxt_p[u] = res[t:]
        p, s_inv = nxt_p, nxt_s

    rbs = {u: _dot(rb[u], bd(s_inv[u])) for u in units}
    wu = {u: _dot(jnp.concatenate([s_inv[u], rbs[u]], axis=0).astype(BF16),
                  jnp.concatenate([bd(cut(a_t, u)), bd(mv[u])], axis=1)) for u in units}
    w_t = {u: wu[u][:t, :gw] for u in units}
    u_t = {u: wu[u][:t, gw:] for u in units}
    r_hat = {u: (cut(r_t, u) + wu[u][t:, :gw]).astype(BF16) for u in units}
    o_hat = {u: wu[u][t:, gw:] + o_kv[u] for u in units}
    g_mat = {u: jnp.where(bd_mask, _dot(w_t[u].T.astype(BF16), cut(b_g, u).astype(BF16)), 0.0).astype(BF16)
             for u in units}
    c_mat = {}
    for u in units:
        uv_t = jnp.concatenate([u_t[u], cut(v, u)], axis=0).T.astype(BF16)
        bkg = jnp.concatenate([cut(b_g, u), cut(k_g, u)], axis=0).astype(BF16)
        full = jnp.where(bd_mask, _dot(uv_t, bkg), 0.0)
        c_mat[u] = full[0:t] + full[t:2 * t] + full[2 * t:3 * t] + full[3 * t:4 * t]

    for s in range(nsub):
        for grp in range(RW_GROUPS):
            u = (s, grp)
            st = state_ref[grp]
            st_b = st.astype(BF16)
            oacc_ref[s * t:(s + 1) * t, grp * gw:(grp + 1) * gw] = _dot_nt(r_hat[u], bd(st_b)) + o_hat[u]
            g_row = gamma[(s + 1) * t - 1:(s + 1) * t, grp * gw:(grp + 1) * gw]
            state_ref[grp] = st * g_row + _dot(st_b, g_mat[u]) + c_mat[u]

    o = oacc_ref[...]
    mean = _head_sum(o) * (1.0 / RW_DIM)
    dlt = o - mean
    var = _head_sum(dlt * dlt) * (1.0 / RW_DIM)
    on = dlt * lax.rsqrt(var + RW_GN_EPS) * gw_ref[...] + gb_ref[...]
    bonus = _head_sum(r * k * rk_ref[...])
    o_ref[...] = ((on + bonus * v) * g_ref[...]).astype(o_ref.dtype)


RW_SLOT_DTYPES = (BF16, BF16, BF16, F32, BF16, BF16, BF16)


def _rw_mix_body(*refs, has_vres, emit_v, nsub):
    nxt, nxt_prev, first, prm = refs[0:4], refs[4:8], refs[8:12], refs[12:20]
    pos = 20
    vres_nxt = vres_first = None
    if has_vres:
        vf_nxt, vf_first, v0, v1, v2 = refs[pos:pos + 5]
        vres_nxt, vres_first = (vf_nxt, v0, v1, v2), (vf_first, v0, v1, v2)
        pos += 5
    rk_ref, gw_ref, gb_ref = refs[pos:pos + 3]
    pos += 3
    o_ref = refs[pos]
    pos += 1
    v_out = None
    if emit_v:
        v_out = refs[pos]
        pos += 1
    state_ref, oacc_ref = refs[pos:pos + 2]
    slots = refs[pos + 2:pos + 9]

    j = pl.program_id(1)
    last = pl.num_programs(1) - 1

    def put(slot, vals):
        for ref, val in zip(slots, vals):
            ref[slot] = val.astype(ref.dtype)

    @pl.when(j == 0)
    def _():
        state_ref[...] = jnp.zeros_like(state_ref)
        put(0, _rw_prep(first, None, prm, vres_first, None))

    put((j + 1) % 2, _rw_prep(nxt, nxt_prev, prm, vres_nxt, jnp.minimum(j + 1, last) > 0))

    cur = [ref.at[j % 2] for ref in slots]
    _rw_scan(*cur, rk_ref, gw_ref, gb_ref, o_ref, state_ref, oacc_ref, nsub)
    if emit_v:
        v_out[...] = cur[2][...]


def _rw_mix_call(u3, mu, w0, w2, a0, a2, g2, k_k, k_a, vres, r_k, gn_w, gn_b, emit_v):
    bsz, seq, _ = u3.shape
    tc = min(256, seq)
    n = seq // tc
    w = RW_WIDTH
    has_vres = vres is not None

    def full(arr):
        return pl.BlockSpec(arr.shape, lambda b, j: (0,) * arr.ndim)

    def nxt(col, width):
        return pl.BlockSpec((None, tc, width), lambda b, j: (b, jnp.minimum(j + 1, n - 1), col // width))

    def nxt_prev(col, width):
        return pl.BlockSpec(
            (None, SUB, width),
            lambda b, j: (b, jnp.maximum(jnp.minimum(j + 1, n - 1) * (tc // SUB) - 1, 0), col // width))

    def first(col, width):
        return pl.BlockSpec((None, tc, width), lambda b, j: (b, 0, col // width))

    cols = [(COL_RR, w), (COL_RK, w), (COL_RV, w), (COL_RL, RW_LORA_W)]
    params = [mu.reshape(1, -1), w0.reshape(1, w), w2, a0.reshape(1, w), a2, g2,
              k_k.reshape(1, w), k_a.reshape(1, w)]
    in_specs = [f(c, wd) for f in (nxt, nxt_prev, first) for c, wd in cols] + [full(p) for p in params]
    args = [u3] * 12 + params
    if has_vres:
        v_first, v0, v1, v2 = vres
        extra = [v0.reshape(1, w), v1, v2]
        in_specs += [pl.BlockSpec((None, tc, w), lambda b, j: (b, jnp.minimum(j + 1, n - 1), 0)),
                     pl.BlockSpec((None, tc, w), lambda b, j: (b, 0, 0))] + [full(p) for p in extra]
        args += [v_first, v_first] + extra
    tail = [r_k.reshape(1, w), gn_w.reshape(1, w), gn_b.reshape(1, w)]
    in_specs += [full(p) for p in tail]
    args += tail

    out_spec = pl.BlockSpec((None, tc, w), lambda b, j: (b, j, 0))
    out_shape = jax.ShapeDtypeStruct((bsz, seq, w), BF16)
    scratch = [pltpu.VMEM((RW_GROUPS, RW_DIM, RW_GW), F32), pltpu.VMEM((tc, w), F32)]
    scratch += [pltpu.VMEM((2, tc, w), dt) for dt in RW_SLOT_DTYPES]
    return pl.pallas_call(
        functools.partial(_rw_mix_body, has_vres=has_vres, emit_v=emit_v, nsub=tc // CHUNK),
        grid=(bsz, n),
        in_specs=in_specs,
        out_specs=[out_spec, out_spec] if emit_v else out_spec,
        out_shape=[out_shape, out_shape] if emit_v else out_shape,
        scratch_shapes=scratch,
        compiler_params=_cp(("parallel", "arbitrary")),
        name="rwkv7",
    )(*args)


def _merge_body(ga_ref, gb_ref, gc_ref, oa_ref, ob_ref, oc_ref, x_ref, gate_ref,
                wa_ref, wb_ref, wc_ref, wo_ref, o_ref):
    merged = (_sigmoid(ga_ref[...].astype(F32)) * _dot(oa_ref[...], wa_ref[...])
              + _sigmoid(gb_ref[...].astype(F32)) * _dot(ob_ref[...], wb_ref[...])
              + _sigmoid(gc_ref[...].astype(F32)) * _dot(oc_ref[...], wc_ref[...]))
    mix = _dot(merged.astype(BF16), wo_ref[...])
    o_ref[...] = x_ref[...] + gate_ref[...] * mix


def _merge_call(u2, o_a, o_b, o_c, x2, ada3, wa, wb, wc, wo, layer, bsz, seq):
    m, d = x2.shape
    tm = min(512, seq)
    per_b = seq // tm
    base = layer * bsz * 6
    wdt = o_a.shape[1]

    def const(arr):
        return pl.BlockSpec(arr.shape, lambda i: (0, 0))

    return pl.pallas_call(
        _merge_body,
        grid=(m // tm,),
        in_specs=[
            pl.BlockSpec((tm, d), lambda i: (i, 0)),
            pl.BlockSpec((tm, d), lambda i: (i, 1)),
            pl.BlockSpec((tm, d), lambda i: (i, 2)),
            pl.BlockSpec((tm, wdt), lambda i: (i, 0)),
            pl.BlockSpec((tm, wdt), lambda i: (i, 0)),
            pl.BlockSpec((tm, wdt), lambda i: (i, 0)),
            pl.BlockSpec((tm, d), lambda i: (i, 0)),
            pl.BlockSpec((None, 1, d), lambda i: (base + (i // per_b) * 6 + 2, 0, 0)),
            const(wa), const(wb), const(wc), const(wo),
        ],
        out_specs=pl.BlockSpec((tm, d), lambda i: (i, 0)),
        out_shape=jax.ShapeDtypeStruct((m, d), F32),
        compiler_params=_cp(("parallel",)),
        name="merge_out",
    )(u2, u2, u2, o_a, o_b, o_c, x2, ada3, wa, wb, wc, wo)


def _ffn_body(x_ref, nw_ref, shift_ref, scale_ref, gate_ref, wi_ref, wo_ref, fn_ref, o_ref, *, final):
    x = x_ref[...]
    h = _norm_mod(x, nw_ref[...], scale_ref[...], shift_ref[...]).astype(BF16)
    acc = jnp.zeros(x.shape, F32)
    lo = 0
    for width in FFN_CHUNKS:
        gh = _dot(h, wi_ref[:, lo:lo + width])
        uh = _dot(h, wi_ref[:, FFN_HIDDEN + lo:FFN_HIDDEN + lo + width])
        act = (gh * _sigmoid(gh) * uh).astype(BF16)
        acc = acc + _dot(act, wo_ref[lo:lo + width, :])
        lo += width
    y = x + gate_ref[...] * acc
    if final:
        ms = jnp.mean(y * y, axis=-1, keepdims=True)
        y = y * lax.rsqrt(ms + NORM_EPS) * fn_ref[...]
    o_ref[...] = y


def _ffn_call(x2, nw, ada3, wi, wo, final_w, layer, bsz, seq, final):
    m, d = x2.shape
    tm = min(1024, seq)
    per_b = seq // tm
    base = layer * bsz * 6

    def ada_spec(k):
        return pl.BlockSpec((None, 1, d), lambda i: (base + (i // per_b) * 6 + k, 0, 0))

    def const(arr):
        return pl.BlockSpec(arr.shape, lambda i: (0, 0), pipeline_mode=pl.Buffered(1))

    return pl.pallas_call(
        functools.partial(_ffn_body, final=final),
        grid=(m // tm,),
        in_specs=[
            pl.BlockSpec((tm, d), lambda i: (i, 0)),
            pl.BlockSpec((1, d), lambda i: (0, 0)),
            ada_spec(3), ada_spec(4), ada_spec(5),
            const(wi), const(wo),
            pl.BlockSpec((1, d), lambda i: (0, 0)),
        ],
        out_specs=pl.BlockSpec((tm, d), lambda i: (i, 0)),
        out_shape=jax.ShapeDtypeStruct((m, d), F32),
        compiler_params=_cp(("parallel",)),
        name="ffn_final" if final else "ffn",
    )(x2, nw, ada3, ada3, ada3, wi, wo, final_w)


def _permute_w_in(w):
    def qk_lanes(t):
        t = t.reshape(t.shape[0], DA_HEADS, 2, 2, DA_QK_DIM // 2)
        return jnp.swapaxes(t, 2, 3).reshape(t.shape[0], DA_HEADS * 2 * DA_QK_DIM)

    q, k, v = qk_lanes(w[:, 0:512]), qk_lanes(w[:, 512:1024]), w[:, 1024:1536]
    hgrn = w[:, 1536:3584]
    rw = w[:, 3584:5376]
    gates = w[:, 5376:8448]
    return jnp.concatenate([gates, hgrn, q, k, v, rw], axis=1)


def kernel(x, c, positions, ada_w, ada_b, norm_mix_w, norm_ffn_w, w_in, da_lambda, da_subln_w, hg_lb, hg_norm_w, rw_mu, rw_w0, rw_w2, rw_a0, rw_a2, rw_g2, rw_k_k, rw_k_a, rw_r_k, rw_gn_w, rw_gn_b, rw_v0, rw_v1, rw_v2, w_branch_a, w_branch_b, w_branch_c, w_out, ffn_w_in, ffn_w_out, final_norm_w):
    bsz, seq, d = x.shape
    depth = ada_w.shape[0]
    m = bsz * seq

    ada = _ada_call(c, ada_w, ada_b)
    ada3 = ada.reshape(depth * bsz * 6, 1, d)
    cos, sin_s = _rope_call(positions)

    x2 = x.reshape(m, d)
    v_first = None
    for l in range(depth):
        w_l = _permute_w_in(w_in[l]).astype(BF16)
        u2 = _proj_in_call(x2, norm_mix_w[l].reshape(1, d), ada3, cos, sin_s, w_l, l, bsz, seq)
        u3 = u2.reshape(bsz, seq, IN_COLS)

        o_a = _attn_call(u3, da_lambda[l], da_subln_w[l], l)
        o_b = _hgrn_call(u3, hg_lb, hg_norm_w[l], l)
        vres = None if l == 0 else (v_first, rw_v0[l - 1], rw_v1[l - 1], rw_v2[l - 1])
        emit_v = l == 0 and depth > 1
        res = _rw_mix_call(u3, rw_mu[l], rw_w0[l], rw_w2[l], rw_a0[l], rw_a2[l], rw_g2[l], rw_k_k[l],
                           rw_k_a[l], vres, rw_r_k[l], rw_gn_w[l], rw_gn_b[l], emit_v)
        if emit_v:
            o_c, v_first = res
        else:
            o_c = res

        x2 = _merge_call(u2, o_a.reshape(m, -1), o_b.reshape(m, -1), o_c.reshape(m, -1), x2, ada3,
                         w_branch_a[l].astype(BF16), w_branch_b[l].astype(BF16),
                         w_branch_c[l].astype(BF16), w_out[l].astype(BF16), l, bsz, seq)
        x2 = _ffn_call(x2, norm_ffn_w[l].reshape(1, d), ada3, ffn_w_in[l].astype(BF16),
                       ffn_w_out[l].astype(BF16), final_norm_w.reshape(1, d), l, bsz, seq,
                       final=(l == depth - 1))
    return x2.reshape(bsz, seq, d)
```
